```python
import math
import jax, jax.numpy as jnp
from jax import lax
import numpy as np

D_MODEL = 2048
BATCH = 4
SEQ = 4096
DEPTH = 2

N_MIXERS = 2
HEAD_DIM = 128
N_MIX_HEADS = (3 * D_MODEL) // (4 * HEAD_DIM)
MIX_WIDTH = N_MIX_HEADS * HEAD_DIM
N_MEM_HEADS = 4
MEM_WIDTH = N_MEM_HEADS * HEAD_DIM
MEM_TOKENS = 256
IN_WIDTH = 3 * MIX_WIDTH + MEM_WIDTH
OUT_WIDTH = MIX_WIDTH + MEM_WIDTH
D_FF = 4 * D_MODEL
DIFF_HALF = HEAD_DIM // 2
DILATED_GROUPS = ((128, 1), (512, 4), (2048, 16))
HEADS_PER_GROUP = N_MIX_HEADS // len(DILATED_GROUPS)
QBLOCK = 128
NORM_EPS = 1e-6

kernel_name = "hybrid_diffattn_dilated_memory_block"


def _alibi_slopes(n):
    def pow2(m):
        start = 2.0 ** (-(2.0 ** -(math.log2(m) - 3)))
        return [start * start ** i for i in range(m)]
    def slopes(m):
        if math.log2(m).is_integer():
            return pow2(m)
        c = 2 ** math.floor(math.log2(m))
        return pow2(c) + slopes(2 * c)[0::2][: m - c]
    return jnp.asarray(sorted(slopes(n), reverse=True), dtype=jnp.float32)


def _rmsnorm(x, g):
    xf = x.astype(jnp.float32)
    y = xf * lax.rsqrt(jnp.mean(xf * xf, axis=-1, keepdims=True) + NORM_EPS)
    return (y * g.astype(jnp.float32)).astype(x.dtype)


def _heads(t, n_heads):
    b, s, _ = t.shape
    return t.reshape(b, s, n_heads, -1).transpose(0, 2, 1, 3)


def _diff_attention(q1, q2, k1, k2, v, lam, slopes):
    b, h, s, d = q1.shape
    scale = d ** -0.5
    kpos = jnp.arange(s)
    def block(i):
        q0 = i * QBLOCK
        qa = lax.dynamic_slice_in_dim(q1, q0, QBLOCK, axis=2)
        qb = lax.dynamic_slice_in_dim(q2, q0, QBLOCK, axis=2)
        dist = (q0 + jnp.arange(QBLOCK))[:, None] - kpos[None, :]
        bias = jnp.where(dist >= 0, -slopes[:, None, None] * dist.astype(jnp.float32), -jnp.inf)
        s1 = jnp.einsum('bhqd,bhkd->bhqk', qa, k1, preferred_element_type=jnp.float32) * scale + bias
        s2 = jnp.einsum('bhqd,bhkd->bhqk', qb, k2, preferred_element_type=jnp.float32) * scale + bias
        a = (jax.nn.softmax(s1, axis=-1) - lam * jax.nn.softmax(s2, axis=-1)).astype(v.dtype)
        return jnp.einsum('bhqk,bhkd->bhqd', a, v)
    out = lax.map(block, jnp.arange(s // QBLOCK))
    return out.transpose(1, 2, 0, 3, 4).reshape(b, h, s, v.shape[-1])


def _dilated_group(q, k, v, window, dilation, slopes):
    b, h, s, dh = q.shape
    n_keys = window // dilation + 1
    offs = jnp.arange(n_keys) * dilation
    scale = dh ** -0.5
    bias = -slopes[:, None, None] * offs.astype(jnp.float32)[None, None, :]
    def block(i):
        q0 = i * QBLOCK
        qb = lax.dynamic_slice_in_dim(q, q0, QBLOCK, axis=2)
        kidx = (q0 + jnp.arange(QBLOCK))[:, None] - offs[None, :]
        valid = kidx >= 0
        kidx = jnp.maximum(kidx, 0)
        kb = k[:, :, kidx]
        vb = v[:, :, kidx]
        sc = jnp.einsum('bhqd,bhqjd->bhqj', qb, kb, preferred_element_type=jnp.float32) * scale + bias
        sc = jnp.where(valid, sc, -jnp.inf)
        lse = jax.nn.logsumexp(sc, axis=-1)
        p = jnp.exp(sc - lse[..., None]).astype(v.dtype)
        return jnp.einsum('bhqj,bhqjd->bhqd', p, vb), lse
    o, lse = lax.map(block, jnp.arange(s // QBLOCK))
    o = o.transpose(1, 2, 0, 3, 4).reshape(b, h, s, dh)
    lse = lse.transpose(1, 2, 0, 3).reshape(b, h, s)
    return o, lse


def _dilated_attention(q, k, v, slopes):
    outs, lses = [], []
    for g, (window, dilation) in enumerate(DILATED_GROUPS):
        sl = slice(g * HEADS_PER_GROUP, (g + 1) * HEADS_PER_GROUP)
        o, lse = _dilated_group(q[:, sl], k[:, sl], v[:, sl], window, dilation, slopes[sl])
        outs.append(o)
        lses.append(lse)
    alpha = jax.nn.softmax(jnp.stack(lses, axis=0), axis=0)
    return jnp.concatenate([o * alpha[g][..., None].astype(o.dtype) for g, o in enumerate(outs)], axis=1)


def _memory_attention(qm, km, vm):
    sc = jnp.einsum('bhqd,bhmd->bhqm', qm, km, preferred_element_type=jnp.float32) * (qm.shape[-1] ** -0.5)
    p = jax.nn.softmax(sc, axis=-1).astype(vm.dtype)
    return jnp.einsum('bhqm,bhmd->bhqd', p, vm)


def setup_inputs(seed: int = 0) -> dict:
    key = jax.random.key(seed)
    ks = jax.random.split(key, 14)
    n_diff = (DEPTH + N_MIXERS - 1) // N_MIXERS
    f32 = jnp.float32
    def nrm(k, shape, scale):
        return jax.random.normal(k, shape, f32) * scale
    return {
        "x": nrm(ks[0], (BATCH, SEQ, D_MODEL), 1.0),
        "mem": nrm(ks[1], (BATCH, MEM_TOKENS, D_MODEL), 1.0),
        "g_attn": 1.0 + nrm(ks[2], (DEPTH, D_MODEL), 0.02),
        "w_in": nrm(ks[3], (DEPTH, D_MODEL, IN_WIDTH), D_MODEL ** -0.5),
        "w_out": nrm(ks[4], (DEPTH, OUT_WIDTH, D_MODEL), OUT_WIDTH ** -0.5),
        "lambda_qk": nrm(ks[5], (n_diff, 4, DIFF_HALF), 0.1),
        "diff_subln_g": 1.0 + nrm(ks[6], (n_diff, HEAD_DIM), 0.02),
        "g_mem": 1.0 + nrm(ks[7], (D_MODEL,), 0.02),
        "w_mem_kv": nrm(ks[8], (DEPTH, D_MODEL, 2 * MEM_WIDTH), D_MODEL ** -0.5),
        "g_mlp": 1.0 + nrm(ks[9], (DEPTH, D_MODEL), 0.02),
        "w_mlp1": nrm(ks[10], (DEPTH, D_MODEL, D_FF), D_MODEL ** -0.5),
        "w_mlp2": nrm(ks[11], (DEPTH, D_FF, D_MODEL), D_FF ** -0.5),
        "g_final": 1.0 + nrm(ks[12], (D_MODEL,), 0.02),
    }


def reference(x, mem, g_attn, w_in, w_out, lambda_qk, diff_subln_g, g_mem, w_mem_kv,
              g_mlp, w_mlp1, w_mlp2, g_final):
    b, s, d_model = x.shape
    slopes = _alibi_slopes(N_MIX_HEADS)
    memn = _rmsnorm(mem, g_mem)
    for i in range(DEPTH):
        h = _rmsnorm(x, g_attn[i])
        proj = h @ w_in[i]
        q, k, v, qm = jnp.split(proj, [MIX_WIDTH, 2 * MIX_WIDTH, 3 * MIX_WIDTH], axis=-1)
        q, k, v = _heads(q, N_MIX_HEADS), _heads(k, N_MIX_HEADS), _heads(v, N_MIX_HEADS)
        if i % N_MIXERS == 0:
            j = i // N_MIXERS
            lam_init = 0.8 - 0.6 * math.exp(-0.3 * i)
            lp = lambda_qk[j].astype(jnp.float32)
            lam = jnp.exp(jnp.sum(lp[0] * lp[1])) - jnp.exp(jnp.sum(lp[2] * lp[3])) + lam_init
            o = _diff_attention(q[..., :DIFF_HALF], q[..., DIFF_HALF:],
                                k[..., :DIFF_HALF], k[..., DIFF_HALF:], v, lam, slopes)
            o = _rmsnorm(o, diff_subln_g[j]) * (1.0 - lam_init)
        else:
            o = _dilated_attention(q, k, v, slopes)
        km, vm = jnp.split(memn @ w_mem_kv[i], 2, axis=-1)
        om = _memory_attention(_heads(qm, N_MEM_HEADS), _heads(km, N_MEM_HEADS), _heads(vm, N_MEM_HEADS))
        heads = jnp.concatenate([o, om], axis=1)
        merged = heads.transpose(0, 2, 1, 3).reshape(b, s, OUT_WIDTH)
        x = x + merged @ w_out[i]
        hm = _rmsnorm(x, g_mlp[i])
        x = x + jnp.square(jax.nn.relu(hm @ w_mlp1[i])) @ w_mlp2[i]
    return _rmsnorm(x, g_final)
```

```python
import functools
import math

import numpy as np
import jax
import jax.numpy as jnp
from jax import lax
from jax.experimental import pallas as pl
from jax.experimental.pallas import tpu as pltpu

HEAD_DIM = 128
N_MEM_HEADS = 4
N_MIXERS = 2
DILATED_GROUPS = ((128, 1), (512, 4), (2048, 16))
HEADS_PER_GROUP = 4
NORM_EPS = 1e-6
NEG = -1e30
LANES = 128
VMEM_LIMIT = 56 * 1024 * 1024

_F32 = jnp.float32
_BF16 = jnp.bfloat16
_NT = (((1,), (1,)), ((), ()))


def _alibi_slopes(n):
    def pow2(m):
        start = 2.0 ** (-(2.0 ** -(math.log2(m) - 3)))
        return [start * start ** i for i in range(m)]

    def slopes(m):
        if math.log2(m).is_integer():
            return pow2(m)
        c = 2 ** math.floor(math.log2(m))
        return pow2(c) + slopes(2 * c)[0::2][: m - c]

    return np.asarray(sorted(slopes(n), reverse=True), dtype=np.float32)


def _cparams(semantics):
    return pltpu.CompilerParams(dimension_semantics=semantics, vmem_limit_bytes=VMEM_LIMIT)


def _rms(x, g):
    ms = jnp.mean(x * x, axis=-1, keepdims=True)
    return x * lax.rsqrt(ms + NORM_EPS) * g


def _norm_matmul_kernel(x_ref, g_ref, w_ref, o_ref, hn_ref):
    @pl.when(pl.program_id(1) == 0)
    def _():
        hn_ref[...] = _rms(x_ref[...], g_ref[...]).astype(hn_ref.dtype)

    o_ref[...] = jnp.dot(hn_ref[...], w_ref[...], preferred_element_type=_F32).astype(o_ref.dtype)


def _norm_matmul(x, g, w, *, tm, tn):
    m, k = x.shape
    n = w.shape[1]
    return pl.pallas_call(
        _norm_matmul_kernel,
        grid=(m // tm, n // tn),
        in_specs=[
            pl.BlockSpec((tm, k), lambda i, j: (i, 0)),
            pl.BlockSpec((1, k), lambda i, j: (0, 0)),
            pl.BlockSpec((k, tn), lambda i, j: (0, j)),
        ],
        out_specs=pl.BlockSpec((tm, tn), lambda i, j: (i, j)),
        out_shape=jax.ShapeDtypeStruct((m, n), _BF16),
        scratch_shapes=[pltpu.VMEM((tm, k), _BF16)],
        compiler_params=_cparams(("parallel", "arbitrary")),
        name="norm_matmul",
    )(x, g.reshape(1, k), w)


def _diff_attn_kernel(lam_ref, slope_ref, g_ref, q_ref, k_ref, v_ref, o_ref,
                      m_ref, l_ref, acc_ref, *, tq, lam_init):
    qi = pl.program_id(2)
    tk = tq
    half = HEAD_DIM // 2

    lp = lam_ref[...]
    lam = (jnp.exp(jnp.sum(lp[0:1] * lp[1:2], axis=1, keepdims=True))
           - jnp.exp(jnp.sum(lp[2:3] * lp[3:4], axis=1, keepdims=True)) + lam_init)

    q = q_ref[...] * jnp.asarray(half ** -0.5, _BF16)
    lane = lax.broadcasted_iota(jnp.int32, q.shape, 1)
    zero = jnp.zeros_like(q)
    qz = jnp.concatenate([jnp.where(lane < half, q, zero), jnp.where(lane >= half, q, zero)], axis=0)

    slope = slope_ref[...][:, :1]
    col = lax.broadcasted_iota(jnp.int32, (1, tk), 1).astype(_F32)
    row_i = lax.broadcasted_iota(jnp.int32, (2 * tq, tk), 0)
    col_i = lax.broadcasted_iota(jnp.int32, (2 * tq, tk), 1)
    causal = col_i <= jnp.where(row_i >= tq, row_i - tq, row_i)

    m_ref[...] = jnp.full(m_ref.shape, NEG, _F32)
    l_ref[...] = jnp.zeros(l_ref.shape, _F32)
    acc_ref[...] = jnp.zeros(acc_ref.shape, _F32)

    def step(kj, masked):
        start = pl.multiple_of(kj * tk, tk)
        k = k_ref[pl.ds(start, tk), :]
        v = v_ref[pl.ds(start, tk), :]
        s = lax.dot_general(qz, k, _NT, preferred_element_type=_F32)
        off = ((kj - qi) * tk).astype(_F32)
        s = s + slope * (col + off)
        if masked:
            s = jnp.where(causal, s, NEG)
        m_prev = m_ref[...]
        m_new = jnp.maximum(m_prev, jnp.max(s, axis=1, keepdims=True))
        alpha = jnp.exp(m_prev - m_new)
        p = jnp.exp(s - jnp.concatenate([m_new] * (tk // LANES), axis=1))
        l_ref[...] = alpha * l_ref[...] + jnp.sum(p, axis=1, keepdims=True)
        acc_ref[...] = acc_ref[...] * alpha + jnp.dot(p.astype(_BF16), v, preferred_element_type=_F32)
        m_ref[...] = m_new

    def body(kj, carry):
        step(kj, False)
        return carry

    lax.fori_loop(0, qi, body, 0)
    step(qi, True)

    o = acc_ref[...] / l_ref[...]
    od = o[:tq] - lam * o[tq:]
    o_ref[...] = (_rms(od, g_ref[...]) * (1.0 - lam_init)).astype(o_ref.dtype)


def _diff_attention(proj, lam_params, subln_g, slopes, *, batch, seq, n_heads, lam_init, tq):
    nq = seq // tq
    hd = HEAD_DIM
    slope_arr = jnp.broadcast_to(jnp.asarray(slopes).reshape(n_heads, 1, 1), (n_heads, 1, hd))
    kern = functools.partial(_diff_attn_kernel, tq=tq, lam_init=lam_init)
    return pl.pallas_call(
        kern,
        grid=(batch, n_heads, nq),
        in_specs=[
            pl.BlockSpec(lam_params.shape, lambda b, h, i: (0, 0)),
            pl.BlockSpec((None, 1, hd), lambda b, h, i: (h, 0, 0)),
            pl.BlockSpec((1, hd), lambda b, h, i: (0, 0)),
            pl.BlockSpec((tq, hd), lambda b, h, i: (b * nq + i, h)),
            pl.BlockSpec((seq, hd), lambda b, h, i: (b, n_heads + h)),
            pl.BlockSpec((seq, hd), lambda b, h, i: (b, 2 * n_heads + h)),
        ],
        out_specs=pl.BlockSpec((tq, hd), lambda b, h, i: (b * nq + i, h)),
        out_shape=jax.ShapeDtypeStruct((batch * seq, n_heads * hd), _BF16),
        scratch_shapes=[
            pltpu.VMEM((2 * tq, hd), _F32),
            pltpu.VMEM((2 * tq, hd), _F32),
            pltpu.VMEM((2 * tq, hd), _F32),
        ],
        compiler_params=_cparams(("parallel", "parallel", "arbitrary")),
        name="diff_attention",
    )(lam_params, slope_arr, subln_g.reshape(1, hd), proj, proj, proj)


def _dilated_kernel(q_ref, kp_ref, kc_ref, vp_ref, vc_ref, o_ref, lse_ref, *, tq, span, slopes, scale):
    ui = pl.program_id(2)
    nk = span + tq
    row = lax.broadcasted_iota(jnp.int32, (tq, nk), 0)
    col = lax.broadcasted_iota(jnp.int32, (tq, nk), 1)
    dist = row - col + span
    valid = (dist >= 0) & (dist <= span) & ((col >= span) | (ui > 0))
    distf = dist.astype(_F32)
    for hh, slope_step in enumerate(slopes):
        sl = slice(hh * HEAD_DIM, (hh + 1) * HEAD_DIM)
        q = q_ref[:, sl]
        sp = lax.dot_general(q, kp_ref[:, sl], _NT, preferred_element_type=_F32)
        sc = lax.dot_general(q, kc_ref[:, sl], _NT, preferred_element_type=_F32)
        s = jnp.concatenate([sp, sc], axis=1) * scale - slope_step * distf
        s = jnp.where(valid, s, NEG)
        m = jnp.max(s, axis=1, keepdims=True)
        p = jnp.exp(s - m)
        l = jnp.sum(p, axis=1, keepdims=True)
        pb = p.astype(_BF16)
        o = (jnp.dot(pb[:, :span], vp_ref[:, sl], preferred_element_type=_F32)
             + jnp.dot(pb[:, span:], vc_ref[:, sl], preferred_element_type=_F32))
        o_ref[:, sl] = (o / l).astype(o_ref.dtype)
        lse_ref[:, sl] = jnp.broadcast_to(m + jnp.log(l), (tq, HEAD_DIM))


def _dilated_group(proj, slopes_g, *, batch, seq, group, window, dilation, in_width, mix_width):
    span = window // dilation
    assert span == LANES, "key window per stream must be one 128-row block"
    stream = seq // dilation
    tq = min(256, stream)
    nu = stream // tq
    gw = HEADS_PER_GROUP * HEAD_DIM
    cpr = in_width // gw
    assert in_width % gw == 0 and mix_width % gw == 0
    koff = mix_width // gw
    projv = proj.reshape(batch * stream, dilation * in_width)
    prev_per_cur = tq // span

    def cur(c):
        return lambda b, r, u: (b * nu + u, r * cpr + c)

    def prev(c):
        return lambda b, r, u: (b * (stream // span) + jnp.maximum(u * prev_per_cur - 1, 0), r * cpr + c)

    kern = functools.partial(
        _dilated_kernel, tq=tq, span=span,
        slopes=tuple(float(s) * dilation for s in slopes_g), scale=HEAD_DIM ** -0.5)
    o, lse = pl.pallas_call(
        kern,
        grid=(batch, dilation, nu),
        in_specs=[
            pl.BlockSpec((tq, gw), cur(group)),
            pl.BlockSpec((span, gw), prev(koff + group)),
            pl.BlockSpec((tq, gw), cur(koff + group)),
            pl.BlockSpec((span, gw), prev(2 * koff + group)),
            pl.BlockSpec((tq, gw), cur(2 * koff + group)),
        ],
        out_specs=[
            pl.BlockSpec((tq, gw), lambda b, r, u: (b * nu + u, r)),
            pl.BlockSpec((tq, gw), lambda b, r, u: (b * nu + u, r)),
        ],
        out_shape=[
            jax.ShapeDtypeStruct((batch * stream, dilation * gw), _BF16),
            jax.ShapeDtypeStruct((batch * stream, dilation * gw), _F32),
        ],
        compiler_params=_cparams(("parallel", "parallel", "arbitrary")),
        name=f"dilated_attention_d{dilation}",
    )(projv, projv, projv, projv, projv)
    return o.reshape(batch * seq, gw), lse.reshape(batch * seq, gw)


def _combine_kernel(o0_ref, o1_ref, o2_ref, l0_ref, l1_ref, l2_ref, out_ref):
    l0, l1, l2 = l0_ref[...], l1_ref[...], l2_ref[...]
    mx = jnp.maximum(jnp.maximum(l0, l1), l2)
    e0, e1, e2 = jnp.exp(l0 - mx), jnp.exp(l1 - mx), jnp.exp(l2 - mx)
    den = e0 + e1 + e2
    gw = o0_ref.shape[1]
    for g, (o_ref, e) in enumerate(((o0_ref, e0), (o1_ref, e1), (o2_ref, e2))):
        out_ref[:, g * gw:(g + 1) * gw] = (o_ref[...].astype(_F32) * (e / den)).astype(out_ref.dtype)


def _combine_groups(outs, lses, *, tm):
    m, gw = outs[0].shape
    spec = pl.BlockSpec((tm, gw), lambda i: (i, 0))
    return pl.pallas_call(
        _combine_kernel,
        grid=(m // tm,),
        in_specs=[spec] * 6,
        out_specs=pl.BlockSpec((tm, 3 * gw), lambda i: (i, 0)),
        out_shape=jax.ShapeDtypeStruct((m, 3 * gw), _BF16),
        compiler_params=_cparams(("parallel",)),
        name="combine_groups",
    )(*outs, *lses)


def _mem_attn_kernel(q_ref, k_ref, v_ref, o_ref, *, scale):
    for hh in range(N_MEM_HEADS):
        sl = slice(hh * HEAD_DIM, (hh + 1) * HEAD_DIM)
        s = lax.dot_general(q_ref[:, sl], k_ref[:, sl], _NT, preferred_element_type=_F32) * scale
        m = jnp.max(s, axis=1, keepdims=True)
        p = jnp.exp(s - m)
        l = jnp.sum(p, axis=1, keepdims=True)
        o = jnp.dot(p.astype(_BF16), v_ref[:, sl], preferred_element_type=_F32)
        o_ref[:, sl] = (o / l).astype(o_ref.dtype)


def _memory_attention(proj, kvm, *, batch, seq, mem_tokens, q_col_block, tq):
    nq = seq // tq
    mw = N_MEM_HEADS * HEAD_DIM
    return pl.pallas_call(
        functools.partial(_mem_attn_kernel, scale=HEAD_DIM ** -0.5),
        grid=(batch, nq),
        in_specs=[
            pl.BlockSpec((tq, mw), lambda b, i: (b * nq + i, q_col_block)),
            pl.BlockSpec((mem_tokens, mw), lambda b, i: (b, 0)),
            pl.BlockSpec((mem_tokens, mw), lambda b, i: (b, 1)),
        ],
        out_specs=pl.BlockSpec((tq, mw), lambda b, i: (b * nq + i, 0)),
        out_shape=jax.ShapeDtypeStruct((batch * seq, mw), _BF16),
        compiler_params=_cparams(("parallel", "parallel")),
        name="memory_attention",
    )(proj, kvm, kvm)


def _out_proj_kernel(x_ref, a_ref, b_ref, wa_ref, wb_ref, o_ref):
    acc = jnp.dot(a_ref[...], wa_ref[...], preferred_element_type=_F32)
    acc = acc + jnp.dot(b_ref[...], wb_ref[...], preferred_element_type=_F32)
    o_ref[...] = x_ref[...] + acc


def _out_proj(x, o_mix, o_mem, w_mix, w_mem, *, tm):
    m, d = x.shape
    ka, kb = o_mix.shape[1], o_mem.shape[1]
    return pl.pallas_call(
        _out_proj_kernel,
        grid=(m // tm,),
        in_specs=[
            pl.BlockSpec((tm, d), lambda i: (i, 0)),
            pl.BlockSpec((tm, ka), lambda i: (i, 0)),
            pl.BlockSpec((tm, kb), lambda i: (i, 0)),
            pl.BlockSpec((ka, d), lambda i: (0, 0)),
            pl.BlockSpec((kb, d), lambda i: (0, 0)),
        ],
        out_specs=pl.BlockSpec((tm, d), lambda i: (i, 0)),
        out_shape=jax.ShapeDtypeStruct((m, d), _F32),
        compiler_params=_cparams(("parallel",)),
        name="out_proj",
    )(x, o_mix, o_mem, w_mix, w_mem)


def _mlp_kernel(x_ref, g_ref, w1_ref, w2_ref, gf_ref, o_ref, hn_ref, *, final_norm):
    f = pl.program_id(1)

    @pl.when(f == 0)
    def _():
        x = x_ref[...]
        hn_ref[...] = _rms(x, g_ref[...]).astype(hn_ref.dtype)
        o_ref[...] = x

    a = jnp.dot(hn_ref[...], w1_ref[...], preferred_element_type=_F32)
    a = jnp.square(jnp.maximum(a, 0.0)).astype(_BF16)
    o_ref[...] += jnp.dot(a, w2_ref[...], preferred_element_type=_F32)

    if final_norm:
        @pl.when(f == pl.num_programs(1) - 1)
        def _():
            o_ref[...] = _rms(o_ref[...], gf_ref[...])


def _mlp(x, g, w1, w2, g_final, *, final_norm, tm, tf):
    m, d = x.shape
    ff = w1.shape[1]
    return pl.pallas_call(
        functools.partial(_mlp_kernel, final_norm=final_norm),
        grid=(m // tm, ff // tf),
        in_specs=[
            pl.BlockSpec((tm, d), lambda i, f: (i, 0)),
            pl.BlockSpec((1, d), lambda i, f: (0, 0)),
            pl.BlockSpec((d, tf), lambda i, f: (0, f)),
            pl.BlockSpec((tf, d), lambda i, f: (f, 0)),
            pl.BlockSpec((1, d), lambda i, f: (0, 0)),
        ],
        out_specs=pl.BlockSpec((tm, d), lambda i, f: (i, 0)),
        out_shape=jax.ShapeDtypeStruct((m, d), _F32),
        scratch_shapes=[pltpu.VMEM((tm, d), _BF16)],
        compiler_params=_cparams(("parallel", "arbitrary")),
        name="mlp",
    )(x, g.reshape(1, d), w1, w2, g_final.reshape(1, d))


def kernel(x, mem, g_attn, w_in, w_out, lambda_qk, diff_subln_g, g_mem, w_mem_kv, g_mlp, w_mlp1, w_mlp2, g_final):
    batch, seq, d_model = x.shape
    depth = w_in.shape[0]
    mem_tokens = mem.shape[1]
    in_width = w_in.shape[2]
    mem_width = N_MEM_HEADS * HEAD_DIM
    mix_width = (in_width - mem_width) // 3
    n_heads = mix_width // HEAD_DIM
    slopes = _alibi_slopes(n_heads)

    xf = x.reshape(batch * seq, d_model)
    memf = mem.reshape(batch * mem_tokens, d_model)
    for i in range(depth):
        proj = _norm_matmul(xf, g_attn[i], w_in[i].astype(_BF16), tm=512, tn=1024)
        kvm = _norm_matmul(memf, g_mem, w_mem_kv[i].astype(_BF16), tm=512, tn=512)
        if i % N_MIXERS == 0:
            j = i // N_MIXERS
            lam_init = 0.8 - 0.6 * math.exp(-0.3 * i)
            o_mix = _diff_attention(proj, lambda_qk[j], diff_subln_g[j], slopes, batch=batch, seq=seq,
                                    n_heads=n_heads, lam_init=lam_init, tq=256)
        else:
            outs, lses = [], []
            for g, (window, dilation) in enumerate(DILATED_GROUPS):
                sl = slice(g * HEADS_PER_GROUP, (g + 1) * HEADS_PER_GROUP)
                o, lse = _dilated_group(proj, slopes[sl], batch=batch, seq=seq, group=g, window=window,
                                        dilation=dilation, in_width=in_width, mix_width=mix_width)
                outs.append(o)
                lses.append(lse)
            o_mix = _combine_groups(outs, lses, tm=1024)
        o_mem = _memory_attention(proj, kvm, batch=batch, seq=seq, mem_tokens=mem_tokens,
                                  q_col_block=3 * mix_width // mem_width, tq=512)
        w_o = w_out[i].astype(_BF16)
        xf = _out_proj(xf, o_mix, o_mem, w_o[:mix_width], w_o[mix_width:], tm=512)
        xf = _mlp(xf, g_mlp[i], w_mlp1[i].astype(_BF16), w_mlp2[i].astype(_BF16), g_final,
                  final_norm=(i == depth - 1), tm=1024, tf=512)
    return xf.reshape(batch, seq, d_model)
```

```python
import functools
import math

import numpy as np
import jax
import jax.numpy as jnp
from jax import lax
from jax.experimental import pallas as pl
from jax.experimental.pallas import tpu as pltpu

HEAD_DIM = 128
N_MEM_HEADS = 4
N_MIXERS = 2
DILATED_GROUPS = ((128, 1), (512, 4), (2048, 16))
HEADS_PER_GROUP = 4
NORM_EPS = 1e-6
NEG = -1e30
LANES = 128
VMEM_LIMIT = 56 * 1024 * 1024
PERM_ROWS = 512
LOG2E = math.log2(math.e)

_F32 = jnp.float32
_BF16 = jnp.bfloat16
_NT = (((1,), (1,)), ((), ()))


def _alibi_slopes(n):
    def pow2(m):
        start = 2.0 ** (-(2.0 ** -(math.log2(m) - 3)))
        return [start * start ** i for i in range(m)]

    def slopes(m):
        if math.log2(m).is_integer():
            return pow2(m)
        c = 2 ** math.floor(math.log2(m))
        return pow2(c) + slopes(2 * c)[0::2][: m - c]

    return np.asarray(sorted(slopes(n), reverse=True), dtype=np.float32)


def _cparams(semantics):
    return pltpu.CompilerParams(dimension_semantics=semantics, vmem_limit_bytes=VMEM_LIMIT)


def _rms(x, g):
    ms = jnp.mean(x * x, axis=-1, keepdims=True)
    return x * lax.rsqrt(ms + NORM_EPS) * g


def _norm_matmul_kernel(x_ref, g_ref, cs_ref, w_ref, o_ref, xn_ref, hn_ref, *, dilations, tile_slot):
    j = pl.program_id(1)
    tm = x_ref.shape[0]

    @pl.when(j == 0)
    def _():
        xn = _rms(x_ref[...], g_ref[...])
        hn_ref[0] = xn.astype(hn_ref.dtype)
        if len(dilations) == 1:
            return
        for c in range(xn_ref.shape[0]):
            xn_ref[c] = xn[:, c * LANES:(c + 1) * LANES]

        def permute_chunk(c, carry):
            lanes = pl.ds(pl.multiple_of(c * LANES, LANES), LANES)
            for s, d in enumerate(dilations):
                if d == 1:
                    continue
                n = PERM_ROWS // d
                for grp in range(tm // PERM_ROWS):
                    for r in range(d):
                        rows = xn_ref[c, pl.ds(grp * PERM_ROWS + r, n, stride=d), :]
                        dst = grp * PERM_ROWS + r * n
                        hn_ref[s, dst:dst + n, lanes] = rows.astype(hn_ref.dtype)
            return carry

        lax.fori_loop(0, xn_ref.shape[0], permute_chunk, 0)

    slot = tile_slot(j) if len(dilations) > 1 else 0
    acc = jnp.dot(hn_ref[slot], w_ref[...], preferred_element_type=_F32)
    o_ref[...] = (acc * cs_ref[...]).astype(o_ref.dtype)


def _norm_matmul(x, g, w, col_scale, *, tm, tn, dilations=(1,), tile_slot=None):
    m, k = x.shape
    n = w.shape[1]
    xn_shape = (k // LANES, tm, LANES) if len(dilations) > 1 else (1, 8, LANES)
    return pl.pallas_call(
        functools.partial(_norm_matmul_kernel, dilations=dilations, tile_slot=tile_slot),
        grid=(m // tm, n // tn),
        in_specs=[
            pl.BlockSpec((tm, k), lambda i, j: (i, 0)),
            pl.BlockSpec((1, k), lambda i, j: (0, 0)),
            pl.BlockSpec((1, tn), lambda i, j: (0, j)),
            pl.BlockSpec((k, tn), lambda i, j: (0, j)),
        ],
        out_specs=pl.BlockSpec((tm, tn), lambda i, j: (i, j)),
        out_shape=jax.ShapeDtypeStruct((m, n), _BF16),
        scratch_shapes=[pltpu.VMEM(xn_shape, _F32), pltpu.VMEM((len(dilations), tm, k), _BF16)],
        compiler_params=_cparams(("parallel", "arbitrary")),
        name="norm_matmul",
    )(x, g.reshape(1, k), col_scale.reshape(1, n), w)


def _diff_attn_kernel(lam_ref, sig_ref, sigp_ref, g_ref, q_ref, k_ref, v_ref, o_ref,
                      vt_ref, kc_ref, m_ref, acc_ref, *, tq, tk, hb, lam_init):
    qi = pl.program_id(2)
    hd = HEAD_DIM
    half = hd // 2
    ones_rows = vt_ref.shape[1] - hd

    lp = lam_ref[...]
    lam = (jnp.exp(jnp.sum(lp[0:1] * lp[1:2], axis=1, keepdims=True))
           - jnp.exp(jnp.sum(lp[2:3] * lp[3:4], axis=1, keepdims=True)) + lam_init)

    @pl.when(qi == 0)
    def _():
        seq = v_ref.shape[0]
        for hh in range(hb):
            for c in range(seq // 512):
                vt_ref[hh, :hd, c * 512:(c + 1) * 512] = (
                    v_ref[c * 512:(c + 1) * 512, hh * hd:(hh + 1) * hd].astype(_F32).T.astype(_BF16))
            vt_ref[hh, hd:, :] = jnp.ones((ones_rows, seq), _BF16)
        key = lax.broadcasted_iota(jnp.int32, (tk, LANES), 0)
        lane_k = lax.broadcasted_iota(jnp.int32, (tk, LANES), 1)
        kc_ref[...] = jnp.where(lane_k < 3, key, 0).astype(_F32).astype(_BF16)

    lane = lax.broadcasted_iota(jnp.int32, (tq, hd), 1)
    qzs, sigs = [], []
    for hh in range(hb):
        q = q_ref[:, hh * hd:(hh + 1) * hd]
        zero = jnp.zeros_like(q)
        qz = jnp.concatenate([jnp.where(lane < half, q, zero), jnp.where(lane >= half, q, zero)], axis=0)
        ext = jnp.broadcast_to(sigp_ref[hh], (2 * tq, hd)).astype(_BF16)
        qzs.append(jnp.concatenate([qz, ext], axis=1))
        sigs.append(sig_ref[hh][:, :1])

    key_i = lax.broadcasted_iota(jnp.int32, (tk, 2 * tq), 0)
    qry_i = lax.broadcasted_iota(jnp.int32, (tk, 2 * tq), 1)
    causal = key_i <= jnp.where(qry_i >= tq, qry_i - tq, qry_i)

    m_ref[...] = jnp.full(m_ref.shape, NEG, _F32)
    acc_ref[...] = jnp.zeros(acc_ref.shape, _F32)

    def step(kj, masked):
        start = pl.multiple_of(kj * tk, tk)
        off = ((kj - qi) * tk).astype(_F32)
        kc = kc_ref[...]
        ss = []
        for hh in range(hb):
            k = jnp.concatenate([k_ref[pl.ds(start, tk), hh * hd:(hh + 1) * hd], kc], axis=1)
            ss.append(lax.dot_general(k, qzs[hh], _NT, preferred_element_type=_F32))
        ps = []
        for hh in range(hb):
            s = ss[hh]
            if masked:
                s = jnp.where(causal, s, NEG)
            c = sigs[hh] * off
            m_prev = m_ref[hh]
            m_new = jnp.maximum(m_prev, jnp.max(s, axis=0, keepdims=True) + c)
            alpha = jnp.exp2(m_prev - m_new)
            p = jnp.exp2(s - (m_new - c))
            m_ref[hh] = m_new
            ps.append((p.astype(_BF16), alpha))
        for hh in range(hb):
            p, alpha = ps[hh]
            vt = vt_ref[hh, :, pl.ds(start, tk)]
            acc_ref[hh] = acc_ref[hh] * alpha + jnp.dot(vt, p, preferred_element_type=_F32)

    def body(kj, carry):
        step(kj, False)
        return carry

    lax.fori_loop(0, qi, body, 0)
    step(qi, True)

    for hh in range(hb):
        ot = acc_ref[hh, :hd, :] / acc_ref[hh, hd:hd + 1, :]
        odt = ot[:, :tq] - lam * ot[:, tq:]
        ms = jnp.mean(odt * odt, axis=0, keepdims=True)
        yt = odt * lax.rsqrt(ms + NORM_EPS) * g_ref[...] * (1.0 - lam_init)
        o_ref[:, hh * hd:(hh + 1) * hd] = yt.T.astype(o_ref.dtype)


def _bf16_pieces(x):
    hi = x.astype(_BF16).astype(np.float32)
    mid = (x - hi).astype(_BF16).astype(np.float32)
    lo = (x - hi - mid).astype(_BF16).astype(np.float32)
    return hi, mid, lo


def _diff_attention(proj, lam_params, subln_g, slopes, *, batch, seq, n_heads, lam_init, tq, hb):
    nq = seq // tq
    hd = HEAD_DIM
    ng = n_heads // hb
    sig = (slopes.astype(np.float64) * LOG2E).astype(np.float32)
    sig_arr = np.broadcast_to(sig.reshape(n_heads, 1, 1), (n_heads, 1, hd))
    sigp_arr = np.zeros((n_heads, 1, hd), np.float32)
    for lane_idx, piece in enumerate(_bf16_pieces(sig)):
        sigp_arr[:, 0, lane_idx] = piece
    ones_rows = 16
    kern = functools.partial(_diff_attn_kernel, tq=tq, tk=tq, hb=hb, lam_init=lam_init)
    return pl.pallas_call(
        kern,
        grid=(batch, ng, nq),
        in_specs=[
            pl.BlockSpec(lam_params.shape, lambda b, h, i: (0, 0)),
            pl.BlockSpec((hb, 1, hd), lambda b, h, i: (h, 0, 0)),
            pl.BlockSpec((hb, 1, hd), lambda b, h, i: (h, 0, 0)),
            pl.BlockSpec((hd, 1), lambda b, h, i: (0, 0)),
            pl.BlockSpec((tq, hb * hd), lambda b, h, i: (b * nq + i, h)),
            pl.BlockSpec((seq, hb * hd), lambda b, h, i: (b, ng + h)),
            pl.BlockSpec((seq, hb * hd), lambda b, h, i: (b, 2 * ng + h)),
        ],
        out_specs=pl.BlockSpec((tq, hb * hd), lambda b, h, i: (b * nq + i, h)),
        out_shape=jax.ShapeDtypeStruct((batch * seq, n_heads * hd), _BF16),
        scratch_shapes=[
            pltpu.VMEM((hb, hd + ones_rows, seq), _BF16),
            pltpu.VMEM((tq, LANES), _BF16),
            pltpu.VMEM((hb, 1, 2 * tq), _F32),
            pltpu.VMEM((hb, hd + ones_rows, 2 * tq), _F32),
        ],
        compiler_params=_cparams(("parallel", "parallel", "arbitrary")),
        name="diff_attention",
    )(lam_params, jnp.asarray(sig_arr), jnp.asarray(sigp_arr), subln_g.reshape(hd, 1), proj, proj, proj)


def _rows(ref, start, size, sl):
    if len(ref.shape) == 2:
        return ref[start:start + size, sl]
    n = ref.shape[1]
    return ref[start // n:(start + size) // n, :, sl].reshape(size, sl.stop - sl.start)


def _store_rows(ref, start, size, sl, val):
    if len(ref.shape) == 2:
        ref[start:start + size, sl] = val
    else:
        n = ref.shape[1]
        ref[start // n:(start + size) // n, :, sl] = val.reshape(size // n, n, sl.stop - sl.start)


def _dilated_kernel(*refs, tq, span, slopes, scale, has_prev):
    if has_prev:
        q_ref, kp_ref, kc_ref, vp_ref, vc_ref, o_ref, lse_ref = refs
    else:
        q_ref, kc_ref, vc_ref, o_ref, lse_ref = refs
    length = int(np.prod(q_ref.shape[:-1]))
    first = pl.program_id(2) == 0

    def geometry(nk, shift):
        row = lax.broadcasted_iota(jnp.int32, (tq, nk), 0)
        col = lax.broadcasted_iota(jnp.int32, (tq, nk), 1)
        dist = row - col + shift
        return col, dist, (dist >= 0) & (dist <= span), dist.astype(_F32)

    geo_head = geometry(tq + span if has_prev else tq, span if has_prev else 0)
    geo_body = geometry(tq + span, span)

    for qb in range(length // tq):
        u0 = qb * tq
        col, dist, valid, distf = geo_head if qb == 0 else geo_body
        if qb == 0 and has_prev:
            valid = valid & ((col >= span) | jnp.logical_not(first))
        for hh, slope_step in enumerate(slopes):
            sl = slice(hh * HEAD_DIM, (hh + 1) * HEAD_DIM)
            q = _rows(q_ref, u0, tq, sl)
            if qb == 0 and has_prev:
                k = jnp.concatenate([kp_ref[:, sl], _rows(kc_ref, 0, tq, sl)], axis=0)
                v = jnp.concatenate([vp_ref[:, sl], _rows(vc_ref, 0, tq, sl)], axis=0)
            elif qb == 0:
                k, v = _rows(kc_ref, 0, tq, sl), _rows(vc_ref, 0, tq, sl)
            else:
                k, v = _rows(kc_ref, u0 - span, tq + span, sl), _rows(vc_ref, u0 - span, tq + span, sl)
            s = lax.dot_general(q, k, _NT, preferred_element_type=_F32) * scale - slope_step * distf
            s = jnp.where(valid, s, NEG)
            m = jnp.max(s, axis=1, keepdims=True)
            p = jnp.exp(s - m)
            l = jnp.sum(p, axis=1, keepdims=True)
            o = jnp.dot(p.astype(_BF16), v, preferred_element_type=_F32) / l
            _store_rows(o_ref, u0, tq, sl, o.astype(o_ref.dtype))
            _store_rows(lse_ref, u0, tq, sl, jnp.broadcast_to(m + jnp.log(l), (tq, HEAD_DIM)))


def _dilated_group(proj, slopes_g, *, batch, seq, group, window, dilation, in_width, mix_width):
    span = window // dilation
    assert span == LANES, "key window per stream must be one 128-row block"
    gw = HEADS_PER_GROUP * HEAD_DIM
    koff = mix_width // gw
    tq = 256
    kern = functools.partial(
        _dilated_kernel, tq=tq, span=span,
        slopes=tuple(float(s) * dilation for s in slopes_g), scale=HEAD_DIM ** -0.5, has_prev=dilation == 1)
    rows = batch * seq
    if dilation == 1:
        chunk = 1024
        nc = seq // chunk
        cur = lambda c: pl.BlockSpec((chunk, gw), lambda b, r, u: (b * nc + u, c))
        prev = lambda c: pl.BlockSpec(
            (span, gw), lambda b, r, u: (b * (seq // span) + jnp.maximum(u * (chunk // span) - 1, 0), c))
        in_specs = [cur(group), prev(koff + group), cur(koff + group), prev(2 * koff + group), cur(2 * koff + group)]
        operands = [proj] * 5
        out_spec = pl.BlockSpec((chunk, gw), lambda b, r, u: (b * nc + u, 0))
        out_dims = (rows, gw)
        grid = (batch, 1, nc)
    else:
        tiles = seq // PERM_ROWS
        n = PERM_ROWS // dilation
        proj3 = proj.reshape(rows // PERM_ROWS, PERM_ROWS, in_width)
        blk = lambda c: pl.BlockSpec((tiles, n, gw), lambda b, r, u: (b, r, c))
        in_specs = [blk(group), blk(koff + group), blk(2 * koff + group)]
        operands = [proj3] * 3
        out_spec = pl.BlockSpec((tiles, n, gw), lambda b, r, u: (b, r, 0))
        out_dims = (rows // PERM_ROWS, PERM_ROWS, gw)
        grid = (batch, dilation, 1)
    o, lse = pl.pallas_call(
        kern,
        grid=grid,
        in_specs=in_specs,
        out_specs=[out_spec, out_spec],
        out_shape=[jax.ShapeDtypeStruct(out_dims, _BF16), jax.ShapeDtypeStruct(out_dims, _F32)],
        compiler_params=_cparams(("parallel", "parallel", "arbitrary")),
        name=f"dilated_attention_d{dilation}",
    )(*operands)
    return o.reshape(rows, gw), lse.reshape(rows, gw)


def _combine_kernel(o0_ref, o1_ref, o2_ref, l0_ref, l1_ref, l2_ref, out_ref, on_ref, ln_ref, *, dilations):
    tm = out_ref.shape[0]
    gw = o0_ref.shape[1]
    nh = gw // LANES
    for g, (o_ref, l_ref, d) in enumerate(zip((o0_ref, o1_ref, o2_ref), (l0_ref, l1_ref, l2_ref), dilations)):
        n = PERM_ROWS // d
        for hh in range(nh):
            sl = slice(hh * LANES, (hh + 1) * LANES)
            if d == 1:
                on_ref[g, hh] = o_ref[:, sl].astype(_F32)
                ln_ref[g, hh] = l_ref[:, sl]
                continue
            for grp in range(tm // PERM_ROWS):
                for r in range(d):
                    src = slice(grp * PERM_ROWS + r * n, grp * PERM_ROWS + (r + 1) * n)
                    dst = pl.ds(grp * PERM_ROWS + r, n, stride=d)
                    on_ref[g, hh, dst, :] = o_ref[src, sl].astype(_F32)
                    ln_ref[g, hh, dst, :] = l_ref[src, sl]
    for hh in range(nh):
        l0, l1, l2 = ln_ref[0, hh], ln_ref[1, hh], ln_ref[2, hh]
        mx = jnp.maximum(jnp.maximum(l0, l1), l2)
        es = (jnp.exp(l0 - mx), jnp.exp(l1 - mx), jnp.exp(l2 - mx))
        den = es[0] + es[1] + es[2]
        for g in range(3):
            col = g * gw + hh * LANES
            out_ref[:, col:col + LANES] = (on_ref[g, hh] * (es[g] / den)).astype(out_ref.dtype)


def _combine_groups(outs, lses, dilations, *, tm):
    m, gw = outs[0].shape
    spec = pl.BlockSpec((tm, gw), lambda i: (i, 0))
    return pl.pallas_call(
        functools.partial(_combine_kernel, dilations=dilations),
        grid=(m // tm,),
        in_specs=[spec] * 6,
        out_specs=pl.BlockSpec((tm, 3 * gw), lambda i: (i, 0)),
        out_shape=jax.ShapeDtypeStruct((m, 3 * gw), _BF16),
        scratch_shapes=[pltpu.VMEM((3, gw // LANES, tm, LANES), _F32)] * 2,
        compiler_params=_cparams(("parallel",)),
        name="combine_groups",
    )(*outs, *lses)


def _mem_attn_kernel(q_ref, k_ref, v_ref, o_ref, *, scale):
    for hh in range(N_MEM_HEADS):
        sl = slice(hh * HEAD_DIM, (hh + 1) * HEAD_DIM)
        s = lax.dot_general(q_ref[:, sl], k_ref[:, sl], _NT, preferred_element_type=_F32) * scale
        m = jnp.max(s, axis=1, keepdims=True)
        p = jnp.exp(s - m)
        l = jnp.sum(p, axis=1, keepdims=True)
        o = jnp.dot(p.astype(_BF16), v_ref[:, sl], preferred_element_type=_F32)
        o_ref[:, sl] = (o / l).astype(o_ref.dtype)


def _memory_attention(proj, kvm, *, batch, seq, mem_tokens, q_col_block, tq):
    nq = seq // tq
    mw = N_MEM_HEADS * HEAD_DIM
    return pl.pallas_call(
        functools.partial(_mem_attn_kernel, scale=HEAD_DIM ** -0.5),
        grid=(batch, nq),
        in_specs=[
            pl.BlockSpec((tq, mw), lambda b, i: (b * nq + i, q_col_block)),
            pl.BlockSpec((mem_tokens, mw), lambda b, i: (b, 0)),
            pl.BlockSpec((mem_tokens, mw), lambda b, i: (b, 1)),
        ],
        out_specs=pl.BlockSpec((tq, mw), lambda b, i: (b * nq + i, 0)),
        out_shape=jax.ShapeDtypeStruct((batch * seq, mw), _BF16),
        compiler_params=_cparams(("parallel", "parallel")),
        name="memory_attention",
    )(proj, kvm, kvm)


def _out_proj_kernel(x_ref, a_ref, b_ref, wa_ref, wb_ref, o_ref):
    acc = jnp.dot(a_ref[...], wa_ref[...], preferred_element_type=_F32)
    acc = acc + jnp.dot(b_ref[...], wb_ref[...], preferred_element_type=_F32)
    o_ref[...] = x_ref[...] + acc


def _out_proj(x, o_mix, o_mem, w, *, tm):
    m, d = x.shape
    ka, kb = o_mix.shape[1], o_mem.shape[1]
    assert ka % kb == 0
    return pl.pallas_call(
        _out_proj_kernel,
        grid=(m // tm,),
        in_specs=[
            pl.BlockSpec((tm, d), lambda i: (i, 0)),
            pl.BlockSpec((tm, ka), lambda i: (i, 0)),
            pl.BlockSpec((tm, kb), lambda i: (i, 0)),
            pl.BlockSpec((ka, d), lambda i: (0, 0)),
            pl.BlockSpec((kb, d), lambda i: (ka // kb, 0)),
        ],
        out_specs=pl.BlockSpec((tm, d), lambda i: (i, 0)),
        out_shape=jax.ShapeDtypeStruct((m, d), _F32),
        compiler_params=_cparams(("parallel",)),
        name="out_proj",
    )(x, o_mix, o_mem, w, w)


def _mlp_kernel(x_ref, g_ref, w1_ref, w2_ref, gf_ref, o_ref, hn_ref, *, final_norm):
    f = pl.program_id(1)

    @pl.when(f == 0)
    def _():
        x = x_ref[...]
        hn_ref[...] = _rms(x, g_ref[...]).astype(hn_ref.dtype)
        o_ref[...] = x

    a = jnp.dot(hn_ref[...], w1_ref[...], preferred_element_type=_F32)
    a = jnp.square(jnp.maximum(a, 0.0)).astype(_BF16)
    o_ref[...] += jnp.dot(a, w2_ref[...], preferred_element_type=_F32)

    if final_norm:
        @pl.when(f == pl.num_programs(1) - 1)
        def _():
            o_ref[...] = _rms(o_ref[...], gf_ref[...])


def _mlp(x, g, w1, w2, g_final, *, final_norm, tm, tf):
    m, d = x.shape
    ff = w1.shape[1]
    return pl.pallas_call(
        functools.partial(_mlp_kernel, final_norm=final_norm),
        grid=(m // tm, ff // tf),
        in_specs=[
            pl.BlockSpec((tm, d), lambda i, f: (i, 0)),
            pl.BlockSpec((1, d), lambda i, f: (0, 0)),
            pl.BlockSpec((d, tf), lambda i, f: (0, f)),
            pl.BlockSpec((tf, d), lambda i, f: (f, 0)),
            pl.BlockSpec((1, d), lambda i, f: (0, 0)),
        ],
        out_specs=pl.BlockSpec((tm, d), lambda i, f: (i, 0)),
        out_shape=jax.ShapeDtypeStruct((m, d), _F32),
        scratch_shapes=[pltpu.VMEM((tm, d), _BF16)],
        compiler_params=_cparams(("parallel", "arbitrary")),
        name="mlp",
    )(x, g.reshape(1, d), w1, w2, g_final.reshape(1, d))


def kernel(x, mem, g_attn, w_in, w_out, lambda_qk, diff_subln_g, g_mem, w_mem_kv, g_mlp, w_mlp1, w_mlp2, g_final):
    batch, seq, d_model = x.shape
    depth = w_in.shape[0]
    mem_tokens = mem.shape[1]
    in_width = w_in.shape[2]
    mem_width = N_MEM_HEADS * HEAD_DIM
    mix_width = (in_width - mem_width) // 3
    n_heads = mix_width // HEAD_DIM
    slopes = _alibi_slopes(n_heads)
    dilations = tuple(d for _, d in DILATED_GROUPS)
    gw = HEADS_PER_GROUP * HEAD_DIM
    n_qkv_tiles = 3 * mix_width // gw

    xf = x.reshape(batch * seq, d_model)
    memf = mem.reshape(batch * mem_tokens, d_model)
    ones_in = np.ones((in_width,), np.float32)
    ones_kv = jnp.ones((w_mem_kv.shape[2],), _F32)
    for i in range(depth):
        kvm = _norm_matmul(memf, g_mem, w_mem_kv[i].astype(_BF16), ones_kv, tm=512, tn=512)
        w_i = w_in[i].astype(_BF16)
        if i % N_MIXERS == 0:
            j = i // N_MIXERS
            lam_init = 0.8 - 0.6 * math.exp(-0.3 * i)
            q_scale = ones_in.copy()
            q_scale[:mix_width] = LOG2E * (HEAD_DIM // 2) ** -0.5
            proj = _norm_matmul(xf, g_attn[i], w_i, jnp.asarray(q_scale), tm=512, tn=1024)
            o_mix = _diff_attention(proj, lambda_qk[j], diff_subln_g[j], slopes, batch=batch, seq=seq,
                                    n_heads=n_heads, lam_init=lam_init, tq=256, hb=6)
        else:
            tile_slot = lambda t: jnp.where(t < n_qkv_tiles, t % len(dilations), 0)
            proj = _norm_matmul(xf, g_attn[i], w_i, jnp.asarray(ones_in), tm=1024, tn=gw,
                                dilations=dilations, tile_slot=tile_slot)
            outs, lses = [], []
            for g, (window, dilation) in enumerate(DILATED_GROUPS):
                sl = slice(g * HEADS_PER_GROUP, (g + 1) * HEADS_PER_GROUP)
                o, lse = _dilated_group(proj, slopes[sl], batch=batch, seq=seq, group=g, window=window,
                                        dilation=dilation, in_width=in_width, mix_width=mix_width)
                outs.append(o)
                lses.append(lse)
            o_mix = _combine_groups(outs, lses, dilations, tm=PERM_ROWS)
        o_mem = _memory_attention(proj, kvm, batch=batch, seq=seq, mem_tokens=mem_tokens,
                                  q_col_block=3 * mix_width // mem_width, tq=512)
        xf = _out_proj(xf, o_mix, o_mem, w_out[i].astype(_BF16), tm=512)
        xf = _mlp(xf, g_mlp[i], w_mlp1[i].astype(_BF16), w_mlp2[i].astype(_BF16), g_final,
                  final_norm=(i == depth - 1), tm=1024, tf=512)
    return xf.reshape(batch, seq, d_model)
```

```python
import functools
import math

import numpy as np
import jax
import jax.numpy as jnp
from jax import lax
from jax.experimental import pallas as pl
from jax.experimental.pallas import tpu as pltpu

HEAD_DIM = 128
N_MEM_HEADS = 4
N_MIXERS = 2
DILATED_GROUPS = ((128, 1), (512, 4), (2048, 16))
HEADS_PER_GROUP = 4
NORM_EPS = 1e-6
NEG = -1e30
LANES = 128
VMEM_LIMIT = 56 * 1024 * 1024
PERM_ROWS = 512
LOG2E = math.log2(math.e)

_F32 = jnp.float32
_BF16 = jnp.bfloat16
_NT = (((1,), (1,)), ((), ()))


def _alibi_slopes(n):
    def pow2(m):
        start = 2.0 ** (-(2.0 ** -(math.log2(m) - 3)))
        return [start * start ** i for i in range(m)]

    def slopes(m):
        if math.log2(m).is_integer():
            return pow2(m)
        c = 2 ** math.floor(math.log2(m))
        return pow2(c) + slopes(2 * c)[0::2][: m - c]

    return np.asarray(sorted(slopes(n), reverse=True), dtype=np.float32)


def _cparams(semantics):
    return pltpu.CompilerParams(dimension_semantics=semantics, vmem_limit_bytes=VMEM_LIMIT)


def _rms(x, g):
    ms = jnp.mean(x * x, axis=-1, keepdims=True)
    return x * lax.rsqrt(ms + NORM_EPS) * g


def _norm_matmul_kernel(x_ref, g_ref, w_ref, o_ref, xn_ref, hn_ref, *, dilations, tile_slot):
    j = pl.program_id(1)
    tm = x_ref.shape[0]

    @pl.when(j == 0)
    def _():
        xn = _rms(x_ref[...], g_ref[...])
        hn_ref[0] = xn.astype(hn_ref.dtype)
        if len(dilations) == 1:
            return
        for c in range(xn_ref.shape[0]):
            xn_ref[c] = xn[:, c * LANES:(c + 1) * LANES]

        def permute_chunk(c, carry):
            lanes = pl.ds(pl.multiple_of(c * LANES, LANES), LANES)
            for s, d in enumerate(dilations):
                if d == 1:
                    continue
                n = PERM_ROWS // d
                for grp in range(tm // PERM_ROWS):
                    for r in range(d):
                        rows = xn_ref[c, pl.ds(grp * PERM_ROWS + r, n, stride=d), :]
                        dst = grp * PERM_ROWS + r * n
                        hn_ref[s, dst:dst + n, lanes] = rows.astype(hn_ref.dtype)
            return carry

        lax.fori_loop(0, xn_ref.shape[0], permute_chunk, 0)

    slot = tile_slot(j) if len(dilations) > 1 else 0
    o_ref[...] = jnp.dot(hn_ref[slot], w_ref[...], preferred_element_type=_F32).astype(o_ref.dtype)


def _norm_matmul(x, g, w, layer, *, tm, tn, n_tiles=None, col_tile=lambda j: j, dilations=(1,), tile_slot=None):
    m, k = x.shape
    n_tiles = w.shape[2] // tn if n_tiles is None else n_tiles
    xn_shape = (k // LANES, tm, LANES) if len(dilations) > 1 else (1, 8, LANES)
    return pl.pallas_call(
        functools.partial(_norm_matmul_kernel, dilations=dilations, tile_slot=tile_slot),
        grid=(m // tm, n_tiles),
        in_specs=[
            pl.BlockSpec((tm, k), lambda i, j: (i, 0)),
            pl.BlockSpec((1, k), lambda i, j: (0, 0)),
            pl.BlockSpec((None, k, tn), lambda i, j: (layer, 0, col_tile(j))),
        ],
        out_specs=pl.BlockSpec((tm, tn), lambda i, j: (i, j)),
        out_shape=jax.ShapeDtypeStruct((m, n_tiles * tn), _BF16),
        scratch_shapes=[pltpu.VMEM(xn_shape, _F32), pltpu.VMEM((len(dilations), tm, k), _BF16)],
        compiler_params=_cparams(("parallel", "arbitrary")),
        name="norm_matmul",
    )(x, g.reshape(1, k), w)


def _norm_matmul_t_kernel(x_ref, g_ref, wt_ref, o_ref, hn_ref, *, n_scaled, scale):
    j = pl.program_id(1)

    @pl.when(j == 0)
    def _():
        hn_ref[...] = _rms(x_ref[...], g_ref[...]).astype(hn_ref.dtype)

    acc = lax.dot_general(wt_ref[...], hn_ref[...], _NT, preferred_element_type=_F32)
    o_ref[...] = (acc * jnp.where(j < n_scaled, scale, 1.0)).astype(o_ref.dtype)


def _norm_matmul_t(x, g, wt, *, tm, tn, n_tiles, row_tile, n_scaled, scale):
    m, k = x.shape
    return pl.pallas_call(
        functools.partial(_norm_matmul_t_kernel, n_scaled=n_scaled, scale=scale),
        grid=(m // tm, n_tiles),
        in_specs=[
            pl.BlockSpec((tm, k), lambda i, j: (i, 0)),
            pl.BlockSpec((1, k), lambda i, j: (0, 0)),
            pl.BlockSpec((tn, k), lambda i, j: (row_tile(j), 0)),
        ],
        out_specs=pl.BlockSpec((tn, tm), lambda i, j: (j, i)),
        out_shape=jax.ShapeDtypeStruct((n_tiles * tn, m), _BF16),
        scratch_shapes=[pltpu.VMEM((tm, k), _BF16)],
        compiler_params=_cparams(("parallel", "arbitrary")),
        name="norm_matmul_t",
    )(x, g.reshape(1, k), wt)


def _diff_attn_kernel(lam_ref, sig_ref, sigp_ref, g_ref, kc_ref, qt_ref, k_ref, vt_ref, o_ref,
                      m_ref, acc_ref, *, tq, tk, hb, ones_rows, lam_init):
    qi = pl.program_id(2)
    hd = HEAD_DIM
    half = hd // 2
    reps = 2 * tq // LANES

    lp = lam_ref[...]
    lam = (jnp.exp(jnp.sum(lp[0:1] * lp[1:2], axis=1, keepdims=True))
           - jnp.exp(jnp.sum(lp[2:3] * lp[3:4], axis=1, keepdims=True)) + lam_init)

    row = lax.broadcasted_iota(jnp.int32, (hd, tq), 0)
    qzts, sigs = [], []
    for hh in range(hb):
        qt = qt_ref[hh * hd:(hh + 1) * hd, :]
        zero = jnp.zeros_like(qt)
        top = jnp.concatenate([jnp.where(row < half, qt, zero), jnp.where(row >= half, qt, zero)], axis=1)
        ext = jnp.concatenate([sigp_ref[hh]] * reps, axis=1).astype(_BF16)
        qzts.append(jnp.concatenate([top, ext], axis=0))
        sigs.append(sig_ref[hh][:, :1])
    key_i = lax.broadcasted_iota(jnp.int32, (tq, 2 * tq), 0)
    qry_i = lax.broadcasted_iota(jnp.int32, (tq, 2 * tq), 1)
    causal = key_i <= jnp.where(qry_i >= tq, qry_i - tq, qry_i)

    m_ref[...] = jnp.full(m_ref.shape, NEG, _F32)
    acc_ref[...] = jnp.zeros(acc_ref.shape, _F32)

    def step(start, size, masked):
        off = (start - qi * tq).astype(_F32)
        kc = kc_ref[:size, :]
        ones = jnp.ones((ones_rows, size), _BF16)
        ps = []
        for hh in range(hb):
            k = jnp.concatenate([k_ref[pl.ds(start, size), hh * hd:(hh + 1) * hd], kc], axis=1)
            s = jnp.dot(k, qzts[hh], preferred_element_type=_F32)
            if masked:
                s = jnp.where(causal, s, NEG)
            c = sigs[hh] * off
            m_prev = m_ref[hh]
            m_new = jnp.maximum(m_prev, jnp.max(s, axis=0, keepdims=True) + c)
            alpha = jnp.exp2(m_prev - m_new)
            p = jnp.exp2(s - (m_new - c))
            m_ref[hh] = m_new
            ps.append((p.astype(_BF16), alpha))
        for hh in range(hb):
            p, alpha = ps[hh]
            vt = jnp.concatenate([vt_ref[hh * hd:(hh + 1) * hd, pl.ds(start, size)], ones], axis=0)
            acc_ref[hh] = acc_ref[hh] * alpha + jnp.dot(vt, p, preferred_element_type=_F32)

    def body(j, carry):
        step(pl.multiple_of(j * tk, tk), tk, False)
        return carry

    lax.fori_loop(0, (qi * tq) // tk, body, 0)
    if tk != tq:
        @pl.when(qi % (tk // tq) == 1)
        def _():
            step(pl.multiple_of((qi - 1) * tq, tq), tq, False)
    step(pl.multiple_of(qi * tq, tq), tq, True)

    for hh in range(hb):
        ot = acc_ref[hh, :hd, :] / acc_ref[hh, hd:hd + 1, :]
        odt = ot[:, :tq] - lam * ot[:, tq:]
        ms = jnp.mean(odt * odt, axis=0, keepdims=True)
        yt = odt * lax.rsqrt(ms + NORM_EPS) * g_ref[...] * (1.0 - lam_init)
        o_ref[:, hh * hd:(hh + 1) * hd] = yt.T.astype(o_ref.dtype)


def _bf16_pieces(x):
    hi = x.astype(_BF16).astype(np.float32)
    mid = (x - hi).astype(_BF16).astype(np.float32)
    lo = (x - hi - mid).astype(_BF16).astype(np.float32)
    return hi, mid, lo


def _diff_attention(qvt, knat, lam_params, subln_g, slopes, *, batch, seq, n_heads, lam_init, tq, tk, hb):
    assert tk in (tq, 2 * tq)
    nq = seq // tq
    hd = HEAD_DIM
    ng = n_heads // hb
    sig = (slopes.astype(np.float64) * LOG2E).astype(np.float32)
    sig_arr = np.broadcast_to(sig.reshape(n_heads, 1, 1), (n_heads, 1, hd))
    sigp_arr = np.zeros((n_heads, hd, LANES), np.float32)
    for idx, piece in enumerate(_bf16_pieces(sig)):
        sigp_arr[:, idx, :] = piece[:, None]
        sigp_arr[:, 3 + idx, :] = piece[:, None]
    key = np.arange(tk)
    kc_arr = np.zeros((tk, LANES), np.float32)
    kc_arr[:, 0:3] = (key & 255)[:, None]
    kc_arr[:, 3:6] = (key - (key & 255))[:, None]
    ones_rows = 16
    kern = functools.partial(_diff_attn_kernel, tq=tq, tk=tk, hb=hb, ones_rows=ones_rows, lam_init=lam_init)
    return pl.pallas_call(
        kern,
        grid=(batch, ng, nq),
        in_specs=[
            pl.BlockSpec(lam_params.shape, lambda b, h, i: (0, 0)),
            pl.BlockSpec((hb, 1, hd), lambda b, h, i: (h, 0, 0)),
            pl.BlockSpec((hb, hd, LANES), lambda b, h, i: (h, 0, 0)),
            pl.BlockSpec((hd, 1), lambda b, h, i: (0, 0)),
            pl.BlockSpec((tk, LANES), lambda b, h, i: (0, 0)),
            pl.BlockSpec((hb * hd, tq), lambda b, h, i: (h, b * nq + i)),
            pl.BlockSpec((seq, hb * hd), lambda b, h, i: (b, h)),
            pl.BlockSpec((hb * hd, seq), lambda b, h, i: (ng + h, b)),
        ],
        out_specs=pl.BlockSpec((tq, hb * hd), lambda b, h, i: (b * nq + i, h)),
        out_shape=jax.ShapeDtypeStruct((batch * seq, n_heads * hd), _BF16),
        scratch_shapes=[
            pltpu.VMEM((hb, 1, 2 * tq), _F32),
            pltpu.VMEM((hb, hd + ones_rows, 2 * tq), _F32),
        ],
        compiler_params=_cparams(("parallel", "parallel", "arbitrary")),
        name="diff_attention",
    )(lam_params, jnp.asarray(sig_arr), jnp.asarray(sigp_arr), subln_g.reshape(hd, 1),
      jnp.asarray(kc_arr, dtype=_BF16), qvt, knat, qvt)


def _rows(ref, start, size, sl):
    if len(ref.shape) == 2:
        return ref[start:start + size, sl]
    n = ref.shape[1]
    return ref[start // n:(start + size) // n, :, sl].reshape(size, sl.stop - sl.start)


def _store_rows(ref, start, size, sl, val):
    if len(ref.shape) == 2:
        ref[start:start + size, sl] = val
    else:
        n = ref.shape[1]
        ref[start // n:(start + size) // n, :, sl] = val.reshape(size // n, n, sl.stop - sl.start)


def _dilated_kernel(*refs, tq, span, slopes, scale, has_prev):
    if has_prev:
        q_ref, kp_ref, kc_ref, vp_ref, vc_ref, o_ref, lse_ref = refs
    else:
        q_ref, kc_ref, vc_ref, o_ref, lse_ref = refs
    length = int(np.prod(q_ref.shape[:-1]))
    first = pl.program_id(2) == 0

    def geometry(nk, shift):
        row = lax.broadcasted_iota(jnp.int32, (tq, nk), 0)
        col = lax.broadcasted_iota(jnp.int32, (tq, nk), 1)
        dist = row - col + shift
        return col, dist, (dist >= 0) & (dist <= span), dist.astype(_F32)

    geo_head = geometry(tq + span if has_prev else tq, span if has_prev else 0)
    geo_body = geometry(tq + span, span)

    for qb in range(length // tq):
        u0 = qb * tq
        col, dist, valid, distf = geo_head if qb == 0 else geo_body
        if qb == 0 and has_prev:
            valid = valid & ((col >= span) | jnp.logical_not(first))
        for hh, slope_step in enumerate(slopes):
            sl = slice(hh * HEAD_DIM, (hh + 1) * HEAD_DIM)
            q = _rows(q_ref, u0, tq, sl)
            if qb == 0 and has_prev:
                k = jnp.concatenate([kp_ref[:, sl], _rows(kc_ref, 0, tq, sl)], axis=0)
                v = jnp.concatenate([vp_ref[:, sl], _rows(vc_ref, 0, tq, sl)], axis=0)
            elif qb == 0:
                k, v = _rows(kc_ref, 0, tq, sl), _rows(vc_ref, 0, tq, sl)
            else:
                k, v = _rows(kc_ref, u0 - span, tq + span, sl), _rows(vc_ref, u0 - span, tq + span, sl)
            s = lax.dot_general(q, k, _NT, preferred_element_type=_F32) * scale - slope_step * distf
            s = jnp.where(valid, s, NEG)
            m = jnp.max(s, axis=1, keepdims=True)
            p = jnp.exp(s - m)
            l = jnp.sum(p, axis=1, keepdims=True)
            o = jnp.dot(p.astype(_BF16), v, preferred_element_type=_F32) / l
            _store_rows(o_ref, u0, tq, sl, o.astype(o_ref.dtype))
            _store_rows(lse_ref, u0, tq, sl, jnp.broadcast_to(m + jnp.log(l), (tq, HEAD_DIM)))


def _dilated_group(proj, slopes_g, *, batch, seq, group, window, dilation, in_width, mix_width):
    span = window // dilation
    assert span == LANES, "key window per stream must be one 128-row block"
    gw = HEADS_PER_GROUP * HEAD_DIM
    koff = mix_width // gw
    tq = 256
    kern = functools.partial(
        _dilated_kernel, tq=tq, span=span,
        slopes=tuple(float(s) * dilation for s in slopes_g), scale=HEAD_DIM ** -0.5, has_prev=dilation == 1)
    rows = batch * seq
    if dilation == 1:
        chunk = 1024
        nc = seq // chunk
        cur = lambda c: pl.BlockSpec((chunk, gw), lambda b, r, u: (b * nc + u, c))
        prev = lambda c: pl.BlockSpec(
            (span, gw), lambda b, r, u: (b * (seq // span) + jnp.maximum(u * (chunk // span) - 1, 0), c))
        in_specs = [cur(group), prev(koff + group), cur(koff + group), prev(2 * koff + group), cur(2 * koff + group)]
        operands = [proj] * 5
        out_spec = pl.BlockSpec((chunk, gw), lambda b, r, u: (b * nc + u, 0))
        out_dims = (rows, gw)
        grid = (batch, 1, nc)
    else:
        tiles = seq // PERM_ROWS
        n = PERM_ROWS // dilation
        proj3 = proj.reshape(rows // PERM_ROWS, PERM_ROWS, in_width)
        blk = lambda c: pl.BlockSpec((tiles, n, gw), lambda b, r, u: (b, r, c))
        in_specs = [blk(group), blk(koff + group), blk(2 * koff + group)]
        operands = [proj3] * 3
        out_spec = pl.BlockSpec((tiles, n, gw), lambda b, r, u: (b, r, 0))
        out_dims = (rows // PERM_ROWS, PERM_ROWS, gw)
        grid = (batch, dilation, 1)
    o, lse = pl.pallas_call(
        kern,
        grid=grid,
        in_specs=in_specs,
        out_specs=[out_spec, out_spec],
        out_shape=[jax.ShapeDtypeStruct(out_dims, _BF16), jax.ShapeDtypeStruct(out_dims, _F32)],
        compiler_params=_cparams(("parallel", "parallel", "arbitrary")),
        name=f"dilated_attention_d{dilation}",
    )(*operands)
    return o.reshape(rows, gw), lse.reshape(rows, gw)


def _combine_kernel(o0_ref, o1_ref, o2_ref, l0_ref, l1_ref, l2_ref, out_ref, on_ref, ln_ref, *, dilations):
    tm = out_ref.shape[0]
    gw = o0_ref.shape[1]
    nh = gw // LANES
    for g, (o_ref, l_ref, d) in enumerate(zip((o0_ref, o1_ref, o2_ref), (l0_ref, l1_ref, l2_ref), dilations)):
        n = PERM_ROWS // d
        for hh in range(nh):
            sl = slice(hh * LANES, (hh + 1) * LANES)
            if d == 1:
                on_ref[g, hh] = o_ref[:, sl].astype(_F32)
                ln_ref[g, hh] = l_ref[:, sl]
                continue
            for grp in range(tm // PERM_ROWS):
                for r in range(d):
                    src = slice(grp * PERM_ROWS + r * n, grp * PERM_ROWS + (r + 1) * n)
                    dst = pl.ds(grp * PERM_ROWS + r, n, stride=d)
                    on_ref[g, hh, dst, :] = o_ref[src, sl].astype(_F32)
                    ln_ref[g, hh, dst, :] = l_ref[src, sl]
    for hh in range(nh):
        l0, l1, l2 = ln_ref[0, hh], ln_ref[1, hh], ln_ref[2, hh]
        mx = jnp.maximum(jnp.maximum(l0, l1), l2)
        es = (jnp.exp(l0 - mx), jnp.exp(l1 - mx), jnp.exp(l2 - mx))
        den = es[0] + es[1] + es[2]
        for g in range(3):
            col = g * gw + hh * LANES
            out_ref[:, col:col + LANES] = (on_ref[g, hh] * (es[g] / den)).astype(out_ref.dtype)


def _combine_groups(outs, lses, dilations, *, tm):
    m, gw = outs[0].shape
    spec = pl.BlockSpec((tm, gw), lambda i: (i, 0))
    return pl.pallas_call(
        functools.partial(_combine_kernel, dilations=dilations),
        grid=(m // tm,),
        in_specs=[spec] * 6,
        out_specs=pl.BlockSpec((tm, 3 * gw), lambda i: (i, 0)),
        out_shape=jax.ShapeDtypeStruct((m, 3 * gw), _BF16),
        scratch_shapes=[pltpu.VMEM((3, gw // LANES, tm, LANES), _F32)] * 2,
        compiler_params=_cparams(("parallel",)),
        name="combine_groups",
    )(*outs, *lses)


def _mem_attn_kernel(q_ref, k_ref, v_ref, o_ref, *, scale):
    for hh in range(N_MEM_HEADS):
        sl = slice(hh * HEAD_DIM, (hh + 1) * HEAD_DIM)
        s = lax.dot_general(q_ref[:, sl], k_ref[:, sl], _NT, preferred_element_type=_F32) * scale
        m = jnp.max(s, axis=1, keepdims=True)
        p = jnp.exp(s - m)
        l = jnp.sum(p, axis=1, keepdims=True)
        o = jnp.dot(p.astype(_BF16), v_ref[:, sl], preferred_element_type=_F32)
        o_ref[:, sl] = (o / l).astype(o_ref.dtype)


def _memory_attention(proj, kvm, *, batch, seq, mem_tokens, q_col_block, tq):
    nq = seq // tq
    mw = N_MEM_HEADS * HEAD_DIM
    return pl.pallas_call(
        functools.partial(_mem_attn_kernel, scale=HEAD_DIM ** -0.5),
        grid=(batch, nq),
        in_specs=[
            pl.BlockSpec((tq, mw), lambda b, i: (b * nq + i, q_col_block)),
            pl.BlockSpec((mem_tokens, mw), lambda b, i: (b, 0)),
            pl.BlockSpec((mem_tokens, mw), lambda b, i: (b, 1)),
        ],
        out_specs=pl.BlockSpec((tq, mw), lambda b, i: (b * nq + i, 0)),
        out_shape=jax.ShapeDtypeStruct((batch * seq, mw), _BF16),
        compiler_params=_cparams(("parallel", "parallel")),
        name="memory_attention",
    )(proj, kvm, kvm)


def _out_proj_kernel(x_ref, a_ref, b_ref, wa_ref, wb_ref, o_ref):
    acc = jnp.dot(a_ref[...], wa_ref[...], preferred_element_type=_F32)
    acc = acc + jnp.dot(b_ref[...], wb_ref[...], preferred_element_type=_F32)
    o_ref[...] = x_ref[...] + acc


def _out_proj(x, o_mix, o_mem, w, layer, *, tm):
    m, d = x.shape
    ka, kb = o_mix.shape[1], o_mem.shape[1]
    assert ka % kb == 0
    return pl.pallas_call(
        _out_proj_kernel,
        grid=(m // tm,),
        in_specs=[
            pl.BlockSpec((tm, d), lambda i: (i, 0)),
            pl.BlockSpec((tm, ka), lambda i: (i, 0)),
            pl.BlockSpec((tm, kb), lambda i: (i, 0)),
            pl.BlockSpec((None, ka, d), lambda i: (layer, 0, 0)),
            pl.BlockSpec((None, kb, d), lambda i: (layer, ka // kb, 0)),
        ],
        out_specs=pl.BlockSpec((tm, d), lambda i: (i, 0)),
        out_shape=jax.ShapeDtypeStruct((m, d), _F32),
        compiler_params=_cparams(("parallel",)),
        name="out_proj",
    )(x, o_mix, o_mem, w, w)


def _mlp_kernel(x_ref, g_ref, w1_ref, w2_ref, gf_ref, o_ref, hn_ref, *, final_norm):
    f = pl.program_id(1)

    @pl.when(f == 0)
    def _():
        x = x_ref[...]
        hn_ref[...] = _rms(x, g_ref[...]).astype(hn_ref.dtype)
        o_ref[...] = x

    a = jnp.dot(hn_ref[...], w1_ref[...], preferred_element_type=_F32)
    a = jnp.square(jnp.maximum(a, 0.0)).astype(_BF16)
    o_ref[...] += jnp.dot(a, w2_ref[...], preferred_element_type=_F32)

    if final_norm:
        @pl.when(f == pl.num_programs(1) - 1)
        def _():
            o_ref[...] = _rms(o_ref[...], gf_ref[...])


def _mlp(x, g, w1, w2, layer, g_final, *, final_norm, tm, tf):
    m, d = x.shape
    ff = w1.shape[2]
    return pl.pallas_call(
        functools.partial(_mlp_kernel, final_norm=final_norm),
        grid=(m // tm, ff // tf),
        in_specs=[
            pl.BlockSpec((tm, d), lambda i, f: (i, 0)),
            pl.BlockSpec((1, d), lambda i, f: (0, 0)),
            pl.BlockSpec((None, d, tf), lambda i, f: (layer, 0, f)),
            pl.BlockSpec((None, tf, d), lambda i, f: (layer, f, 0)),
            pl.BlockSpec((1, d), lambda i, f: (0, 0)),
        ],
        out_specs=pl.BlockSpec((tm, d), lambda i, f: (i, 0)),
        out_shape=jax.ShapeDtypeStruct((m, d), _F32),
        scratch_shapes=[pltpu.VMEM((tm, d), _BF16)],
        compiler_params=_cparams(("parallel", "arbitrary")),
        name="mlp",
    )(x, g.reshape(1, d), w1, w2, g_final.reshape(1, d))


def kernel(x, mem, g_attn, w_in, w_out, lambda_qk, diff_subln_g, g_mem, w_mem_kv, g_mlp, w_mlp1, w_mlp2, g_final):
    batch, seq, d_model = x.shape
    depth = w_in.shape[0]
    mem_tokens = mem.shape[1]
    in_width = w_in.shape[2]
    mem_width = N_MEM_HEADS * HEAD_DIM
    mix_width = (in_width - mem_width) // 3
    n_heads = mix_width // HEAD_DIM
    slopes = _alibi_slopes(n_heads)
    dilations = tuple(d for _, d in DILATED_GROUPS)
    gw = HEADS_PER_GROUP * HEAD_DIM
    n_qkv_tiles = 3 * mix_width // gw

    xf = x.reshape(batch * seq, d_model)
    memf = mem.reshape(batch * mem_tokens, d_model)
    w_in_b, w_kv_b, w_out_b = w_in.astype(_BF16), w_mem_kv.astype(_BF16), w_out.astype(_BF16)
    w1_b, w2_b = w_mlp1.astype(_BF16), w_mlp2.astype(_BF16)
    mix_tiles = mix_width // gw
    for i in range(depth):
        kvm = _norm_matmul(memf, g_mem, w_kv_b, i, tm=512, tn=512)
        if i % N_MIXERS == 0:
            j = i // N_MIXERS
            lam_init = 0.8 - 0.6 * math.exp(-0.3 * i)
            proj = _norm_matmul(xf, g_attn[i], w_in_b, i, tm=1024, tn=gw, n_tiles=mix_tiles + 1,
                                col_tile=lambda t: jnp.where(t < mix_tiles, t + mix_tiles, 3 * mix_tiles))
            qvt = _norm_matmul_t(xf, g_attn[i], jnp.transpose(w_in[i]).astype(_BF16), tm=1024, tn=gw,
                                 n_tiles=2 * mix_tiles, row_tile=lambda t: jnp.where(t < mix_tiles, t, t + mix_tiles),
                                 n_scaled=mix_tiles, scale=LOG2E * (HEAD_DIM // 2) ** -0.5)
            o_mix = _diff_attention(qvt, proj, lambda_qk[j], diff_subln_g[j], slopes, batch=batch, seq=seq,
                                    n_heads=n_heads, lam_init=lam_init, tq=256, tk=512, hb=6)
            qm_col_block = mix_tiles
        else:
            tile_slot = lambda t: jnp.where(t < n_qkv_tiles, t % len(dilations), 0)
            proj = _norm_matmul(xf, g_attn[i], w_in_b, i, tm=1024, tn=gw, dilations=dilations, tile_slot=tile_slot)
            outs, lses = [], []
            for g, (window, dilation) in enumerate(DILATED_GROUPS):
                sl = slice(g * HEADS_PER_GROUP, (g + 1) * HEADS_PER_GROUP)
                o, lse = _dilated_group(proj, slopes[sl], batch=batch, seq=seq, group=g, window=window,
                                        dilation=dilation, in_width=in_width, mix_width=mix_width)
                outs.append(o)
                lses.append(lse)
            o_mix = _combine_groups(outs, lses, dilations, tm=PERM_ROWS)
            qm_col_block = 3 * mix_tiles
        o_mem = _memory_attention(proj, kvm, batch=batch, seq=seq, mem_tokens=mem_tokens,
                                  q_col_block=qm_col_block, tq=512)
        xf = _out_proj(xf, o_mix, o_mem, w_out_b, i, tm=512)
        xf = _mlp(xf, g_mlp[i], w1_b, w2_b, i, g_final, final_norm=(i == depth - 1), tm=1024, tf=512)
    return xf.reshape(batch, seq, d_model)
```

```python
import functools
import math

import numpy as np
import jax
import jax.numpy as jnp
from jax import lax
from jax.experimental import pallas as pl
from jax.experimental.pallas import tpu as pltpu

HEAD_DIM = 128
N_MEM_HEADS = 4
N_MIXERS = 2
DILATED_GROUPS = ((128, 1), (512, 4), (2048, 16))
HEADS_PER_GROUP = 4
NORM_EPS = 1e-6
NEG = -1e30
LANES = 128
VMEM_LIMIT = 56 * 1024 * 1024
PERM_ROWS = 512
LOG2E = math.log2(math.e)

_F32 = jnp.float32
_BF16 = jnp.bfloat16
_NT = (((1,), (1,)), ((), ()))


def _alibi_slopes(n):
    def pow2(m):
        start = 2.0 ** (-(2.0 ** -(math.log2(m) - 3)))
        return [start * start ** i for i in range(m)]

    def slopes(m):
        if math.log2(m).is_integer():
            return pow2(m)
        c = 2 ** math.floor(math.log2(m))
        return pow2(c) + slopes(2 * c)[0::2][: m - c]

    return np.asarray(sorted(slopes(n), reverse=True), dtype=np.float32)


def _cparams(semantics):
    return pltpu.CompilerParams(dimension_semantics=semantics, vmem_limit_bytes=VMEM_LIMIT)


def _rms(x, g):
    ms = jnp.mean(x * x, axis=-1, keepdims=True)
    return x * lax.rsqrt(ms + NORM_EPS) * g


def _norm_matmul_kernel(x_ref, g_ref, w_ref, o_ref, xn_ref, hn_ref, *, dilations, tile_slot, slot_width):
    j = pl.program_id(1)
    tm = x_ref.shape[0]

    @pl.when(j == 0)
    def _():
        xn = _rms(x_ref[...], g_ref[...])
        hn_ref[0] = xn.astype(hn_ref.dtype)
        if len(dilations) == 1:
            return
        for c in range(xn_ref.shape[0]):
            xn_ref[c] = xn[:, c * LANES:(c + 1) * LANES]

        def permute_chunk(c, carry):
            lanes = pl.ds(pl.multiple_of(c * LANES, LANES), LANES)
            for s, d in enumerate(dilations):
                if d == 1:
                    continue
                n = PERM_ROWS // d
                for grp in range(tm // PERM_ROWS):
                    for r in range(d):
                        rows = xn_ref[c, pl.ds(grp * PERM_ROWS + r, n, stride=d), :]
                        dst = grp * PERM_ROWS + r * n
                        hn_ref[s, dst:dst + n, lanes] = rows.astype(hn_ref.dtype)
            return carry

        lax.fori_loop(0, xn_ref.shape[0], permute_chunk, 0)

    if len(dilations) == 1:
        o_ref[...] = jnp.dot(hn_ref[0], w_ref[...], preferred_element_type=_F32).astype(o_ref.dtype)
    else:
        parts = o_ref.shape[1] // slot_width
        for t in range(parts):
            cols = slice(t * slot_width, (t + 1) * slot_width)
            acc = jnp.dot(hn_ref[tile_slot(j * parts + t)], w_ref[:, cols], preferred_element_type=_F32)
            o_ref[:, cols] = acc.astype(o_ref.dtype)


def _norm_matmul(x, g, w, layer, *, tm, tn, n_tiles=None, col_tile=lambda j: j, dilations=(1,), tile_slot=None,
                 slot_width=None):
    m, k = x.shape
    n_tiles = w.shape[2] // tn if n_tiles is None else n_tiles
    xn_shape = (k // LANES, tm, LANES) if len(dilations) > 1 else (1, 8, LANES)
    return pl.pallas_call(
        functools.partial(_norm_matmul_kernel, dilations=dilations, tile_slot=tile_slot, slot_width=slot_width),
        grid=(m // tm, n_tiles),
        in_specs=[
            pl.BlockSpec((tm, k), lambda i, j: (i, 0)),
            pl.BlockSpec((1, k), lambda i, j: (0, 0)),
            pl.BlockSpec((None, k, tn), lambda i, j: (layer, 0, col_tile(j))),
        ],
        out_specs=pl.BlockSpec((tm, tn), lambda i, j: (i, j)),
        out_shape=jax.ShapeDtypeStruct((m, n_tiles * tn), _BF16),
        scratch_shapes=[pltpu.VMEM(xn_shape, _F32), pltpu.VMEM((len(dilations), tm, k), _BF16)],
        compiler_params=_cparams(("parallel", "arbitrary")),
        name="norm_matmul",
    )(x, g.reshape(1, k), w)


def _pick(j, values):
    out = values[-1]
    for idx in range(len(values) - 2, -1, -1):
        out = jnp.where(j == idx, values[idx], out)
    return out


def _norm_matmul_dual_kernel(x_ref, g_ref, wa_ref, wb_ref, ta_ref, tb_ref, nat_ref, tr_ref, hn_ref,
                             *, nat_steps, scales_a, scales_b):
    j = pl.program_id(1)
    tn = wa_ref.shape[1]

    @pl.when(j == 0)
    def _():
        hn_ref[...] = _rms(x_ref[...], g_ref[...]).astype(hn_ref.dtype)

    @pl.when(j < nat_steps)
    def _():
        nat_ref[:, :tn] = jnp.dot(hn_ref[...], wa_ref[...], preferred_element_type=_F32).astype(nat_ref.dtype)
        nat_ref[:, tn:] = jnp.dot(hn_ref[...], wb_ref[...], preferred_element_type=_F32).astype(nat_ref.dtype)

    @pl.when(j >= nat_steps)
    def _():
        acc = lax.dot_general(ta_ref[...], hn_ref[...], _NT, preferred_element_type=_F32)
        tr_ref[:tn, :] = (acc * _pick(j, scales_a)).astype(tr_ref.dtype)
        acc = lax.dot_general(tb_ref[...], hn_ref[...], _NT, preferred_element_type=_F32)
        tr_ref[tn:, :] = (acc * _pick(j, scales_b)).astype(tr_ref.dtype)


def _norm_matmul_dual(x, g, w, layer, wt, *, tm, tn, nat_cols, t_rows):
    m, k = x.shape
    assert len(nat_cols) % 2 == 0 and len(t_rows) % 2 == 0
    nat_steps, t_steps = len(nat_cols) // 2, len(t_rows) // 2
    ca = [nat_cols[2 * s] for s in range(nat_steps)] + [nat_cols[-2]] * t_steps
    cb = [nat_cols[2 * s + 1] for s in range(nat_steps)] + [nat_cols[-1]] * t_steps
    ra = [t_rows[0][0]] * nat_steps + [t_rows[2 * s][0] for s in range(t_steps)]
    rb = [t_rows[1][0]] * nat_steps + [t_rows[2 * s + 1][0] for s in range(t_steps)]
    scales_a = [1.0] * nat_steps + [float(t_rows[2 * s][1]) for s in range(t_steps)]
    scales_b = [1.0] * nat_steps + [float(t_rows[2 * s + 1][1]) for s in range(t_steps)]
    return pl.pallas_call(
        functools.partial(_norm_matmul_dual_kernel, nat_steps=nat_steps, scales_a=scales_a, scales_b=scales_b),
        grid=(m // tm, nat_steps + t_steps),
        in_specs=[
            pl.BlockSpec((tm, k), lambda i, j: (i, 0)),
            pl.BlockSpec((1, k), lambda i, j: (0, 0)),
            pl.BlockSpec((None, k, tn), lambda i, j: (layer, 0, _pick(j, ca))),
            pl.BlockSpec((None, k, tn), lambda i, j: (layer, 0, _pick(j, cb))),
            pl.BlockSpec((tn, k), lambda i, j: (_pick(j, ra), 0)),
            pl.BlockSpec((tn, k), lambda i, j: (_pick(j, rb), 0)),
        ],
        out_specs=[
            pl.BlockSpec((tm, 2 * tn), lambda i, j: (i, jnp.minimum(j, nat_steps - 1))),
            pl.BlockSpec((2 * tn, tm), lambda i, j: (jnp.maximum(j - nat_steps, 0), i)),
        ],
        out_shape=[
            jax.ShapeDtypeStruct((m, len(nat_cols) * tn), _BF16),
            jax.ShapeDtypeStruct((len(t_rows) * tn, m), _BF16),
        ],
        scratch_shapes=[pltpu.VMEM((tm, k), _BF16)],
        compiler_params=_cparams(("parallel", "arbitrary")),
        name="norm_matmul_dual",
    )(x, g.reshape(1, k), w, w, wt, wt)


def _diff_attn_kernel(lam_ref, sig_ref, sigp_ref, g_ref, kc_ref, qt_ref, k_ref, vt_ref, o_ref,
                      m_ref, acc_ref, *, tq, tk, hb, ones_rows, lam_init):
    qi = pl.program_id(2)
    hd = HEAD_DIM
    half = hd // 2
    reps = 2 * tq // LANES

    lp = lam_ref[...]
    lam = (jnp.exp(jnp.sum(lp[0:1] * lp[1:2], axis=1, keepdims=True))
           - jnp.exp(jnp.sum(lp[2:3] * lp[3:4], axis=1, keepdims=True)) + lam_init)

    row = lax.broadcasted_iota(jnp.int32, (hd, tq), 0)
    qzts, sigs = [], []
    for hh in range(hb):
        qt = qt_ref[hh * hd:(hh + 1) * hd, :]
        zero = jnp.zeros_like(qt)
        top = jnp.concatenate([jnp.where(row < half, qt, zero), jnp.where(row >= half, qt, zero)], axis=1)
        ext = jnp.concatenate([sigp_ref[hh]] * reps, axis=1).astype(_BF16)
        qzts.append(jnp.concatenate([top, ext], axis=0))
        sigs.append(sig_ref[hh][:, :1])
    key_i = lax.broadcasted_iota(jnp.int32, (tq, 2 * tq), 0)
    qry_i = lax.broadcasted_iota(jnp.int32, (tq, 2 * tq), 1)
    causal = key_i <= jnp.where(qry_i >= tq, qry_i - tq, qry_i)

    m_ref[...] = jnp.full(m_ref.shape, NEG, _F32)
    acc_ref[...] = jnp.zeros(acc_ref.shape, _F32)

    def step(start, size, masked):
        off = (start - qi * tq).astype(_F32)
        kc = kc_ref[:size, :]
        ones = jnp.ones((ones_rows, size), _BF16)
        ps = []
        for hh in range(hb):
            k = jnp.concatenate([k_ref[pl.ds(start, size), hh * hd:(hh + 1) * hd], kc], axis=1)
            c = sigs[hh] * off
            s = jnp.dot(k, qzts[hh], preferred_element_type=_F32)
            if masked:
                s = jnp.where(causal, s, NEG)
            m_prev = m_ref[hh]
            m_new = jnp.maximum(m_prev, jnp.max(s, axis=0, keepdims=True) + c)
            alpha = jnp.exp2(m_prev - m_new)
            p = jnp.exp2(s - (m_new - c))
            m_ref[hh] = m_new
            ps.append((p.astype(_BF16), alpha))
        for hh in range(hb):
            p, alpha = ps[hh]
            vt = jnp.concatenate([vt_ref[hh * hd:(hh + 1) * hd, pl.ds(start, size)], ones], axis=0)
            acc_ref[hh] = acc_ref[hh] * alpha + jnp.dot(vt, p, preferred_element_type=_F32)

    def body(j, carry):
        step(pl.multiple_of(j * tk, tk), tk, False)
        return carry

    lax.fori_loop(0, (qi * tq) // tk, body, 0)
    if tk != tq:
        @pl.when(qi % (tk // tq) == 1)
        def _():
            step(pl.multiple_of((qi - 1) * tq, tq), tq, False)
    step(pl.multiple_of(qi * tq, tq), tq, True)

    for hh in range(hb):
        ot = acc_ref[hh, :hd, :] / acc_ref[hh, hd:hd + 1, :]
        odt = ot[:, :tq] - lam * ot[:, tq:]
        ms = jnp.mean(odt * odt, axis=0, keepdims=True)
        yt = odt * lax.rsqrt(ms + NORM_EPS) * g_ref[...] * (1.0 - lam_init)
        o_ref[:, hh * hd:(hh + 1) * hd] = yt.T.astype(o_ref.dtype)


def _bf16_pieces(x):
    hi = x.astype(_BF16).astype(np.float32)
    mid = (x - hi).astype(_BF16).astype(np.float32)
    lo = (x - hi - mid).astype(_BF16).astype(np.float32)
    return hi, mid, lo


def _diff_attention(qvt, knat, lam_params, subln_g, slopes, *, batch, seq, n_heads, lam_init, tq, tk, hb):
    assert tk in (tq, 2 * tq)
    nq = seq // tq
    hd = HEAD_DIM
    ng = n_heads // hb
    sig = (slopes.astype(np.float64) * LOG2E).astype(np.float32)
    sig_arr = np.broadcast_to(sig.reshape(n_heads, 1, 1), (n_heads, 1, hd))
    sigp_arr = np.zeros((n_heads, hd, LANES), np.float32)
    for idx, piece in enumerate(_bf16_pieces(sig)):
        sigp_arr[:, idx, :] = piece[:, None]
        sigp_arr[:, 3 + idx, :] = piece[:, None]
    key = np.arange(tk)
    kc_arr = np.zeros((tk, LANES), np.float32)
    kc_arr[:, 0:3] = (key & 255)[:, None]
    kc_arr[:, 3:6] = (key - (key & 255))[:, None]
    ones_rows = 16
    kern = functools.partial(_diff_attn_kernel, tq=tq, tk=tk, hb=hb, ones_rows=ones_rows, lam_init=lam_init)
    return pl.pallas_call(
        kern,
        grid=(batch, ng, nq),
        in_specs=[
            pl.BlockSpec(lam_params.shape, lambda b, h, i: (0, 0)),
            pl.BlockSpec((hb, 1, hd), lambda b, h, i: (h, 0, 0)),
            pl.BlockSpec((hb, hd, LANES), lambda b, h, i: (h, 0, 0)),
            pl.BlockSpec((hd, 1), lambda b, h, i: (0, 0)),
            pl.BlockSpec((tk, LANES), lambda b, h, i: (0, 0)),
            pl.BlockSpec((hb * hd, tq), lambda b, h, i: (h, b * nq + i)),
            pl.BlockSpec((seq, hb * hd), lambda b, h, i: (b, h)),
            pl.BlockSpec((hb * hd, seq), lambda b, h, i: (ng + h, b)),
        ],
        out_specs=pl.BlockSpec((tq, hb * hd), lambda b, h, i: (b * nq + i, h)),
        out_shape=jax.ShapeDtypeStruct((batch * seq, n_heads * hd), _BF16),
        scratch_shapes=[
            pltpu.VMEM((hb, 1, 2 * tq), _F32),
            pltpu.VMEM((hb, hd + ones_rows, 2 * tq), _F32),
        ],
        compiler_params=_cparams(("parallel", "parallel", "arbitrary")),
        name="diff_attention",
    )(lam_params, jnp.asarray(sig_arr), jnp.asarray(sigp_arr), subln_g.reshape(hd, 1),
      jnp.asarray(kc_arr, dtype=_BF16), qvt, knat, qvt)


def _rows(ref, n, st, start, size, sl):
    if len(ref.shape) == 2:
        return ref[start:start + size, sl]
    return ref[start // n:(start + size) // n, st * n:(st + 1) * n, sl].reshape(size, sl.stop - sl.start)


def _store_rows(ref, n, st, start, size, sl, val):
    if len(ref.shape) == 2:
        ref[start:start + size, sl] = val
    else:
        ref[start // n:(start + size) // n, st * n:(st + 1) * n, sl] = val.reshape(size // n, n, sl.stop - sl.start)


def _dilated_kernel(*refs, tq, span, n, streams, slopes2, scale2, has_prev):
    if has_prev:
        q_ref, kp_ref, kc_ref, vp_ref, vc_ref, o_ref, lse_ref = refs
    else:
        q_ref, kc_ref, vc_ref, o_ref, lse_ref = refs
    length = q_ref.shape[0] if len(q_ref.shape) == 2 else q_ref.shape[0] * n
    first = pl.program_id(2) == 0
    nh = len(slopes2)
    lane_grp = lax.broadcasted_iota(jnp.int32, (tq, LANES), 1) // (LANES // nh)

    def masked_bias(nk, shift, first_block):
        row = lax.broadcasted_iota(jnp.int32, (tq, nk), 0)
        col = lax.broadcasted_iota(jnp.int32, (tq, nk), 1)
        dist = row - col + shift
        valid = (dist >= 0) & (dist <= span)
        if first_block and has_prev:
            valid = valid & ((col >= span) | jnp.logical_not(first))
        distf = dist.astype(_F32)
        return [jnp.where(valid, -s2 * distf, NEG) for s2 in slopes2]

    bias_head = masked_bias(tq + span if has_prev else tq, span if has_prev else 0, True)
    bias_body = masked_bias(tq + span, span, False) if length > tq else None

    for st in range(streams):
        for qb in range(length // tq):
            u0 = qb * tq
            lses = []
            for hh in range(nh):
                sl = slice(hh * HEAD_DIM, (hh + 1) * HEAD_DIM)
                q = _rows(q_ref, n, st, u0, tq, sl)
                if qb == 0 and has_prev:
                    k = jnp.concatenate([kp_ref[:, sl], _rows(kc_ref, n, st, 0, tq, sl)], axis=0)
                    v = jnp.concatenate([vp_ref[:, sl], _rows(vc_ref, n, st, 0, tq, sl)], axis=0)
                elif qb == 0:
                    k, v = _rows(kc_ref, n, st, 0, tq, sl), _rows(vc_ref, n, st, 0, tq, sl)
                else:
                    k = _rows(kc_ref, n, st, u0 - span, tq + span, sl)
                    v = _rows(vc_ref, n, st, u0 - span, tq + span, sl)
                bias = (bias_head if qb == 0 else bias_body)[hh]
                s = lax.dot_general(q, k, _NT, preferred_element_type=_F32) * scale2 + bias
                m = jnp.max(s, axis=1, keepdims=True)
                p = jnp.exp2(s - m)
                l = jnp.sum(p, axis=1, keepdims=True)
                o = jnp.dot(p.astype(_BF16), v, preferred_element_type=_F32) / l
                _store_rows(o_ref, n, st, u0, tq, sl, o.astype(o_ref.dtype))
                lses.append(m + jnp.log2(l))
            packed = jnp.broadcast_to(lses[nh - 1], (tq, LANES))
            for hh in range(nh - 2, -1, -1):
                packed = jnp.where(lane_grp == hh, lses[hh], packed)
            _store_rows(lse_ref, n, st, u0, tq, slice(0, LANES), packed)


def _dilated_group(proj, slopes_g, *, batch, seq, group, window, dilation, in_width, mix_width):
    span = window // dilation
    assert span == LANES, "key window per stream must be one 128-row block"
    gw = HEADS_PER_GROUP * HEAD_DIM
    koff = mix_width // gw
    tq = 256
    rows = batch * seq
    n = PERM_ROWS // dilation
    streams = max(1, LANES // n) if dilation > 1 else 1
    kern = functools.partial(
        _dilated_kernel, tq=tq, span=span, n=n, streams=streams,
        slopes2=tuple(float(s) * dilation * LOG2E for s in slopes_g), scale2=HEAD_DIM ** -0.5 * LOG2E,
        has_prev=dilation == 1)
    if dilation == 1:
        chunk = 1024
        nc = seq // chunk
        cur = lambda c: pl.BlockSpec((chunk, gw), lambda b, r, u: (b * nc + u, c))
        prev = lambda c: pl.BlockSpec(
            (span, gw), lambda b, r, u: (b * (seq // span) + jnp.maximum(u * (chunk // span) - 1, 0), c))
        in_specs = [cur(group), prev(koff + group), cur(koff + group), prev(2 * koff + group), cur(2 * koff + group)]
        operands = [proj] * 5
        out_specs = [pl.BlockSpec((chunk, w), lambda b, r, u: (b * nc + u, 0)) for w in (gw, LANES)]
        out_dims = [(rows, gw), (rows, LANES)]
        grid = (batch, 1, nc)
    else:
        tiles = seq // PERM_ROWS
        proj3 = proj.reshape(rows // PERM_ROWS, PERM_ROWS, in_width)
        blk = lambda c, w: pl.BlockSpec((tiles, streams * n, w), lambda b, r, u: (b, r, c))
        in_specs = [blk(group, gw), blk(koff + group, gw), blk(2 * koff + group, gw)]
        operands = [proj3] * 3
        out_specs = [blk(0, gw), blk(0, LANES)]
        out_dims = [(rows // PERM_ROWS, PERM_ROWS, gw), (rows // PERM_ROWS, PERM_ROWS, LANES)]
        grid = (batch, dilation // streams, 1)
    o, lse = pl.pallas_call(
        kern,
        grid=grid,
        in_specs=in_specs,
        out_specs=out_specs,
        out_shape=[jax.ShapeDtypeStruct(out_dims[0], _BF16), jax.ShapeDtypeStruct(out_dims[1], _F32)],
        compiler_params=_cparams(("parallel", "parallel", "arbitrary")),
        name=f"dilated_attention_d{dilation}",
    )(*operands)
    return o.reshape(rows, gw), lse.reshape(rows, LANES)


def _combine_kernel(o0_ref, o1_ref, o2_ref, l0_ref, l1_ref, l2_ref, out_ref, on_ref, ln_ref, *, dilations):
    tm = out_ref.shape[0]
    gw = o0_ref.shape[1]
    nh = gw // LANES
    for g, (o_ref, l_ref, d) in enumerate(zip((o0_ref, o1_ref, o2_ref), (l0_ref, l1_ref, l2_ref), dilations)):
        n = PERM_ROWS // d
        if d == 1:
            ln_ref[g] = l_ref[...]
            for hh in range(nh):
                on_ref[g, hh] = o_ref[:, hh * LANES:(hh + 1) * LANES].astype(_F32)
            continue
        for grp in range(tm // PERM_ROWS):
            for r in range(d):
                src = slice(grp * PERM_ROWS + r * n, grp * PERM_ROWS + (r + 1) * n)
                dst = pl.ds(grp * PERM_ROWS + r, n, stride=d)
                ln_ref[g, dst, :] = l_ref[src, :]
                for hh in range(nh):
                    on_ref[g, hh, dst, :] = o_ref[src, hh * LANES:(hh + 1) * LANES].astype(_F32)
    l0, l1, l2 = ln_ref[0], ln_ref[1], ln_ref[2]
    mx = jnp.maximum(jnp.maximum(l0, l1), l2)
    es = (jnp.exp2(l0 - mx), jnp.exp2(l1 - mx), jnp.exp2(l2 - mx))
    den = es[0] + es[1] + es[2]
    for g in range(3):
        wg = es[g] / den
        for hh in range(nh):
            lane0 = hh * (LANES // nh)
            col = g * gw + hh * LANES
            out_ref[:, col:col + LANES] = (on_ref[g, hh] * wg[:, lane0:lane0 + 1]).astype(out_ref.dtype)


def _combine_groups(outs, lses, dilations, *, tm):
    m, gw = outs[0].shape
    spec = pl.BlockSpec((tm, gw), lambda i: (i, 0))
    return pl.pallas_call(
        functools.partial(_combine_kernel, dilations=dilations),
        grid=(m // tm,),
        in_specs=[spec] * 3 + [pl.BlockSpec((tm, LANES), lambda i: (i, 0))] * 3,
        out_specs=pl.BlockSpec((tm, 3 * gw), lambda i: (i, 0)),
        out_shape=jax.ShapeDtypeStruct((m, 3 * gw), _BF16),
        scratch_shapes=[pltpu.VMEM((3, gw // LANES, tm, LANES), _F32), pltpu.VMEM((3, tm, LANES), _F32)],
        compiler_params=_cparams(("parallel",)),
        name="combine_groups",
    )(*outs, *lses)


def _mem_attn_kernel(q_ref, k_ref, v_ref, o_ref, *, scale):
    for hh in range(N_MEM_HEADS):
        sl = slice(hh * HEAD_DIM, (hh + 1) * HEAD_DIM)
        s = lax.dot_general(q_ref[:, sl], k_ref[:, sl], _NT, preferred_element_type=_F32) * scale
        m = jnp.max(s, axis=1, keepdims=True)
        p = jnp.exp(s - m)
        l = jnp.sum(p, axis=1, keepdims=True)
        o = jnp.dot(p.astype(_BF16), v_ref[:, sl], preferred_element_type=_F32)
        o_ref[:, sl] = (o / l).astype(o_ref.dtype)


def _memory_attention(proj, kvm, *, batch, seq, mem_tokens, q_col_block, tq):
    nq = seq // tq
    mw = N_MEM_HEADS * HEAD_DIM
    return pl.pallas_call(
        functools.partial(_mem_attn_kernel, scale=HEAD_DIM ** -0.5),
        grid=(batch, nq),
        in_specs=[
            pl.BlockSpec((tq, mw), lambda b, i: (b * nq + i, q_col_block)),
            pl.BlockSpec((mem_tokens, mw), lambda b, i: (b, 0)),
            pl.BlockSpec((mem_tokens, mw), lambda b, i: (b, 1)),
        ],
        out_specs=pl.BlockSpec((tq, mw), lambda b, i: (b * nq + i, 0)),
        out_shape=jax.ShapeDtypeStruct((batch * seq, mw), _BF16),
        compiler_params=_cparams(("parallel", "parallel")),
        name="memory_attention",
    )(proj, kvm, kvm)


def _out_proj_kernel(x_ref, a_ref, b_ref, wa_ref, wb_ref, o_ref):
    acc = jnp.dot(a_ref[...], wa_ref[...], preferred_element_type=_F32)
    acc = acc + jnp.dot(b_ref[...], wb_ref[...], preferred_element_type=_F32)
    o_ref[...] = x_ref[...] + acc


def _out_proj(x, o_mix, o_mem, w, layer, *, tm):
    m, d = x.shape
    ka, kb = o_mix.shape[1], o_mem.shape[1]
    assert ka % kb == 0
    return pl.pallas_call(
        _out_proj_kernel,
        grid=(m // tm,),
        in_specs=[
            pl.BlockSpec((tm, d), lambda i: (i, 0)),
            pl.BlockSpec((tm, ka), lambda i: (i, 0)),
            pl.BlockSpec((tm, kb), lambda i: (i, 0)),
            pl.BlockSpec((None, ka, d), lambda i: (layer, 0, 0)),
            pl.BlockSpec((None, kb, d), lambda i: (layer, ka // kb, 0)),
        ],
        out_specs=pl.BlockSpec((tm, d), lambda i: (i, 0)),
        out_shape=jax.ShapeDtypeStruct((m, d), _F32),
        compiler_params=_cparams(("parallel",)),
        name="out_proj",
    )(x, o_mix, o_mem, w, w)


def _mlp_kernel(x_ref, g_ref, w1_ref, w2_ref, gf_ref, o_ref, hn_ref, *, final_norm):
    f = pl.program_id(1)

    @pl.when(f == 0)
    def _():
        x = x_ref[...]
        hn_ref[...] = _rms(x, g_ref[...]).astype(hn_ref.dtype)
        o_ref[...] = x

    a = jnp.dot(hn_ref[...], w1_ref[...], preferred_element_type=_F32)
    a = jnp.square(jnp.maximum(a, 0.0)).astype(_BF16)
    o_ref[...] += jnp.dot(a, w2_ref[...], preferred_element_type=_F32)

    if final_norm:
        @pl.when(f == pl.num_programs(1) - 1)
        def _():
            o_ref[...] = _rms(o_ref[...], gf_ref[...])


def _mlp(x, g, w1, w2, layer, g_final, *, final_norm, tm, tf):
    m, d = x.shape
    ff = w1.shape[2]
    return pl.pallas_call(
        functools.partial(_mlp_kernel, final_norm=final_norm),
        grid=(m // tm, ff // tf),
        in_specs=[
            pl.BlockSpec((tm, d), lambda i, f: (i, 0)),
            pl.BlockSpec((1, d), lambda i, f: (0, 0)),
            pl.BlockSpec((None, d, tf), lambda i, f: (layer, 0, f)),
            pl.BlockSpec((None, tf, d), lambda i, f: (layer, f, 0)),
            pl.BlockSpec((1, d), lambda i, f: (0, 0)),
        ],
        out_specs=pl.BlockSpec((tm, d), lambda i, f: (i, 0)),
        out_shape=jax.ShapeDtypeStruct((m, d), _F32),
        scratch_shapes=[pltpu.VMEM((tm, d), _BF16)],
        compiler_params=_cparams(("parallel", "arbitrary")),
        name="mlp",
    )(x, g.reshape(1, d), w1, w2, g_final.reshape(1, d))


def kernel(x, mem, g_attn, w_in, w_out, lambda_qk, diff_subln_g, g_mem, w_mem_kv, g_mlp, w_mlp1, w_mlp2, g_final):
    batch, seq, d_model = x.shape
    depth = w_in.shape[0]
    mem_tokens = mem.shape[1]
    in_width = w_in.shape[2]
    mem_width = N_MEM_HEADS * HEAD_DIM
    mix_width = (in_width - mem_width) // 3
    n_heads = mix_width // HEAD_DIM
    slopes = _alibi_slopes(n_heads)
    dilations = tuple(d for _, d in DILATED_GROUPS)
    gw = HEADS_PER_GROUP * HEAD_DIM
    n_qkv_tiles = 3 * mix_width // gw

    xf = x.reshape(batch * seq, d_model)
    memf = mem.reshape(batch * mem_tokens, d_model)
    w_in_b, w_kv_b, w_out_b = w_in.astype(_BF16), w_mem_kv.astype(_BF16), w_out.astype(_BF16)
    w1_b, w2_b = w_mlp1.astype(_BF16), w_mlp2.astype(_BF16)
    mix_tiles = mix_width // gw
    for i in range(depth):
        kvm = _norm_matmul(memf, g_mem, w_kv_b, i, tm=512, tn=512)
        if i % N_MIXERS == 0:
            j = i // N_MIXERS
            lam_init = 0.8 - 0.6 * math.exp(-0.3 * i)
            q_scale = LOG2E * (HEAD_DIM // 2) ** -0.5
            proj, qvt = _norm_matmul_dual(
                xf, g_attn[i], w_in_b, i, jnp.transpose(w_in[i]).astype(_BF16), tm=1024, tn=gw,
                nat_cols=[mix_tiles + t for t in range(mix_tiles)] + [3 * mix_tiles],
                t_rows=[(t, q_scale) for t in range(mix_tiles)] + [(2 * mix_tiles + t, 1.0) for t in range(mix_tiles)])
            o_mix = _diff_attention(qvt, proj, lambda_qk[j], diff_subln_g[j], slopes, batch=batch, seq=seq,
                                    n_heads=n_heads, lam_init=lam_init, tq=256, tk=512, hb=6)
            qm_col_block = mix_tiles
        else:
            tile_slot = lambda t: jnp.where(t < n_qkv_tiles, t % len(dilations), 0)
            proj = _norm_matmul(xf, g_attn[i], w_in_b, i, tm=1024, tn=2 * gw, dilations=dilations,
                                tile_slot=tile_slot, slot_width=gw)
            outs, lses = [], []
            for g, (window, dilation) in enumerate(DILATED_GROUPS):
                sl = slice(g * HEADS_PER_GROUP, (g + 1) * HEADS_PER_GROUP)
                o, lse = _dilated_group(proj, slopes[sl], batch=batch, seq=seq, group=g, window=window,
                                        dilation=dilation, in_width=in_width, mix_width=mix_width)
                outs.append(o)
                lses.append(lse)
            o_mix = _combine_groups(outs, lses, dilations, tm=PERM_ROWS)
            qm_col_block = 3 * mix_tiles
        o_mem = _memory_attention(proj, kvm, batch=batch, seq=seq, mem_tokens=mem_tokens,
                                  q_col_block=qm_col_block, tq=512)
        xf = _out_proj(xf, o_mix, o_mem, w_out_b, i, tm=512)
        xf = _mlp(xf, g_mlp[i], w1_b, w2_b, i, g_final, final_norm=(i == depth - 1), tm=1024, tf=512)
    return xf.reshape(batch, seq, d_model)
```

```python
import functools
import math

import numpy as np
import jax
import jax.numpy as jnp
from jax import lax
from jax.experimental import pallas as pl
from jax.experimental.pallas import tpu as pltpu

HEAD_DIM = 128
N_MEM_HEADS = 4
N_MIXERS = 2
DILATED_GROUPS = ((128, 1), (512, 4), (2048, 16))
HEADS_PER_GROUP = 4
NORM_EPS = 1e-6
NEG = -1e30
LANES = 128
VMEM_LIMIT = 56 * 1024 * 1024
PERM_ROWS = 512
LOG2E = math.log2(math.e)

_F32 = jnp.float32
_BF16 = jnp.bfloat16
_NT = (((1,), (1,)), ((), ()))


def _alibi_slopes(n):
    def pow2(m):
        start = 2.0 ** (-(2.0 ** -(math.log2(m) - 3)))
        return [start * start ** i for i in range(m)]

    def slopes(m):
        if math.log2(m).is_integer():
            return pow2(m)
        c = 2 ** math.floor(math.log2(m))
        return pow2(c) + slopes(2 * c)[0::2][: m - c]

    return np.asarray(sorted(slopes(n), reverse=True), dtype=np.float32)


def _cparams(semantics):
    return pltpu.CompilerParams(dimension_semantics=semantics, vmem_limit_bytes=VMEM_LIMIT)


def _rms(x, g):
    ms = jnp.mean(x * x, axis=-1, keepdims=True)
    return x * lax.rsqrt(ms + NORM_EPS) * g


def _norm_matmul_kernel(x_ref, g_ref, w_ref, o_ref, xn_ref, hn_ref, *, dilations, tile_slot, slot_width):
    j = pl.program_id(1)
    tm = x_ref.shape[0]

    @pl.when(j == 0)
    def _():
        xn = _rms(x_ref[...], g_ref[...])
        hn_ref[0] = xn.astype(hn_ref.dtype)
        if len(dilations) == 1:
            return
        for c in range(xn_ref.shape[0]):
            xn_ref[c] = xn[:, c * LANES:(c + 1) * LANES]

        def permute_chunk(c, carry):
            lanes = pl.ds(pl.multiple_of(c * LANES, LANES), LANES)
            for s, d in enumerate(dilations):
                if d == 1:
                    continue
                n = PERM_ROWS // d
                for grp in range(tm // PERM_ROWS):
                    for r in range(d):
                        rows = xn_ref[c, pl.ds(grp * PERM_ROWS + r, n, stride=d), :]
                        dst = grp * PERM_ROWS + r * n
                        hn_ref[s, dst:dst + n, lanes] = rows.astype(hn_ref.dtype)
            return carry

        lax.fori_loop(0, xn_ref.shape[0], permute_chunk, 0)

    if len(dilations) == 1:
        o_ref[...] = jnp.dot(hn_ref[0], w_ref[...], preferred_element_type=_F32).astype(o_ref.dtype)
    else:
        parts = o_ref.shape[1] // slot_width
        for t in range(parts):
            cols = slice(t * slot_width, (t + 1) * slot_width)
            acc = jnp.dot(hn_ref[tile_slot(j * parts + t)], w_ref[:, cols], preferred_element_type=_F32)
            o_ref[:, cols] = acc.astype(o_ref.dtype)


def _norm_matmul(x, g, w, layer, *, tm, tn, n_tiles=None, col_tile=lambda j: j, dilations=(1,), tile_slot=None,
                 slot_width=None):
    m, k = x.shape
    n_tiles = w.shape[2] // tn if n_tiles is None else n_tiles
    xn_shape = (k // LANES, tm, LANES) if len(dilations) > 1 else (1, 8, LANES)
    return pl.pallas_call(
        functools.partial(_norm_matmul_kernel, dilations=dilations, tile_slot=tile_slot, slot_width=slot_width),
        grid=(m // tm, n_tiles),
        in_specs=[
            pl.BlockSpec((tm, k), lambda i, j: (i, 0)),
            pl.BlockSpec((1, k), lambda i, j: (0, 0)),
            pl.BlockSpec((None, k, tn), lambda i, j: (layer, 0, col_tile(j))),
        ],
        out_specs=pl.BlockSpec((tm, tn), lambda i, j: (i, j)),
        out_shape=jax.ShapeDtypeStruct((m, n_tiles * tn), _BF16),
        scratch_shapes=[pltpu.VMEM(xn_shape, _F32), pltpu.VMEM((len(dilations), tm, k), _BF16)],
        compiler_params=_cparams(("parallel", "arbitrary")),
        name="norm_matmul",
    )(x, g.reshape(1, k), w)


def _pick(j, values):
    out = values[-1]
    for idx in range(len(values) - 2, -1, -1):
        out = jnp.where(j == idx, values[idx], out)
    return out


def _norm_matmul_dual_kernel(x_ref, g_ref, wa_ref, wb_ref, ta_ref, tb_ref, nat_ref, tr_ref, hn_ref,
                             *, nat_steps, scales_a, scales_b):
    j = pl.program_id(1)
    tn = wa_ref.shape[1]

    @pl.when(j == 0)
    def _():
        hn_ref[...] = _rms(x_ref[...], g_ref[...]).astype(hn_ref.dtype)

    @pl.when(j < nat_steps)
    def _():
        nat_ref[:, :tn] = jnp.dot(hn_ref[...], wa_ref[...], preferred_element_type=_F32).astype(nat_ref.dtype)
        nat_ref[:, tn:] = jnp.dot(hn_ref[...], wb_ref[...], preferred_element_type=_F32).astype(nat_ref.dtype)

    @pl.when(j >= nat_steps)
    def _():
        acc = lax.dot_general(ta_ref[...], hn_ref[...], _NT, preferred_element_type=_F32)
        tr_ref[:tn, :] = (acc * _pick(j, scales_a)).astype(tr_ref.dtype)
        acc = lax.dot_general(tb_ref[...], hn_ref[...], _NT, preferred_element_type=_F32)
        tr_ref[tn:, :] = (acc * _pick(j, scales_b)).astype(tr_ref.dtype)


def _norm_matmul_dual(x, g, w, layer, wt, *, tm, tn, nat_cols, t_rows):
    m, k = x.shape
    assert len(nat_cols) % 2 == 0 and len(t_rows) % 2 == 0
    nat_steps, t_steps = len(nat_cols) // 2, len(t_rows) // 2
    ca = [nat_cols[2 * s] for s in range(nat_steps)] + [nat_cols[-2]] * t_steps
    cb = [nat_cols[2 * s + 1] for s in range(nat_steps)] + [nat_cols[-1]] * t_steps
    ra = [t_rows[0][0]] * nat_steps + [t_rows[2 * s][0] for s in range(t_steps)]
    rb = [t_rows[1][0]] * nat_steps + [t_rows[2 * s + 1][0] for s in range(t_steps)]
    scales_a = [1.0] * nat_steps + [float(t_rows[2 * s][1]) for s in range(t_steps)]
    scales_b = [1.0] * nat_steps + [float(t_rows[2 * s + 1][1]) for s in range(t_steps)]
    return pl.pallas_call(
        functools.partial(_norm_matmul_dual_kernel, nat_steps=nat_steps, scales_a=scales_a, scales_b=scales_b),
        grid=(m // tm, nat_steps + t_steps),
        in_specs=[
            pl.BlockSpec((tm, k), lambda i, j: (i, 0)),
            pl.BlockSpec((1, k), lambda i, j: (0, 0)),
            pl.BlockSpec((None, k, tn), lambda i, j: (layer, 0, _pick(j, ca))),
            pl.BlockSpec((None, k, tn), lambda i, j: (layer, 0, _pick(j, cb))),
            pl.BlockSpec((tn, k), lambda i, j: (_pick(j, ra), 0)),
            pl.BlockSpec((tn, k), lambda i, j: (_pick(j, rb), 0)),
        ],
        out_specs=[
            pl.BlockSpec((tm, 2 * tn), lambda i, j: (i, jnp.minimum(j, nat_steps - 1))),
            pl.BlockSpec((2 * tn, tm), lambda i, j: (jnp.maximum(j - nat_steps, 0), i)),
        ],
        out_shape=[
            jax.ShapeDtypeStruct((m, len(nat_cols) * tn), _BF16),
            jax.ShapeDtypeStruct((len(t_rows) * tn, m), _BF16),
        ],
        scratch_shapes=[pltpu.VMEM((tm, k), _BF16)],
        compiler_params=_cparams(("parallel", "arbitrary")),
        name="norm_matmul_dual",
    )(x, g.reshape(1, k), w, w, wt, wt)


def _diff_attn_kernel(lam_ref, sig_ref, sigp_ref, g_ref, kc_ref, qt_ref, k_ref, vt_ref, o_ref,
                      m_ref, acc_ref, *, tq, tk, hb, ones_rows, lam_init):
    qi = pl.program_id(2)
    hd = HEAD_DIM
    half = hd // 2
    reps = 2 * tq // LANES

    lp = lam_ref[...]
    lam = (jnp.exp(jnp.sum(lp[0:1] * lp[1:2], axis=1, keepdims=True))
           - jnp.exp(jnp.sum(lp[2:3] * lp[3:4], axis=1, keepdims=True)) + lam_init)

    row = lax.broadcasted_iota(jnp.int32, (hd, tq), 0)
    qzts, sigs = [], []
    for hh in range(hb):
        qt = qt_ref[hh * hd:(hh + 1) * hd, :]
        zero = jnp.zeros_like(qt)
        top = jnp.concatenate([jnp.where(row < half, qt, zero), jnp.where(row >= half, qt, zero)], axis=1)
        ext = jnp.concatenate([sigp_ref[hh]] * reps, axis=1).astype(_BF16)
        qzts.append(jnp.concatenate([top, ext], axis=0))
        sigs.append(sig_ref[hh][:, :1])
    key_i = lax.broadcasted_iota(jnp.int32, (tq, 2 * tq), 0)
    qry_i = lax.broadcasted_iota(jnp.int32, (tq, 2 * tq), 1)
    causal = key_i <= jnp.where(qry_i >= tq, qry_i - tq, qry_i)

    m_ref[...] = jnp.full(m_ref.shape, NEG, _F32)
    acc_ref[...] = jnp.zeros(acc_ref.shape, _F32)

    def step(start, size, masked):
        off = (start - qi * tq).astype(_F32)
        kc = kc_ref[:size, :]
        ones = jnp.ones((ones_rows, size), _BF16)
        ps = []
        for hh in range(hb):
            k = jnp.concatenate([k_ref[pl.ds(start, size), hh * hd:(hh + 1) * hd], kc], axis=1)
            c = sigs[hh] * off
            s = jnp.dot(k, qzts[hh], preferred_element_type=_F32)
            if masked:
                s = jnp.where(causal, s, NEG)
            m_prev = m_ref[hh]
            m_new = jnp.maximum(m_prev, jnp.max(s, axis=0, keepdims=True) + c)
            alpha = jnp.exp2(m_prev - m_new)
            p = jnp.exp2(s - (m_new - c))
            m_ref[hh] = m_new
            ps.append((p.astype(_BF16), alpha))
        for hh in range(hb):
            p, alpha = ps[hh]
            vt = jnp.concatenate([vt_ref[hh * hd:(hh + 1) * hd, pl.ds(start, size)], ones], axis=0)
            acc_ref[hh] = acc_ref[hh] * alpha + jnp.dot(vt, p, preferred_element_type=_F32)

    nb = (qi * tq) // tk

    def body(j, carry):
        step(pl.multiple_of(2 * j * tk, tk), tk, False)
        step(pl.multiple_of((2 * j + 1) * tk, tk), tk, False)
        return carry

    lax.fori_loop(0, nb // 2, body, 0)

    @pl.when(nb % 2 == 1)
    def _():
        step(pl.multiple_of((nb - 1) * tk, tk), tk, False)
    if tk != tq:
        @pl.when(qi % (tk // tq) == 1)
        def _():
            step(pl.multiple_of((qi - 1) * tq, tq), tq, False)
    step(pl.multiple_of(qi * tq, tq), tq, True)

    for hh in range(hb):
        ot = acc_ref[hh, :hd, :] / acc_ref[hh, hd:hd + 1, :]
        odt = ot[:, :tq] - lam * ot[:, tq:]
        ms = jnp.mean(odt * odt, axis=0, keepdims=True)
        yt = odt * lax.rsqrt(ms + NORM_EPS) * g_ref[...] * (1.0 - lam_init)
        o_ref[:, hh * hd:(hh + 1) * hd] = yt.T.astype(o_ref.dtype)


def _bf16_pieces(x):
    hi = x.astype(_BF16).astype(np.float32)
    mid = (x - hi).astype(_BF16).astype(np.float32)
    lo = (x - hi - mid).astype(_BF16).astype(np.float32)
    return hi, mid, lo


def _diff_attention(qvt, knat, lam_params, subln_g, slopes, *, batch, seq, n_heads, lam_init, tq, tk, hb):
    assert tk in (tq, 2 * tq)
    nq = seq // tq
    hd = HEAD_DIM
    ng = n_heads // hb
    sig = (slopes.astype(np.float64) * LOG2E).astype(np.float32)
    sig_arr = np.broadcast_to(sig.reshape(n_heads, 1, 1), (n_heads, 1, hd))
    sigp_arr = np.zeros((n_heads, hd, LANES), np.float32)
    for idx, piece in enumerate(_bf16_pieces(sig)):
        sigp_arr[:, idx, :] = piece[:, None]
        sigp_arr[:, 3 + idx, :] = piece[:, None]
    key = np.arange(tk)
    kc_arr = np.zeros((tk, LANES), np.float32)
    kc_arr[:, 0:3] = (key & 255)[:, None]
    kc_arr[:, 3:6] = (key - (key & 255))[:, None]
    ones_rows = 16
    kern = functools.partial(_diff_attn_kernel, tq=tq, tk=tk, hb=hb, ones_rows=ones_rows, lam_init=lam_init)
    return pl.pallas_call(
        kern,
        grid=(batch, ng, nq),
        in_specs=[
            pl.BlockSpec(lam_params.shape, lambda b, h, i: (0, 0)),
            pl.BlockSpec((hb, 1, hd), lambda b, h, i: (h, 0, 0)),
            pl.BlockSpec((hb, hd, LANES), lambda b, h, i: (h, 0, 0)),
            pl.BlockSpec((hd, 1), lambda b, h, i: (0, 0)),
            pl.BlockSpec((tk, LANES), lambda b, h, i: (0, 0)),
            pl.BlockSpec((hb * hd, tq), lambda b, h, i: (h, b * nq + i)),
            pl.BlockSpec((seq, hb * hd), lambda b, h, i: (b, h)),
            pl.BlockSpec((hb * hd, seq), lambda b, h, i: (ng + h, b)),
        ],
        out_specs=pl.BlockSpec((tq, hb * hd), lambda b, h, i: (b * nq + i, h)),
        out_shape=jax.ShapeDtypeStruct((batch * seq, n_heads * hd), _BF16),
        scratch_shapes=[
            pltpu.VMEM((hb, 1, 2 * tq), _F32),
            pltpu.VMEM((hb, hd + ones_rows, 2 * tq), _F32),
        ],
        compiler_params=_cparams(("parallel", "parallel", "arbitrary")),
        name="diff_attention",
    )(lam_params, jnp.asarray(sig_arr), jnp.asarray(sigp_arr), subln_g.reshape(hd, 1),
      jnp.asarray(kc_arr, dtype=_BF16), qvt, knat, qvt)


def _rows(ref, n, st, start, size, sl):
    if len(ref.shape) == 2:
        return ref[start:start + size, sl]
    return ref[start // n:(start + size) // n, st * n:(st + 1) * n, sl].reshape(size, sl.stop - sl.start)


def _store_rows(ref, n, st, start, size, sl, val):
    if len(ref.shape) == 2:
        ref[start:start + size, sl] = val
    else:
        ref[start // n:(start + size) // n, st * n:(st + 1) * n, sl] = val.reshape(size // n, n, sl.stop - sl.start)


def _dilated_kernel(*refs, tq, span, n, streams, slopes2, scale2, has_prev):
    if has_prev:
        q_ref, kp_ref, kc_ref, vp_ref, vc_ref, o_ref, lse_ref = refs
    else:
        q_ref, kc_ref, vc_ref, o_ref, lse_ref = refs
    length = q_ref.shape[0] if len(q_ref.shape) == 2 else q_ref.shape[0] * n
    first = pl.program_id(2) == 0
    nh = len(slopes2)
    lane_grp = lax.broadcasted_iota(jnp.int32, (tq, LANES), 1) // (LANES // nh)

    def masked_bias(nk, shift, first_block):
        row = lax.broadcasted_iota(jnp.int32, (tq, nk), 0)
        col = lax.broadcasted_iota(jnp.int32, (tq, nk), 1)
        dist = row - col + shift
        valid = (dist >= 0) & (dist <= span)
        if first_block and has_prev:
            valid = valid & ((col >= span) | jnp.logical_not(first))
        distf = dist.astype(_F32)
        return [jnp.where(valid, -s2 * distf, NEG) for s2 in slopes2]

    bias_head = masked_bias(tq + span if has_prev else tq, span if has_prev else 0, True)
    bias_body = masked_bias(tq + span, span, False) if length > tq else None

    for st in range(streams):
        for qb in range(length // tq):
            u0 = qb * tq
            lses = []
            for hh in range(nh):
                sl = slice(hh * HEAD_DIM, (hh + 1) * HEAD_DIM)
                q = _rows(q_ref, n, st, u0, tq, sl)
                if qb == 0 and has_prev:
                    k = jnp.concatenate([kp_ref[:, sl], _rows(kc_ref, n, st, 0, tq, sl)], axis=0)
                    v = jnp.concatenate([vp_ref[:, sl], _rows(vc_ref, n, st, 0, tq, sl)], axis=0)
                elif qb == 0:
                    k, v = _rows(kc_ref, n, st, 0, tq, sl), _rows(vc_ref, n, st, 0, tq, sl)
                else:
                    k = _rows(kc_ref, n, st, u0 - span, tq + span, sl)
                    v = _rows(vc_ref, n, st, u0 - span, tq + span, sl)
                bias = (bias_head if qb == 0 else bias_body)[hh]
                s = lax.dot_general(q, k, _NT, preferred_element_type=_F32) * scale2 + bias
                m = jnp.max(s, axis=1, keepdims=True)
                p = jnp.exp2(s - m)
                l = jnp.sum(p, axis=1, keepdims=True)
                o = jnp.dot(p.astype(_BF16), v, preferred_element_type=_F32) / l
                _store_rows(o_ref, n, st, u0, tq, sl, o.astype(o_ref.dtype))
                lses.append(m + jnp.log2(l))
            packed = jnp.broadcast_to(lses[nh - 1], (tq, LANES))
            for hh in range(nh - 2, -1, -1):
                packed = jnp.where(lane_grp == hh, lses[hh], packed)
            _store_rows(lse_ref, n, st, u0, tq, slice(0, LANES), packed)


def _dilated_group(proj, slopes_g, *, batch, seq, group, window, dilation, in_width, mix_width):
    span = window // dilation
    assert span == LANES, "key window per stream must be one 128-row block"
    gw = HEADS_PER_GROUP * HEAD_DIM
    koff = mix_width // gw
    rows = batch * seq
    n = PERM_ROWS // dilation
    tq = span if n >= span else 2 * span
    streams = max(1, LANES // n) if dilation > 1 else 1
    kern = functools.partial(
        _dilated_kernel, tq=tq, span=span, n=n, streams=streams,
        slopes2=tuple(float(s) * dilation * LOG2E for s in slopes_g), scale2=HEAD_DIM ** -0.5 * LOG2E,
        has_prev=dilation == 1)
    if dilation == 1:
        chunk = 1024
        nc = seq // chunk
        cur = lambda c: pl.BlockSpec((chunk, gw), lambda b, r, u: (b * nc + u, c))
        prev = lambda c: pl.BlockSpec(
            (span, gw), lambda b, r, u: (b * (seq // span) + jnp.maximum(u * (chunk // span) - 1, 0), c))
        in_specs = [cur(group), prev(koff + group), cur(koff + group), prev(2 * koff + group), cur(2 * koff + group)]
        operands = [proj] * 5
        out_specs = [pl.BlockSpec((chunk, w), lambda b, r, u: (b * nc + u, 0)) for w in (gw, LANES)]
        out_dims = [(rows, gw), (rows, LANES)]
        grid = (batch, 1, nc)
    else:
        tiles = seq // PERM_ROWS
        proj3 = proj.reshape(rows // PERM_ROWS, PERM_ROWS, in_width)
        blk = lambda c, w: pl.BlockSpec((tiles, streams * n, w), lambda b, r, u: (b, r, c))
        in_specs = [blk(group, gw), blk(koff + group, gw), blk(2 * koff + group, gw)]
        operands = [proj3] * 3
        out_specs = [blk(0, gw), blk(0, LANES)]
        out_dims = [(rows // PERM_ROWS, PERM_ROWS, gw), (rows // PERM_ROWS, PERM_ROWS, LANES)]
        grid = (batch, dilation // streams, 1)
    o, lse = pl.pallas_call(
        kern,
        grid=grid,
        in_specs=in_specs,
        out_specs=out_specs,
        out_shape=[jax.ShapeDtypeStruct(out_dims[0], _BF16), jax.ShapeDtypeStruct(out_dims[1], _F32)],
        compiler_params=_cparams(("parallel", "parallel", "arbitrary")),
        name=f"dilated_attention_d{dilation}",
    )(*operands)
    return o.reshape(rows, gw), lse.reshape(rows, LANES)


def _combine_kernel(o0_ref, o1_ref, o2_ref, l0_ref, l1_ref, l2_ref, out_ref, on_ref, ln_ref, *, dilations):
    tm = out_ref.shape[0]
    gw = o0_ref.shape[1]
    nh = gw // LANES
    for g, (o_ref, l_ref, d) in enumerate(zip((o0_ref, o1_ref, o2_ref), (l0_ref, l1_ref, l2_ref), dilations)):
        n = PERM_ROWS // d
        if d == 1:
            ln_ref[g] = l_ref[...]
            for hh in range(nh):
                on_ref[g, hh] = o_ref[:, hh * LANES:(hh + 1) * LANES].astype(_F32)
            continue
        for grp in range(tm // PERM_ROWS):
            for r in range(d):
                src = slice(grp * PERM_ROWS + r * n, grp * PERM_ROWS + (r + 1) * n)
                dst = pl.ds(grp * PERM_ROWS + r, n, stride=d)
                ln_ref[g, dst, :] = l_ref[src, :]
                for hh in range(nh):
                    on_ref[g, hh, dst, :] = o_ref[src, hh * LANES:(hh + 1) * LANES].astype(_F32)
    l0, l1, l2 = ln_ref[0], ln_ref[1], ln_ref[2]
    mx = jnp.maximum(jnp.maximum(l0, l1), l2)
    es = (jnp.exp2(l0 - mx), jnp.exp2(l1 - mx), jnp.exp2(l2 - mx))
    den = es[0] + es[1] + es[2]
    for g in range(3):
        wg = es[g] / den
        for hh in range(nh):
            lane0 = hh * (LANES // nh)
            col = g * gw + hh * LANES
            out_ref[:, col:col + LANES] = (on_ref[g, hh] * wg[:, lane0:lane0 + 1]).astype(out_ref.dtype)


def _combine_groups(outs, lses, dilations, *, tm):
    m, gw = outs[0].shape
    spec = pl.BlockSpec((tm, gw), lambda i: (i, 0))
    return pl.pallas_call(
        functools.partial(_combine_kernel, dilations=dilations),
        grid=(m // tm,),
        in_specs=[spec] * 3 + [pl.BlockSpec((tm, LANES), lambda i: (i, 0))] * 3,
        out_specs=pl.BlockSpec((tm, 3 * gw), lambda i: (i, 0)),
        out_shape=jax.ShapeDtypeStruct((m, 3 * gw), _BF16),
        scratch_shapes=[pltpu.VMEM((3, gw // LANES, tm, LANES), _F32), pltpu.VMEM((3, tm, LANES), _F32)],
        compiler_params=_cparams(("parallel",)),
        name="combine_groups",
    )(*outs, *lses)


def _mem_attn_kernel(q_ref, k_ref, v_ref, o_ref, *, scale):
    for hh in range(N_MEM_HEADS):
        sl = slice(hh * HEAD_DIM, (hh + 1) * HEAD_DIM)
        s = lax.dot_general(q_ref[:, sl], k_ref[:, sl], _NT, preferred_element_type=_F32) * scale
        m = jnp.max(s, axis=1, keepdims=True)
        p = jnp.exp(s - m)
        l = jnp.sum(p, axis=1, keepdims=True)
        o = jnp.dot(p.astype(_BF16), v_ref[:, sl], preferred_element_type=_F32)
        o_ref[:, sl] = (o / l).astype(o_ref.dtype)


def _memory_attention(proj, kvm, *, batch, seq, mem_tokens, q_col_block, tq):
    nq = seq // tq
    mw = N_MEM_HEADS * HEAD_DIM
    return pl.pallas_call(
        functools.partial(_mem_attn_kernel, scale=HEAD_DIM ** -0.5),
        grid=(batch, nq),
        in_specs=[
            pl.BlockSpec((tq, mw), lambda b, i: (b * nq + i, q_col_block)),
            pl.BlockSpec((mem_tokens, mw), lambda b, i: (b, 0)),
            pl.BlockSpec((mem_tokens, mw), lambda b, i: (b, 1)),
        ],
        out_specs=pl.BlockSpec((tq, mw), lambda b, i: (b * nq + i, 0)),
        out_shape=jax.ShapeDtypeStruct((batch * seq, mw), _BF16),
        compiler_params=_cparams(("parallel", "parallel")),
        name="memory_attention",
    )(proj, kvm, kvm)


def _out_proj_kernel(x_ref, a_ref, b_ref, wa_ref, wb_ref, o_ref):
    acc = jnp.dot(a_ref[...], wa_ref[...], preferred_element_type=_F32)
    acc = acc + jnp.dot(b_ref[...], wb_ref[...], preferred_element_type=_F32)
    o_ref[...] = x_ref[...] + acc


def _out_proj(x, o_mix, o_mem, w, layer, *, tm):
    m, d = x.shape
    ka, kb = o_mix.shape[1], o_mem.shape[1]
    assert ka % kb == 0
    return pl.pallas_call(
        _out_proj_kernel,
        grid=(m // tm,),
        in_specs=[
            pl.BlockSpec((tm, d), lambda i: (i, 0)),
            pl.BlockSpec((tm, ka), lambda i: (i, 0)),
            pl.BlockSpec((tm, kb), lambda i: (i, 0)),
            pl.BlockSpec((None, ka, d), lambda i: (layer, 0, 0)),
            pl.BlockSpec((None, kb, d), lambda i: (layer, ka // kb, 0)),
        ],
        out_specs=pl.BlockSpec((tm, d), lambda i: (i, 0)),
        out_shape=jax.ShapeDtypeStruct((m, d), _F32),
        compiler_params=_cparams(("parallel",)),
        name="out_proj",
    )(x, o_mix, o_mem, w, w)


def _mlp_kernel(x_ref, g_ref, w1_ref, w2_ref, gf_ref, o_ref, hn_ref, *, final_norm):
    f = pl.program_id(1)

    @pl.when(f == 0)
    def _():
        x = x_ref[...]
        hn_ref[...] = _rms(x, g_ref[...]).astype(hn_ref.dtype)
        o_ref[...] = x

    a = jnp.dot(hn_ref[...], w1_ref[...], preferred_element_type=_F32)
    a = jnp.square(jnp.maximum(a, 0.0)).astype(_BF16)
    o_ref[...] += jnp.dot(a, w2_ref[...], preferred_element_type=_F32)

    if final_norm:
        @pl.when(f == pl.num_programs(1) - 1)
        def _():
            o_ref[...] = _rms(o_ref[...], gf_ref[...])


def _mlp(x, g, w1, w2, layer, g_final, *, final_norm, tm, tf):
    m, d = x.shape
    ff = w1.shape[2]
    return pl.pallas_call(
        functools.partial(_mlp_kernel, final_norm=final_norm),
        grid=(m // tm, ff // tf),
        in_specs=[
            pl.BlockSpec((tm, d), lambda i, f: (i, 0)),
            pl.BlockSpec((1, d), lambda i, f: (0, 0)),
            pl.BlockSpec((None, d, tf), lambda i, f: (layer, 0, f)),
            pl.BlockSpec((None, tf, d), lambda i, f: (layer, f, 0)),
            pl.BlockSpec((1, d), lambda i, f: (0, 0)),
        ],
        out_specs=pl.BlockSpec((tm, d), lambda i, f: (i, 0)),
        out_shape=jax.ShapeDtypeStruct((m, d), _F32),
        scratch_shapes=[pltpu.VMEM((tm, d), _BF16)],
        compiler_params=_cparams(("parallel", "arbitrary")),
        name="mlp",
    )(x, g.reshape(1, d), w1, w2, g_final.reshape(1, d))


def kernel(x, mem, g_attn, w_in, w_out, lambda_qk, diff_subln_g, g_mem, w_mem_kv, g_mlp, w_mlp1, w_mlp2, g_final):
    batch, seq, d_model = x.shape
    depth = w_in.shape[0]
    mem_tokens = mem.shape[1]
    in_width = w_in.shape[2]
    mem_width = N_MEM_HEADS * HEAD_DIM
    mix_width = (in_width - mem_width) // 3
    n_heads = mix_width // HEAD_DIM
    slopes = _alibi_slopes(n_heads)
    dilations = tuple(d for _, d in DILATED_GROUPS)
    gw = HEADS_PER_GROUP * HEAD_DIM
    n_qkv_tiles = 3 * mix_width // gw

    xf = x.reshape(batch * seq, d_model)
    memf = mem.reshape(batch * mem_tokens, d_model)
    w_in_b, w_kv_b, w_out_b = w_in.astype(_BF16), w_mem_kv.astype(_BF16), w_out.astype(_BF16)
    w1_b, w2_b = w_mlp1.astype(_BF16), w_mlp2.astype(_BF16)
    mix_tiles = mix_width // gw
    for i in range(depth):
        kvm = _norm_matmul(memf, g_mem, w_kv_b, i, tm=512, tn=512)
        if i % N_MIXERS == 0:
            j = i // N_MIXERS
            lam_init = 0.8 - 0.6 * math.exp(-0.3 * i)
            q_scale = LOG2E * (HEAD_DIM // 2) ** -0.5
            proj, qvt = _norm_matmul_dual(
                xf, g_attn[i], w_in_b, i, jnp.transpose(w_in[i]).astype(_BF16), tm=1024, tn=gw,
                nat_cols=[mix_tiles + t for t in range(mix_tiles)] + [3 * mix_tiles],
                t_rows=[(t, q_scale) for t in range(mix_tiles)] + [(2 * mix_tiles + t, 1.0) for t in range(mix_tiles)])
            o_mix = _diff_attention(qvt, proj, lambda_qk[j], diff_subln_g[j], slopes, batch=batch, seq=seq,
                                    n_heads=n_heads, lam_init=lam_init, tq=256, tk=512, hb=6)
            qm_col_block = mix_tiles
        else:
            tile_slot = lambda t: jnp.where(t < n_qkv_tiles, t % len(dilations), 0)
            proj = _norm_matmul(xf, g_attn[i], w_in_b, i, tm=1024, tn=2 * gw, dilations=dilations,
                                tile_slot=tile_slot, slot_width=gw)
            outs, lses = [], []
            for g, (window, dilation) in enumerate(DILATED_GROUPS):
                sl = slice(g * HEADS_PER_GROUP, (g + 1) * HEADS_PER_GROUP)
                o, lse = _dilated_group(proj, slopes[sl], batch=batch, seq=seq, group=g, window=window,
                                        dilation=dilation, in_width=in_width, mix_width=mix_width)
                outs.append(o)
                lses.append(lse)
            o_mix = _combine_groups(outs, lses, dilations, tm=PERM_ROWS)
            qm_col_block = 3 * mix_tiles
        o_mem = _memory_attention(proj, kvm, batch=batch, seq=seq, mem_tokens=mem_tokens,
                                  q_col_block=qm_col_block, tq=512)
        xf = _out_proj(xf, o_mix, o_mem, w_out_b, i, tm=512)
        xf = _mlp(xf, g_mlp[i], w1_b, w2_b, i, g_final, final_norm=(i == depth - 1), tm=1024, tf=512)
    return xf.reshape(batch, seq, d_model)
```

```python
import functools
import math

import numpy as np
import jax
import jax.numpy as jnp
from jax import lax
from jax.experimental import pallas as pl
from jax.experimental.pallas import tpu as pltpu

HEAD_DIM = 128
N_MEM_HEADS = 4
N_MIXERS = 2
DILATED_GROUPS = ((128, 1), (512, 4), (2048, 16))
HEADS_PER_GROUP = 4
NORM_EPS = 1e-6
NEG = -1e30
LANES = 128
VMEM_LIMIT = 56 * 1024 * 1024
PERM_ROWS = 512
LOG2E = math.log2(math.e)

_F32 = jnp.float32
_BF16 = jnp.bfloat16
_NT = (((1,), (1,)), ((), ()))


def _alibi_slopes(n):
    def pow2(m):
        start = 2.0 ** (-(2.0 ** -(math.log2(m) - 3)))
        return [start * start ** i for i in range(m)]

    def slopes(m):
        if math.log2(m).is_integer():
            return pow2(m)
        c = 2 ** math.floor(math.log2(m))
        return pow2(c) + slopes(2 * c)[0::2][: m - c]

    return np.asarray(sorted(slopes(n), reverse=True), dtype=np.float32)


def _cparams(semantics):
    return pltpu.CompilerParams(dimension_semantics=semantics, vmem_limit_bytes=VMEM_LIMIT)


def _rms(x, g):
    ms = jnp.mean(x * x, axis=-1, keepdims=True)
    return x * lax.rsqrt(ms + NORM_EPS) * g


def _norm_matmul_kernel(x_ref, g_ref, w_ref, o_ref, xn_ref, hn_ref, *, dilations, tile_slot, slot_width):
    j = pl.program_id(1)
    tm = x_ref.shape[0]

    @pl.when(j == 0)
    def _():
        xn = _rms(x_ref[...], g_ref[...])
        hn_ref[0] = xn.astype(hn_ref.dtype)
        if len(dilations) == 1:
            return
        for c in range(xn_ref.shape[0]):
            xn_ref[c] = xn[:, c * LANES:(c + 1) * LANES]

        def permute_chunk(c, carry):
            lanes = pl.ds(pl.multiple_of(c * LANES, LANES), LANES)
            for s, d in enumerate(dilations):
                if d == 1:
                    continue
                n = PERM_ROWS // d
                for grp in range(tm // PERM_ROWS):
                    for r in range(d):
                        rows = xn_ref[c, pl.ds(grp * PERM_ROWS + r, n, stride=d), :]
                        dst = grp * PERM_ROWS + r * n
                        hn_ref[s, dst:dst + n, lanes] = rows.astype(hn_ref.dtype)
            return carry

        lax.fori_loop(0, xn_ref.shape[0], permute_chunk, 0)

    if len(dilations) == 1:
        o_ref[...] = jnp.dot(hn_ref[0], w_ref[...], preferred_element_type=_F32).astype(o_ref.dtype)
    else:
        parts = o_ref.shape[1] // slot_width
        for t in range(parts):
            cols = slice(t * slot_width, (t + 1) * slot_width)
            acc = jnp.dot(hn_ref[tile_slot(j * parts + t)], w_ref[:, cols], preferred_element_type=_F32)
            o_ref[:, cols] = acc.astype(o_ref.dtype)


def _norm_matmul(x, g, w, layer, *, tm, tn, n_tiles=None, col_tile=lambda j: j, dilations=(1,), tile_slot=None,
                 slot_width=None):
    m, k = x.shape
    n_tiles = w.shape[2] // tn if n_tiles is None else n_tiles
    xn_shape = (k // LANES, tm, LANES) if len(dilations) > 1 else (1, 8, LANES)
    return pl.pallas_call(
        functools.partial(_norm_matmul_kernel, dilations=dilations, tile_slot=tile_slot, slot_width=slot_width),
        grid=(m // tm, n_tiles),
        in_specs=[
            pl.BlockSpec((tm, k), lambda i, j: (i, 0)),
            pl.BlockSpec((1, k), lambda i, j: (0, 0)),
            pl.BlockSpec((None, k, tn), lambda i, j: (layer, 0, col_tile(j))),
        ],
        out_specs=pl.BlockSpec((tm, tn), lambda i, j: (i, j)),
        out_shape=jax.ShapeDtypeStruct((m, n_tiles * tn), _BF16),
        scratch_shapes=[pltpu.VMEM(xn_shape, _F32), pltpu.VMEM((len(dilations), tm, k), _BF16)],
        compiler_params=_cparams(("parallel", "arbitrary")),
        name="norm_matmul",
    )(x, g.reshape(1, k), w)


def _pick(j, values):
    out = values[-1]
    for idx in range(len(values) - 2, -1, -1):
        out = jnp.where(j == idx, values[idx], out)
    return out


def _norm_matmul_dual_kernel(x_ref, g_ref, wa_ref, wb_ref, ta_ref, tb_ref, nat_ref, tr_ref, hn_ref,
                             *, nat_steps, scales_a, scales_b):
    j = pl.program_id(1)
    tn = wa_ref.shape[1]

    @pl.when(j == 0)
    def _():
        hn_ref[...] = _rms(x_ref[...], g_ref[...]).astype(hn_ref.dtype)

    @pl.when(j < nat_steps)
    def _():
        nat_ref[:, :tn] = jnp.dot(hn_ref[...], wa_ref[...], preferred_element_type=_F32).astype(nat_ref.dtype)
        nat_ref[:, tn:] = jnp.dot(hn_ref[...], wb_ref[...], preferred_element_type=_F32).astype(nat_ref.dtype)

    @pl.when(j >= nat_steps)
    def _():
        acc = lax.dot_general(ta_ref[...], hn_ref[...], _NT, preferred_element_type=_F32)
        tr_ref[:tn, :] = (acc * _pick(j, scales_a)).astype(tr_ref.dtype)
        acc = lax.dot_general(tb_ref[...], hn_ref[...], _NT, preferred_element_type=_F32)
        tr_ref[tn:, :] = (acc * _pick(j, scales_b)).astype(tr_ref.dtype)


def _norm_matmul_dual(x, g, w, layer, wt, *, tm, tn, nat_cols, t_rows):
    m, k = x.shape
    assert len(nat_cols) % 2 == 0 and len(t_rows) % 2 == 0
    nat_steps, t_steps = len(nat_cols) // 2, len(t_rows) // 2
    ca = [nat_cols[2 * s] for s in range(nat_steps)] + [nat_cols[-2]] * t_steps
    cb = [nat_cols[2 * s + 1] for s in range(nat_steps)] + [nat_cols[-1]] * t_steps
    ra = [t_rows[0][0]] * nat_steps + [t_rows[2 * s][0] for s in range(t_steps)]
    rb = [t_rows[1][0]] * nat_steps + [t_rows[2 * s + 1][0] for s in range(t_steps)]
    scales_a = [1.0] * nat_steps + [float(t_rows[2 * s][1]) for s in range(t_steps)]
    scales_b = [1.0] * nat_steps + [float(t_rows[2 * s + 1][1]) for s in range(t_steps)]
    return pl.pallas_call(
        functools.partial(_norm_matmul_dual_kernel, nat_steps=nat_steps, scales_a=scales_a, scales_b=scales_b),
        grid=(m // tm, nat_steps + t_steps),
        in_specs=[
            pl.BlockSpec((tm, k), lambda i, j: (i, 0)),
            pl.BlockSpec((1, k), lambda i, j: (0, 0)),
            pl.BlockSpec((None, k, tn), lambda i, j: (layer, 0, _pick(j, ca))),
            pl.BlockSpec((None, k, tn), lambda i, j: (layer, 0, _pick(j, cb))),
            pl.BlockSpec((tn, k), lambda i, j: (_pick(j, ra), 0)),
            pl.BlockSpec((tn, k), lambda i, j: (_pick(j, rb), 0)),
        ],
        out_specs=[
            pl.BlockSpec((tm, 2 * tn), lambda i, j: (i, jnp.minimum(j, nat_steps - 1))),
            pl.BlockSpec((2 * tn, tm), lambda i, j: (jnp.maximum(j - nat_steps, 0), i)),
        ],
        out_shape=[
            jax.ShapeDtypeStruct((m, len(nat_cols) * tn), _BF16),
            jax.ShapeDtypeStruct((len(t_rows) * tn, m), _BF16),
        ],
        scratch_shapes=[pltpu.VMEM((tm, k), _BF16)],
        compiler_params=_cparams(("parallel", "arbitrary")),
        name="norm_matmul_dual",
    )(x, g.reshape(1, k), w, w, wt, wt)


def _diff_attn_kernel(*refs, n_cast, tq, tk, hb, ones_rows, lam_init):
    lam_ref, sig_ref, sigp_ref, g_ref, kc_ref, qt_ref, k_ref, vt_ref = refs[:8]
    cast_in = refs[8:8 + n_cast]
    o_ref = refs[8 + n_cast]
    cast_out = refs[9 + n_cast:9 + 2 * n_cast]
    m_ref, acc_ref = refs[9 + 2 * n_cast:]
    for src, dst in zip(cast_in, cast_out):
        dst[...] = src[...].astype(dst.dtype)
    qi = pl.program_id(2)
    hd = HEAD_DIM
    half = hd // 2
    reps = 2 * tq // LANES

    lp = lam_ref[...]
    lam = (jnp.exp(jnp.sum(lp[0:1] * lp[1:2], axis=1, keepdims=True))
           - jnp.exp(jnp.sum(lp[2:3] * lp[3:4], axis=1, keepdims=True)) + lam_init)

    row = lax.broadcasted_iota(jnp.int32, (hd, tq), 0)
    qzts, sigs = [], []
    for hh in range(hb):
        qt = qt_ref[hh * hd:(hh + 1) * hd, :]
        zero = jnp.zeros_like(qt)
        top = jnp.concatenate([jnp.where(row < half, qt, zero), jnp.where(row >= half, qt, zero)], axis=1)
        ext = jnp.concatenate([sigp_ref[hh]] * reps, axis=1).astype(_BF16)
        qzts.append(jnp.concatenate([top, ext], axis=0))
        sigs.append(sig_ref[hh][:, :1])
    key_i = lax.broadcasted_iota(jnp.int32, (tq, 2 * tq), 0)
    qry_i = lax.broadcasted_iota(jnp.int32, (tq, 2 * tq), 1)
    causal = key_i <= jnp.where(qry_i >= tq, qry_i - tq, qry_i)

    m_ref[...] = jnp.full(m_ref.shape, NEG, _F32)
    acc_ref[...] = jnp.zeros(acc_ref.shape, _F32)

    def step(start, size, masked):
        off = (start - qi * tq).astype(_F32)
        kc = kc_ref[:size, :]
        ones = jnp.ones((ones_rows, size), _BF16)
        ps = []
        for hh in range(hb):
            k = jnp.concatenate([k_ref[pl.ds(start, size), hh * hd:(hh + 1) * hd], kc], axis=1)
            c = sigs[hh] * off
            s = jnp.dot(k, qzts[hh], preferred_element_type=_F32)
            if masked:
                s = jnp.where(causal, s, NEG)
            m_prev = m_ref[hh]
            m_new = jnp.maximum(m_prev, jnp.max(s, axis=0, keepdims=True) + c)
            alpha = jnp.exp2(m_prev - m_new)
            p = jnp.exp2(s - (m_new - c))
            m_ref[hh] = m_new
            ps.append((p.astype(_BF16), alpha))
        for hh in range(hb):
            p, alpha = ps[hh]
            vt = jnp.concatenate([vt_ref[hh * hd:(hh + 1) * hd, pl.ds(start, size)], ones], axis=0)
            acc_ref[hh] = acc_ref[hh] * alpha + jnp.dot(vt, p, preferred_element_type=_F32)

    nb = (qi * tq) // tk

    def body(j, carry):
        step(pl.multiple_of(2 * j * tk, tk), tk, False)
        step(pl.multiple_of((2 * j + 1) * tk, tk), tk, False)
        return carry

    lax.fori_loop(0, nb // 2, body, 0)

    @pl.when(nb % 2 == 1)
    def _():
        step(pl.multiple_of((nb - 1) * tk, tk), tk, False)
    if tk != tq:
        @pl.when(qi % (tk // tq) == 1)
        def _():
            step(pl.multiple_of((qi - 1) * tq, tq), tq, False)
    step(pl.multiple_of(qi * tq, tq), tq, True)

    for hh in range(hb):
        ot = acc_ref[hh, :hd, :] / acc_ref[hh, hd:hd + 1, :]
        odt = ot[:, :tq] - lam * ot[:, tq:]
        ms = jnp.mean(odt * odt, axis=0, keepdims=True)
        yt = odt * lax.rsqrt(ms + NORM_EPS) * g_ref[...] * (1.0 - lam_init)
        o_ref[:, hh * hd:(hh + 1) * hd] = yt.T.astype(o_ref.dtype)


def _bf16_pieces(x):
    hi = x.astype(_BF16).astype(np.float32)
    mid = (x - hi).astype(_BF16).astype(np.float32)
    lo = (x - hi - mid).astype(_BF16).astype(np.float32)
    return hi, mid, lo


def _diff_attention(qvt, knat, lam_params, subln_g, slopes, casts, *, batch, seq, n_heads, lam_init, tq, tk, hb):
    assert tk in (tq, 2 * tq)
    nq = seq // tq
    hd = HEAD_DIM
    ng = n_heads // hb
    sig = (slopes.astype(np.float64) * LOG2E).astype(np.float32)
    sig_arr = np.broadcast_to(sig.reshape(n_heads, 1, 1), (n_heads, 1, hd))
    sigp_arr = np.zeros((n_heads, hd, LANES), np.float32)
    for idx, piece in enumerate(_bf16_pieces(sig)):
        sigp_arr[:, idx, :] = piece[:, None]
        sigp_arr[:, 3 + idx, :] = piece[:, None]
    key = np.arange(tk)
    kc_arr = np.zeros((tk, LANES), np.float32)
    kc_arr[:, 0:3] = (key & 255)[:, None]
    kc_arr[:, 3:6] = (key - (key & 255))[:, None]
    ones_rows = 16
    steps = batch * ng * nq
    cast_2d = [c.reshape(-1, c.shape[-1]) for c in casts]
    assert all(c.shape[0] % (16 * steps) == 0 for c in cast_2d), "cast slabs must be whole bf16 sublane tiles"
    cast_specs = [pl.BlockSpec((c.shape[0] // steps, c.shape[1]), lambda b, h, i: ((b * ng + h) * nq + i, 0))
                  for c in cast_2d]
    kern = functools.partial(_diff_attn_kernel, n_cast=len(casts), tq=tq, tk=tk, hb=hb, ones_rows=ones_rows,
                             lam_init=lam_init)
    outs = pl.pallas_call(
        kern,
        grid=(batch, ng, nq),
        in_specs=[
            pl.BlockSpec(lam_params.shape, lambda b, h, i: (0, 0)),
            pl.BlockSpec((hb, 1, hd), lambda b, h, i: (h, 0, 0)),
            pl.BlockSpec((hb, hd, LANES), lambda b, h, i: (h, 0, 0)),
            pl.BlockSpec((hd, 1), lambda b, h, i: (0, 0)),
            pl.BlockSpec((tk, LANES), lambda b, h, i: (0, 0)),
            pl.BlockSpec((hb * hd, tq), lambda b, h, i: (h, b * nq + i)),
            pl.BlockSpec((seq, hb * hd), lambda b, h, i: (b, h)),
            pl.BlockSpec((hb * hd, seq), lambda b, h, i: (ng + h, b)),
        ] + cast_specs,
        out_specs=[pl.BlockSpec((tq, hb * hd), lambda b, h, i: (b * nq + i, h))] + cast_specs,
        out_shape=[jax.ShapeDtypeStruct((batch * seq, n_heads * hd), _BF16)]
        + [jax.ShapeDtypeStruct(c.shape, _BF16) for c in cast_2d],
        scratch_shapes=[
            pltpu.VMEM((hb, 1, 2 * tq), _F32),
            pltpu.VMEM((hb, hd + ones_rows, 2 * tq), _F32),
        ],
        compiler_params=_cparams(("parallel", "parallel", "arbitrary")),
        name="diff_attention",
    )(lam_params, jnp.asarray(sig_arr), jnp.asarray(sigp_arr), subln_g.reshape(hd, 1),
      jnp.asarray(kc_arr, dtype=_BF16), qvt, knat, qvt, *cast_2d)
    return outs[0], [o.reshape(c.shape) for o, c in zip(outs[1:], casts)]


def _rows(ref, n, st, start, size, sl):
    if len(ref.shape) == 2:
        return ref[start:start + size, sl]
    return ref[start // n:(start + size) // n, st * n:(st + 1) * n, sl].reshape(size, sl.stop - sl.start)


def _store_rows(ref, n, st, start, size, sl, val):
    if len(ref.shape) == 2:
        ref[start:start + size, sl] = val
    else:
        ref[start // n:(start + size) // n, st * n:(st + 1) * n, sl] = val.reshape(size // n, n, sl.stop - sl.start)


def _dilated_kernel(*refs, tq, span, n, streams, slopes2, scale2, has_prev):
    if has_prev:
        q_ref, kp_ref, kc_ref, vp_ref, vc_ref, o_ref, lse_ref = refs
    else:
        q_ref, kc_ref, vc_ref, o_ref, lse_ref = refs
    length = q_ref.shape[0] if len(q_ref.shape) == 2 else q_ref.shape[0] * n
    first = pl.program_id(2) == 0
    nh = len(slopes2)
    lane_grp = lax.broadcasted_iota(jnp.int32, (tq, LANES), 1) // (LANES // nh)

    def masked_bias(nk, shift, first_block):
        row = lax.broadcasted_iota(jnp.int32, (tq, nk), 0)
        col = lax.broadcasted_iota(jnp.int32, (tq, nk), 1)
        dist = row - col + shift
        valid = (dist >= 0) & (dist <= span)
        if first_block and has_prev:
            valid = valid & ((col >= span) | jnp.logical_not(first))
        distf = dist.astype(_F32)
        return [jnp.where(valid, -s2 * distf, NEG) for s2 in slopes2]

    bias_head = masked_bias(tq + span if has_prev else tq, span if has_prev else 0, True)
    bias_body = masked_bias(tq + span, span, False) if length > tq else None

    for st in range(streams):
        for qb in range(length // tq):
            u0 = qb * tq
            lses = []
            for hh in range(nh):
                sl = slice(hh * HEAD_DIM, (hh + 1) * HEAD_DIM)
                q = _rows(q_ref, n, st, u0, tq, sl)
                if qb == 0 and has_prev:
                    k = jnp.concatenate([kp_ref[:, sl], _rows(kc_ref, n, st, 0, tq, sl)], axis=0)
                    v = jnp.concatenate([vp_ref[:, sl], _rows(vc_ref, n, st, 0, tq, sl)], axis=0)
                elif qb == 0:
                    k, v = _rows(kc_ref, n, st, 0, tq, sl), _rows(vc_ref, n, st, 0, tq, sl)
                else:
                    k = _rows(kc_ref, n, st, u0 - span, tq + span, sl)
                    v = _rows(vc_ref, n, st, u0 - span, tq + span, sl)
                bias = (bias_head if qb == 0 else bias_body)[hh]
                s = lax.dot_general(q, k, _NT, preferred_element_type=_F32) * scale2 + bias
                m = jnp.max(s, axis=1, keepdims=True)
                p = jnp.exp2(s - m)
                l = jnp.sum(p, axis=1, keepdims=True)
                o = jnp.dot(p.astype(_BF16), v, preferred_element_type=_F32) / l
                _store_rows(o_ref, n, st, u0, tq, sl, o.astype(o_ref.dtype))
                lses.append(m + jnp.log2(l))
            packed = jnp.broadcast_to(lses[nh - 1], (tq, LANES))
            for hh in range(nh - 2, -1, -1):
                packed = jnp.where(lane_grp == hh, lses[hh], packed)
            _store_rows(lse_ref, n, st, u0, tq, slice(0, LANES), packed)


def _dilated_group(proj, slopes_g, *, batch, seq, group, window, dilation, in_width, mix_width):
    span = window // dilation
    assert span == LANES, "key window per stream must be one 128-row block"
    gw = HEADS_PER_GROUP * HEAD_DIM
    koff = mix_width // gw
    rows = batch * seq
    n = PERM_ROWS // dilation
    tq = span if n >= span else 2 * span
    streams = max(1, LANES // n) if dilation > 1 else 1
    kern = functools.partial(
        _dilated_kernel, tq=tq, span=span, n=n, streams=streams,
        slopes2=tuple(float(s) * dilation * LOG2E for s in slopes_g), scale2=HEAD_DIM ** -0.5 * LOG2E,
        has_prev=dilation == 1)
    if dilation == 1:
        chunk = 1024
        nc = seq // chunk
        cur = lambda c: pl.BlockSpec((chunk, gw), lambda b, r, u: (b * nc + u, c))
        prev = lambda c: pl.BlockSpec(
            (span, gw), lambda b, r, u: (b * (seq // span) + jnp.maximum(u * (chunk // span) - 1, 0), c))
        in_specs = [cur(group), prev(koff + group), cur(koff + group), prev(2 * koff + group), cur(2 * koff + group)]
        operands = [proj] * 5
        out_specs = [pl.BlockSpec((chunk, w), lambda b, r, u: (b * nc + u, 0)) for w in (gw, LANES)]
        out_dims = [(rows, gw), (rows, LANES)]
        grid = (batch, 1, nc)
    else:
        tiles = seq // PERM_ROWS
        proj3 = proj.reshape(rows // PERM_ROWS, PERM_ROWS, in_width)
        blk = lambda c, w: pl.BlockSpec((tiles, streams * n, w), lambda b, r, u: (b, r, c))
        in_specs = [blk(group, gw), blk(koff + group, gw), blk(2 * koff + group, gw)]
        operands = [proj3] * 3
        out_specs = [blk(0, gw), blk(0, LANES)]
        out_dims = [(rows // PERM_ROWS, PERM_ROWS, gw), (rows // PERM_ROWS, PERM_ROWS, LANES)]
        grid = (batch, dilation // streams, 1)
    o, lse = pl.pallas_call(
        kern,
        grid=grid,
        in_specs=in_specs,
        out_specs=out_specs,
        out_shape=[jax.ShapeDtypeStruct(out_dims[0], _BF16), jax.ShapeDtypeStruct(out_dims[1], _F32)],
        compiler_params=_cparams(("parallel", "parallel", "arbitrary")),
        name=f"dilated_attention_d{dilation}",
    )(*operands)
    return o.reshape(rows, gw), lse.reshape(rows, LANES)


def _combine_kernel(o0_ref, o1_ref, o2_ref, l0_ref, l1_ref, l2_ref, out_ref, on_ref, ln_ref, *, dilations):
    tm = out_ref.shape[0]
    gw = o0_ref.shape[1]
    nh = gw // LANES
    for g, (o_ref, l_ref, d) in enumerate(zip((o0_ref, o1_ref, o2_ref), (l0_ref, l1_ref, l2_ref), dilations)):
        n = PERM_ROWS // d
        if d == 1:
            ln_ref[g] = l_ref[...]
            for hh in range(nh):
                on_ref[g, hh] = o_ref[:, hh * LANES:(hh + 1) * LANES].astype(_F32)
            continue
        for grp in range(tm // PERM_ROWS):
            for r in range(d):
                src = slice(grp * PERM_ROWS + r * n, grp * PERM_ROWS + (r + 1) * n)
                dst = pl.ds(grp * PERM_ROWS + r, n, stride=d)
                ln_ref[g, dst, :] = l_ref[src, :]
                for hh in range(nh):
                    on_ref[g, hh, dst, :] = o_ref[src, hh * LANES:(hh + 1) * LANES].astype(_F32)
    l0, l1, l2 = ln_ref[0], ln_ref[1], ln_ref[2]
    mx = jnp.maximum(jnp.maximum(l0, l1), l2)
    es = (jnp.exp2(l0 - mx), jnp.exp2(l1 - mx), jnp.exp2(l2 - mx))
    den = es[0] + es[1] + es[2]
    for g in range(3):
        wg = es[g] / den
        for hh in range(nh):
            lane0 = hh * (LANES // nh)
            col = g * gw + hh * LANES
            out_ref[:, col:col + LANES] = (on_ref[g, hh] * wg[:, lane0:lane0 + 1]).astype(out_ref.dtype)


def _combine_groups(outs, lses, dilations, *, tm):
    m, gw = outs[0].shape
    spec = pl.BlockSpec((tm, gw), lambda i: (i, 0))
    return pl.pallas_call(
        functools.partial(_combine_kernel, dilations=dilations),
        grid=(m // tm,),
        in_specs=[spec] * 3 + [pl.BlockSpec((tm, LANES), lambda i: (i, 0))] * 3,
        out_specs=pl.BlockSpec((tm, 3 * gw), lambda i: (i, 0)),
        out_shape=jax.ShapeDtypeStruct((m, 3 * gw), _BF16),
        scratch_shapes=[pltpu.VMEM((3, gw // LANES, tm, LANES), _F32), pltpu.VMEM((3, tm, LANES), _F32)],
        compiler_params=_cparams(("parallel",)),
        name="combine_groups",
    )(*outs, *lses)


def _mem_attn_kernel(q_ref, k_ref, v_ref, o_ref, *, scale):
    for hh in range(N_MEM_HEADS):
        sl = slice(hh * HEAD_DIM, (hh + 1) * HEAD_DIM)
        s = lax.dot_general(q_ref[:, sl], k_ref[:, sl], _NT, preferred_element_type=_F32) * scale
        m = jnp.max(s, axis=1, keepdims=True)
        p = jnp.exp(s - m)
        l = jnp.sum(p, axis=1, keepdims=True)
        o = jnp.dot(p.astype(_BF16), v_ref[:, sl], preferred_element_type=_F32)
        o_ref[:, sl] = (o / l).astype(o_ref.dtype)


def _memory_attention(proj, kvm, *, batch, seq, mem_tokens, q_col_block, tq):
    nq = seq // tq
    mw = N_MEM_HEADS * HEAD_DIM
    return pl.pallas_call(
        functools.partial(_mem_attn_kernel, scale=HEAD_DIM ** -0.5),
        grid=(batch, nq),
        in_specs=[
            pl.BlockSpec((tq, mw), lambda b, i: (b * nq + i, q_col_block)),
            pl.BlockSpec((mem_tokens, mw), lambda b, i: (b, 0)),
            pl.BlockSpec((mem_tokens, mw), lambda b, i: (b, 1)),
        ],
        out_specs=pl.BlockSpec((tq, mw), lambda b, i: (b * nq + i, 0)),
        out_shape=jax.ShapeDtypeStruct((batch * seq, mw), _BF16),
        compiler_params=_cparams(("parallel", "parallel")),
        name="memory_attention",
    )(proj, kvm, kvm)


def _out_proj_kernel(x_ref, a_ref, b_ref, wa_ref, wb_ref, o_ref):
    acc = jnp.dot(a_ref[...], wa_ref[...], preferred_element_type=_F32)
    acc = acc + jnp.dot(b_ref[...], wb_ref[...], preferred_element_type=_F32)
    o_ref[...] = x_ref[...] + acc


def _out_proj(x, o_mix, o_mem, w, layer, *, tm):
    m, d = x.shape
    ka, kb = o_mix.shape[1], o_mem.shape[1]
    assert ka % kb == 0
    return pl.pallas_call(
        _out_proj_kernel,
        grid=(m // tm,),
        in_specs=[
            pl.BlockSpec((tm, d), lambda i: (i, 0)),
            pl.BlockSpec((tm, ka), lambda i: (i, 0)),
            pl.BlockSpec((tm, kb), lambda i: (i, 0)),
            pl.BlockSpec((None, ka, d), lambda i: (layer, 0, 0)),
            pl.BlockSpec((None, kb, d), lambda i: (layer, ka // kb, 0)),
        ],
        out_specs=pl.BlockSpec((tm, d), lambda i: (i, 0)),
        out_shape=jax.ShapeDtypeStruct((m, d), _F32),
        compiler_params=_cparams(("parallel",)),
        name="out_proj",
    )(x, o_mix, o_mem, w, w)


def _mlp_kernel(x_ref, g_ref, w1_ref, w2_ref, gf_ref, o_ref, hn_ref, *, final_norm):
    f = pl.program_id(1)

    @pl.when(f == 0)
    def _():
        x = x_ref[...]
        hn_ref[...] = _rms(x, g_ref[...]).astype(hn_ref.dtype)
        o_ref[...] = x

    a = jnp.dot(hn_ref[...], w1_ref[...], preferred_element_type=_F32)
    a = jnp.square(jnp.maximum(a, 0.0)).astype(_BF16)
    o_ref[...] += jnp.dot(a, w2_ref[...], preferred_element_type=_F32)

    if final_norm:
        @pl.when(f == pl.num_programs(1) - 1)
        def _():
            o_ref[...] = _rms(o_ref[...], gf_ref[...])


def _mlp(x, g, w1, w2, layer, g_final, *, final_norm, tm, tf):
    m, d = x.shape
    ff = w1.shape[2]
    return pl.pallas_call(
        functools.partial(_mlp_kernel, final_norm=final_norm),
        grid=(m // tm, ff // tf),
        in_specs=[
            pl.BlockSpec((tm, d), lambda i, f: (i, 0)),
            pl.BlockSpec((1, d), lambda i, f: (0, 0)),
            pl.BlockSpec((None, d, tf), lambda i, f: (layer, 0, f)),
            pl.BlockSpec((None, tf, d), lambda i, f: (layer, f, 0)),
            pl.BlockSpec((1, d), lambda i, f: (0, 0)),
        ],
        out_specs=pl.BlockSpec((tm, d), lambda i, f: (i, 0)),
        out_shape=jax.ShapeDtypeStruct((m, d), _F32),
        scratch_shapes=[pltpu.VMEM((tm, d), _BF16)],
        compiler_params=_cparams(("parallel", "arbitrary")),
        name="mlp",
    )(x, g.reshape(1, d), w1, w2, g_final.reshape(1, d))


def kernel(x, mem, g_attn, w_in, w_out, lambda_qk, diff_subln_g, g_mem, w_mem_kv, g_mlp, w_mlp1, w_mlp2, g_final):
    batch, seq, d_model = x.shape
    depth = w_in.shape[0]
    mem_tokens = mem.shape[1]
    in_width = w_in.shape[2]
    mem_width = N_MEM_HEADS * HEAD_DIM
    mix_width = (in_width - mem_width) // 3
    n_heads = mix_width // HEAD_DIM
    slopes = _alibi_slopes(n_heads)
    dilations = tuple(d for _, d in DILATED_GROUPS)
    gw = HEADS_PER_GROUP * HEAD_DIM
    n_qkv_tiles = 3 * mix_width // gw

    xf = x.reshape(batch * seq, d_model)
    memf = mem.reshape(batch * mem_tokens, d_model)
    w_in0_b = w_in[:1].astype(_BF16)
    mix_tiles = mix_width // gw
    for i in range(depth):
        if i % N_MIXERS == 0:
            j = i // N_MIXERS
            lam_init = 0.8 - 0.6 * math.exp(-0.3 * i)
            q_scale = LOG2E * (HEAD_DIM // 2) ** -0.5
            w_nat, layer = (w_in0_b, 0) if i == 0 else (w_in_rest_b, i - 1)
            proj, qvt = _norm_matmul_dual(
                xf, g_attn[i], w_nat, layer, jnp.transpose(w_in[i]).astype(_BF16), tm=1024, tn=gw,
                nat_cols=[mix_tiles + t for t in range(mix_tiles)] + [3 * mix_tiles],
                t_rows=[(t, q_scale) for t in range(mix_tiles)] + [(2 * mix_tiles + t, 1.0) for t in range(mix_tiles)])
            casts = [w_mem_kv, w_out, w_mlp1, w_mlp2] + ([w_in[1:]] if depth > 1 else []) if i == 0 else []
            o_mix, casted = _diff_attention(qvt, proj, lambda_qk[j], diff_subln_g[j], slopes, casts, batch=batch,
                                            seq=seq, n_heads=n_heads, lam_init=lam_init, tq=256, tk=512, hb=6)
            if i == 0:
                w_kv_b, w_out_b, w1_b, w2_b = casted[:4]
                w_in_rest_b = casted[4] if depth > 1 else None
            qm_col_block = mix_tiles
        else:
            tile_slot = lambda t: jnp.where(t < n_qkv_tiles, t % len(dilations), 0)
            proj = _norm_matmul(xf, g_attn[i], w_in_rest_b, i - 1, tm=1024, tn=2 * gw, dilations=dilations,
                                tile_slot=tile_slot, slot_width=gw)
            outs, lses = [], []
            for g, (window, dilation) in enumerate(DILATED_GROUPS):
                sl = slice(g * HEADS_PER_GROUP, (g + 1) * HEADS_PER_GROUP)
                o, lse = _dilated_group(proj, slopes[sl], batch=batch, seq=seq, group=g, window=window,
                                        dilation=dilation, in_width=in_width, mix_width=mix_width)
                outs.append(o)
                lses.append(lse)
            o_mix = _combine_groups(outs, lses, dilations, tm=PERM_ROWS)
            qm_col_block = 3 * mix_tiles
        kvm = _norm_matmul(memf, g_mem, w_kv_b, i, tm=512, tn=512)
        o_mem = _memory_attention(proj, kvm, batch=batch, seq=seq, mem_tokens=mem_tokens,
                                  q_col_block=qm_col_block, tq=512)
        xf = _out_proj(xf, o_mix, o_mem, w_out_b, i, tm=512)
        xf = _mlp(xf, g_mlp[i], w1_b, w2_b, i, g_final, final_norm=(i == depth - 1), tm=1024, tf=512)
    return xf.reshape(batch, seq, d_model)
```

```python
import functools
import math

import numpy as np
import jax
import jax.numpy as jnp
from jax import lax
from jax.experimental import pallas as pl
from jax.experimental.pallas import tpu as pltpu

HEAD_DIM = 128
N_MEM_HEADS = 4
N_MIXERS = 2
DILATED_GROUPS = ((128, 1), (512, 4), (2048, 16))
HEADS_PER_GROUP = 4
NORM_EPS = 1e-6
NEG = -1e30
LANES = 128
VMEM_LIMIT = 56 * 1024 * 1024
PERM_ROWS = 512
LOG2E = math.log2(math.e)

_F32 = jnp.float32
_BF16 = jnp.bfloat16
_NT = (((1,), (1,)), ((), ()))


def _alibi_slopes(n):
    def pow2(m):
        start = 2.0 ** (-(2.0 ** -(math.log2(m) - 3)))
        return [start * start ** i for i in range(m)]

    def slopes(m):
        if math.log2(m).is_integer():
            return pow2(m)
        c = 2 ** math.floor(math.log2(m))
        return pow2(c) + slopes(2 * c)[0::2][: m - c]

    return np.asarray(sorted(slopes(n), reverse=True), dtype=np.float32)


def _cparams(semantics):
    return pltpu.CompilerParams(dimension_semantics=semantics, vmem_limit_bytes=VMEM_LIMIT)


def _rms(x, g):
    ms = jnp.mean(x * x, axis=-1, keepdims=True)
    return x * lax.rsqrt(ms + NORM_EPS) * g


def _norm_matmul_kernel(x_ref, g_ref, w_ref, o_ref, xn_ref, hn_ref, *, dilations, tile_slot, slot_width):
    j = pl.program_id(1)
    tm = x_ref.shape[0]
    parts = o_ref.shape[1] // slot_width

    def part(t, h):
        cols = slice(t * slot_width, (t + 1) * slot_width)
        o_ref[:, cols] = jnp.dot(h, w_ref[:, cols], preferred_element_type=_F32).astype(o_ref.dtype)

    def permute(s):
        d = dilations[s]
        n = PERM_ROWS // d
        for c in range(xn_ref.shape[0]):
            for grp in range(tm // PERM_ROWS):
                for r in range(d):
                    rows = xn_ref[c, pl.ds(grp * PERM_ROWS + r, n, stride=d), :]
                    dst = grp * PERM_ROWS + r * n
                    hn_ref[s, dst:dst + n, c * LANES:(c + 1) * LANES] = rows.astype(hn_ref.dtype)

    @pl.when(j == 0)
    def _():
        xn = _rms(x_ref[...], g_ref[...])
        h0 = xn.astype(hn_ref.dtype)
        hn_ref[0] = h0
        if len(dilations) > 1:
            for c in range(xn_ref.shape[0]):
                xn_ref[c] = xn[:, c * LANES:(c + 1) * LANES]
        done = {0}
        for t in range(parts):
            s = tile_slot[t]
            if s not in done:
                permute(s)
                done.add(s)
            part(t, h0 if s == 0 else hn_ref[s])
        for s in range(len(dilations)):
            if s not in done:
                permute(s)

    @pl.when(j > 0)
    def _():
        for t in range(parts):
            slot = _pick(j * parts + t, list(tile_slot)) if len(dilations) > 1 else 0
            part(t, hn_ref[slot])


def _norm_matmul(x, g, w, layer, *, tm, tn, dilations=(1,), tile_slot=None, slot_width=None):
    m, k = x.shape
    n_tiles = w.shape[2] // tn
    slot_width = tn if slot_width is None else slot_width
    tile_slot = (0,) * (n_tiles * tn // slot_width) if tile_slot is None else tuple(tile_slot)
    xn_shape = (k // LANES, tm, LANES) if len(dilations) > 1 else (1, 8, LANES)
    return pl.pallas_call(
        functools.partial(_norm_matmul_kernel, dilations=dilations, tile_slot=tile_slot, slot_width=slot_width),
        grid=(m // tm, n_tiles),
        in_specs=[
            pl.BlockSpec((tm, k), lambda i, j: (i, 0)),
            pl.BlockSpec((1, k), lambda i, j: (0, 0)),
            pl.BlockSpec((None, k, tn), lambda i, j: (layer, 0, j)),
        ],
        out_specs=pl.BlockSpec((tm, tn), lambda i, j: (i, j)),
        out_shape=jax.ShapeDtypeStruct((m, n_tiles * tn), _BF16),
        scratch_shapes=[pltpu.VMEM(xn_shape, _F32), pltpu.VMEM((len(dilations), tm, k), _BF16)],
        compiler_params=_cparams(("parallel", "arbitrary")),
        name="norm_matmul",
    )(x, g.reshape(1, k), w)


def _pick(j, values):
    out = values[-1]
    for idx in range(len(values) - 2, -1, -1):
        out = jnp.where(j == idx, values[idx], out)
    return out


def _norm_matmul_dual_kernel(x_ref, g_ref, wa_ref, wb_ref, ta_ref, tb_ref, nat_ref, tr_ref, hn_ref,
                             *, nat_steps, scales_a, scales_b):
    j = pl.program_id(1)
    tn = wa_ref.shape[1]

    def natural(h):
        nat_ref[:, :tn] = jnp.dot(h, wa_ref[...], preferred_element_type=_F32).astype(nat_ref.dtype)
        nat_ref[:, tn:] = jnp.dot(h, wb_ref[...], preferred_element_type=_F32).astype(nat_ref.dtype)

    @pl.when(j == 0)
    def _():
        h = _rms(x_ref[...], g_ref[...]).astype(hn_ref.dtype)
        hn_ref[...] = h
        natural(h)

    @pl.when((j > 0) & (j < nat_steps))
    def _():
        natural(hn_ref[...])

    @pl.when(j >= nat_steps)
    def _():
        acc = lax.dot_general(ta_ref[...], hn_ref[...], _NT, preferred_element_type=_F32)
        tr_ref[:tn, :] = (acc * _pick(j, scales_a)).astype(tr_ref.dtype)
        acc = lax.dot_general(tb_ref[...], hn_ref[...], _NT, preferred_element_type=_F32)
        tr_ref[tn:, :] = (acc * _pick(j, scales_b)).astype(tr_ref.dtype)


def _norm_matmul_dual(x, g, w, layer, wt, *, tm, tn, nat_cols, t_rows):
    m, k = x.shape
    assert len(nat_cols) % 2 == 0 and len(t_rows) % 2 == 0
    nat_steps, t_steps = len(nat_cols) // 2, len(t_rows) // 2
    ca = [nat_cols[2 * s] for s in range(nat_steps)] + [nat_cols[-2]] * t_steps
    cb = [nat_cols[2 * s + 1] for s in range(nat_steps)] + [nat_cols[-1]] * t_steps
    ra = [t_rows[0][0]] * nat_steps + [t_rows[2 * s][0] for s in range(t_steps)]
    rb = [t_rows[1][0]] * nat_steps + [t_rows[2 * s + 1][0] for s in range(t_steps)]
    scales_a = [1.0] * nat_steps + [float(t_rows[2 * s][1]) for s in range(t_steps)]
    scales_b = [1.0] * nat_steps + [float(t_rows[2 * s + 1][1]) for s in range(t_steps)]
    return pl.pallas_call(
        functools.partial(_norm_matmul_dual_kernel, nat_steps=nat_steps, scales_a=scales_a, scales_b=scales_b),
        grid=(m // tm, nat_steps + t_steps),
        in_specs=[
            pl.BlockSpec((tm, k), lambda i, j: (i, 0)),
            pl.BlockSpec((1, k), lambda i, j: (0, 0)),
            pl.BlockSpec((None, k, tn), lambda i, j: (layer, 0, _pick(j, ca))),
            pl.BlockSpec((None, k, tn), lambda i, j: (layer, 0, _pick(j, cb))),
            pl.BlockSpec((tn, k), lambda i, j: (_pick(j, ra), 0)),
            pl.BlockSpec((tn, k), lambda i, j: (_pick(j, rb), 0)),
        ],
        out_specs=[
            pl.BlockSpec((tm, 2 * tn), lambda i, j: (i, jnp.minimum(j, nat_steps - 1))),
            pl.BlockSpec((2 * tn, tm), lambda i, j: (jnp.maximum(j - nat_steps, 0), i)),
        ],
        out_shape=[
            jax.ShapeDtypeStruct((m, len(nat_cols) * tn), _BF16),
            jax.ShapeDtypeStruct((len(t_rows) * tn, m), _BF16),
        ],
        scratch_shapes=[pltpu.VMEM((tm, k), _BF16)],
        compiler_params=_cparams(("parallel", "arbitrary")),
        name="norm_matmul_dual",
    )(x, g.reshape(1, k), w, w, wt, wt)


def _diff_attn_kernel(*refs, n_cast, tq, tk, hb, ones_rows, lam_init):
    lam_ref, sig_ref, sigp_ref, g_ref, kc_ref, qt_ref, k_ref, vt_ref = refs[:8]
    cast_in = refs[8:8 + n_cast]
    o_ref = refs[8 + n_cast]
    cast_out = refs[9 + n_cast:9 + 2 * n_cast]
    m_ref, acc_ref = refs[9 + 2 * n_cast:]
    for src, dst in zip(cast_in, cast_out):
        dst[...] = src[...].astype(dst.dtype)
    qi = pl.program_id(2)
    hd = HEAD_DIM
    half = hd // 2
    reps = 2 * tq // LANES

    lp = lam_ref[...]
    lam = (jnp.exp(jnp.sum(lp[0:1] * lp[1:2], axis=1, keepdims=True))
           - jnp.exp(jnp.sum(lp[2:3] * lp[3:4], axis=1, keepdims=True)) + lam_init)

    row = lax.broadcasted_iota(jnp.int32, (hd, tq), 0)
    qzts, sigs = [], []
    for hh in range(hb):
        qt = qt_ref[hh * hd:(hh + 1) * hd, :]
        zero = jnp.zeros_like(qt)
        top = jnp.concatenate([jnp.where(row < half, qt, zero), jnp.where(row >= half, qt, zero)], axis=1)
        ext = jnp.concatenate([sigp_ref[hh]] * reps, axis=1).astype(_BF16)
        qzts.append(jnp.concatenate([top, ext], axis=0))
        sigs.append(sig_ref[hh][:, :1])
    key_i = lax.broadcasted_iota(jnp.int32, (tq, 2 * tq), 0)
    qry_i = lax.broadcasted_iota(jnp.int32, (tq, 2 * tq), 1)
    causal = key_i <= jnp.where(qry_i >= tq, qry_i - tq, qry_i)

    m_ref[...] = jnp.full(m_ref.shape, NEG, _F32)
    acc_ref[...] = jnp.zeros(acc_ref.shape, _F32)

    def step(start, size, masked):
        off = (start - qi * tq).astype(_F32)
        kc = kc_ref[:size, :]
        ones = jnp.ones((ones_rows, size), _BF16)
        ps = []
        for hh in range(hb):
            k = jnp.concatenate([k_ref[pl.ds(start, size), hh * hd:(hh + 1) * hd], kc], axis=1)
            c = sigs[hh] * off
            s = jnp.dot(k, qzts[hh], preferred_element_type=_F32)
            if masked:
                s = jnp.where(causal, s, NEG)
            m_prev = m_ref[hh]
            m_new = jnp.maximum(m_prev, jnp.max(s, axis=0, keepdims=True) + c)
            alpha = jnp.exp2(m_prev - m_new)
            p = jnp.exp2(s - (m_new - c))
            m_ref[hh] = m_new
            ps.append((p.astype(_BF16), alpha))
        for hh in range(hb):
            p, alpha = ps[hh]
            vt = jnp.concatenate([vt_ref[hh * hd:(hh + 1) * hd, pl.ds(start, size)], ones], axis=0)
            acc_ref[hh] = acc_ref[hh] * alpha + jnp.dot(vt, p, preferred_element_type=_F32)

    nb = (qi * tq) // tk

    def body(j, carry):
        step(pl.multiple_of(2 * j * tk, tk), tk, False)
        step(pl.multiple_of((2 * j + 1) * tk, tk), tk, False)
        return carry

    lax.fori_loop(0, nb // 2, body, 0)

    @pl.when(nb % 2 == 1)
    def _():
        step(pl.multiple_of((nb - 1) * tk, tk), tk, False)
    if tk != tq:
        @pl.when(qi % (tk // tq) == 1)
        def _():
            step(pl.multiple_of((qi - 1) * tq, tq), tq, False)
    step(pl.multiple_of(qi * tq, tq), tq, True)

    for hh in range(hb):
        ot = acc_ref[hh, :hd, :] / acc_ref[hh, hd:hd + 1, :]
        odt = ot[:, :tq] - lam * ot[:, tq:]
        ms = jnp.mean(odt * odt, axis=0, keepdims=True)
        yt = odt * lax.rsqrt(ms + NORM_EPS) * g_ref[...] * (1.0 - lam_init)
        o_ref[:, hh * hd:(hh + 1) * hd] = yt.T.astype(o_ref.dtype)


def _bf16_pieces(x):
    hi = x.astype(_BF16).astype(np.float32)
    mid = (x - hi).astype(_BF16).astype(np.float32)
    lo = (x - hi - mid).astype(_BF16).astype(np.float32)
    return hi, mid, lo


def _diff_attention(qvt, knat, lam_params, subln_g, slopes, casts, *, batch, seq, n_heads, lam_init, tq, tk, hb):
    assert tk in (tq, 2 * tq)
    nq = seq // tq
    hd = HEAD_DIM
    ng = n_heads // hb
    sig = (slopes.astype(np.float64) * LOG2E).astype(np.float32)
    sig_arr = np.broadcast_to(sig.reshape(n_heads, 1, 1), (n_heads, 1, hd))
    sigp_arr = np.zeros((n_heads, hd, LANES), np.float32)
    for idx, piece in enumerate(_bf16_pieces(sig)):
        sigp_arr[:, idx, :] = piece[:, None]
        sigp_arr[:, 3 + idx, :] = piece[:, None]
    key = np.arange(tk)
    kc_arr = np.zeros((tk, LANES), np.float32)
    kc_arr[:, 0:3] = (key & 255)[:, None]
    kc_arr[:, 3:6] = (key - (key & 255))[:, None]
    ones_rows = 16
    steps = batch * ng * nq
    cast_2d = [c.reshape(-1, c.shape[-1]) for c in casts]
    assert all(c.shape[0] % (16 * steps) == 0 for c in cast_2d), "cast slabs must be whole bf16 sublane tiles"
    cast_specs = [pl.BlockSpec((c.shape[0] // steps, c.shape[1]), lambda b, h, i: ((b * ng + h) * nq + i, 0))
                  for c in cast_2d]
    kern = functools.partial(_diff_attn_kernel, n_cast=len(casts), tq=tq, tk=tk, hb=hb, ones_rows=ones_rows,
                             lam_init=lam_init)
    outs = pl.pallas_call(
        kern,
        grid=(batch, ng, nq),
        in_specs=[
            pl.BlockSpec(lam_params.shape, lambda b, h, i: (0, 0)),
            pl.BlockSpec((hb, 1, hd), lambda b, h, i: (h, 0, 0)),
            pl.BlockSpec((hb, hd, LANES), lambda b, h, i: (h, 0, 0)),
            pl.BlockSpec((hd, 1), lambda b, h, i: (0, 0)),
            pl.BlockSpec((tk, LANES), lambda b, h, i: (0, 0)),
            pl.BlockSpec((hb * hd, tq), lambda b, h, i: (h, b * nq + i)),
            pl.BlockSpec((seq, hb * hd), lambda b, h, i: (b, h)),
            pl.BlockSpec((hb * hd, seq), lambda b, h, i: (ng + h, b)),
        ] + cast_specs,
        out_specs=[pl.BlockSpec((tq, hb * hd), lambda b, h, i: (b * nq + i, h))] + cast_specs,
        out_shape=[jax.ShapeDtypeStruct((batch * seq, n_heads * hd), _BF16)]
        + [jax.ShapeDtypeStruct(c.shape, _BF16) for c in cast_2d],
        scratch_shapes=[
            pltpu.VMEM((hb, 1, 2 * tq), _F32),
            pltpu.VMEM((hb, hd + ones_rows, 2 * tq), _F32),
        ],
        compiler_params=_cparams(("parallel", "parallel", "arbitrary")),
        name="diff_attention",
    )(lam_params, jnp.asarray(sig_arr), jnp.asarray(sigp_arr), subln_g.reshape(hd, 1),
      jnp.asarray(kc_arr, dtype=_BF16), qvt, knat, qvt, *cast_2d)
    return outs[0], [o.reshape(c.shape) for o, c in zip(outs[1:], casts)]


def _rows(ref, n, st, start, size, sl):
    if len(ref.shape) == 2:
        return ref[start:start + size, sl]
    return ref[start // n:(start + size) // n, st * n:(st + 1) * n, sl].reshape(size, sl.stop - sl.start)


def _store_rows(ref, n, st, start, size, sl, val):
    if len(ref.shape) == 2:
        ref[start:start + size, sl] = val
    else:
        ref[start // n:(start + size) // n, st * n:(st + 1) * n, sl] = val.reshape(size // n, n, sl.stop - sl.start)


def _dilated_kernel(*refs, tq, span, n, streams, slopes2, scale2, has_prev):
    if has_prev:
        q_ref, kp_ref, kc_ref, vp_ref, vc_ref, o_ref, lse_ref = refs
    else:
        q_ref, kc_ref, vc_ref, o_ref, lse_ref = refs
    length = q_ref.shape[0] if len(q_ref.shape) == 2 else q_ref.shape[0] * n
    first = pl.program_id(2) == 0
    nh = len(slopes2)
    lane_grp = lax.broadcasted_iota(jnp.int32, (tq, LANES), 1) // (LANES // nh)

    def masked_bias(nk, shift, first_block):
        row = lax.broadcasted_iota(jnp.int32, (tq, nk), 0)
        col = lax.broadcasted_iota(jnp.int32, (tq, nk), 1)
        dist = row - col + shift
        valid = (dist >= 0) & (dist <= span)
        if first_block and has_prev:
            valid = valid & ((col >= span) | jnp.logical_not(first))
        distf = dist.astype(_F32)
        return [jnp.where(valid, -s2 * distf, NEG) for s2 in slopes2]

    bias_head = masked_bias(tq + span if has_prev else tq, span if has_prev else 0, True)
    bias_body = masked_bias(tq + span, span, False) if length > tq else None

    for st in range(streams):
        for qb in range(length // tq):
            u0 = qb * tq
            lses = []
            for hh in range(nh):
                sl = slice(hh * HEAD_DIM, (hh + 1) * HEAD_DIM)
                q = _rows(q_ref, n, st, u0, tq, sl)
                if qb == 0 and has_prev:
                    k = jnp.concatenate([kp_ref[:, sl], _rows(kc_ref, n, st, 0, tq, sl)], axis=0)
                    v = jnp.concatenate([vp_ref[:, sl], _rows(vc_ref, n, st, 0, tq, sl)], axis=0)
                elif qb == 0:
                    k, v = _rows(kc_ref, n, st, 0, tq, sl), _rows(vc_ref, n, st, 0, tq, sl)
                else:
                    k = _rows(kc_ref, n, st, u0 - span, tq + span, sl)
                    v = _rows(vc_ref, n, st, u0 - span, tq + span, sl)
                bias = (bias_head if qb == 0 else bias_body)[hh]
                s = lax.dot_general(q, k, _NT, preferred_element_type=_F32) * scale2 + bias
                m = jnp.max(s, axis=1, keepdims=True)
                p = jnp.exp2(s - m)
                l = jnp.sum(p, axis=1, keepdims=True)
                o = jnp.dot(p.astype(_BF16), v, preferred_element_type=_F32) / l
                _store_rows(o_ref, n, st, u0, tq, sl, o.astype(o_ref.dtype))
                lses.append(m + jnp.log2(l))
            packed = jnp.broadcast_to(lses[nh - 1], (tq, LANES))
            for hh in range(nh - 2, -1, -1):
                packed = jnp.where(lane_grp == hh, lses[hh], packed)
            _store_rows(lse_ref, n, st, u0, tq, slice(0, LANES), packed)


def _dilated_group(proj, slopes_g, *, batch, seq, group, window, dilation, in_width, mix_width):
    span = window // dilation
    assert span == LANES, "key window per stream must be one 128-row block"
    gw = HEADS_PER_GROUP * HEAD_DIM
    koff = mix_width // gw
    rows = batch * seq
    n = PERM_ROWS // dilation
    tq = span if n >= span else 2 * span
    streams = max(1, LANES // n) if dilation > 1 else 1
    kern = functools.partial(
        _dilated_kernel, tq=tq, span=span, n=n, streams=streams,
        slopes2=tuple(float(s) * dilation * LOG2E for s in slopes_g), scale2=HEAD_DIM ** -0.5 * LOG2E,
        has_prev=dilation == 1)
    if dilation == 1:
        chunk = 1024
        nc = seq // chunk
        cur = lambda c: pl.BlockSpec((chunk, gw), lambda b, r, u: (b * nc + u, c))
        prev = lambda c: pl.BlockSpec(
            (span, gw), lambda b, r, u: (b * (seq // span) + jnp.maximum(u * (chunk // span) - 1, 0), c))
        in_specs = [cur(group), prev(koff + group), cur(koff + group), prev(2 * koff + group), cur(2 * koff + group)]
        operands = [proj] * 5
        out_specs = [pl.BlockSpec((chunk, w), lambda b, r, u: (b * nc + u, 0)) for w in (gw, LANES)]
        out_dims = [(rows, gw), (rows, LANES)]
        grid = (batch, 1, nc)
    else:
        tiles = seq // PERM_ROWS
        proj3 = proj.reshape(rows // PERM_ROWS, PERM_ROWS, in_width)
        blk = lambda c, w: pl.BlockSpec((tiles, streams * n, w), lambda b, r, u: (b, r, c))
        in_specs = [blk(group, gw), blk(koff + group, gw), blk(2 * koff + group, gw)]
        operands = [proj3] * 3
        out_specs = [blk(0, gw), blk(0, LANES)]
        out_dims = [(rows // PERM_ROWS, PERM_ROWS, gw), (rows // PERM_ROWS, PERM_ROWS, LANES)]
        grid = (batch, dilation // streams, 1)
    o, lse = pl.pallas_call(
        kern,
        grid=grid,
        in_specs=in_specs,
        out_specs=out_specs,
        out_shape=[jax.ShapeDtypeStruct(out_dims[0], _BF16), jax.ShapeDtypeStruct(out_dims[1], _F32)],
        compiler_params=_cparams(("parallel", "parallel", "arbitrary")),
        name=f"dilated_attention_d{dilation}",
    )(*operands)
    return o.reshape(rows, gw), lse.reshape(rows, LANES)


def _combine_kernel(o0_ref, o1_ref, o2_ref, l0_ref, l1_ref, l2_ref, out_ref, on_ref, ln_ref, *, dilations):
    tm = out_ref.shape[0]
    gw = o0_ref.shape[1]
    nh = gw // LANES
    for g, (o_ref, l_ref, d) in enumerate(zip((o0_ref, o1_ref, o2_ref), (l0_ref, l1_ref, l2_ref), dilations)):
        n = PERM_ROWS // d
        if d == 1:
            ln_ref[g] = l_ref[...]
            for hh in range(nh):
                on_ref[g, hh] = o_ref[:, hh * LANES:(hh + 1) * LANES].astype(_F32)
            continue
        for grp in range(tm // PERM_ROWS):
            for r in range(d):
                src = slice(grp * PERM_ROWS + r * n, grp * PERM_ROWS + (r + 1) * n)
                dst = pl.ds(grp * PERM_ROWS + r, n, stride=d)
                ln_ref[g, dst, :] = l_ref[src, :]
                for hh in range(nh):
                    on_ref[g, hh, dst, :] = o_ref[src, hh * LANES:(hh + 1) * LANES].astype(_F32)
    l0, l1, l2 = ln_ref[0], ln_ref[1], ln_ref[2]
    mx = jnp.maximum(jnp.maximum(l0, l1), l2)
    es = (jnp.exp2(l0 - mx), jnp.exp2(l1 - mx), jnp.exp2(l2 - mx))
    den = es[0] + es[1] + es[2]
    for g in range(3):
        wg = es[g] / den
        for hh in range(nh):
            lane0 = hh * (LANES // nh)
            col = g * gw + hh * LANES
            out_ref[:, col:col + LANES] = (on_ref[g, hh] * wg[:, lane0:lane0 + 1]).astype(out_ref.dtype)


def _combine_groups(outs, lses, dilations, *, tm):
    m, gw = outs[0].shape
    spec = pl.BlockSpec((tm, gw), lambda i: (i, 0))
    return pl.pallas_call(
        functools.partial(_combine_kernel, dilations=dilations),
        grid=(m // tm,),
        in_specs=[spec] * 3 + [pl.BlockSpec((tm, LANES), lambda i: (i, 0))] * 3,
        out_specs=pl.BlockSpec((tm, 3 * gw), lambda i: (i, 0)),
        out_shape=jax.ShapeDtypeStruct((m, 3 * gw), _BF16),
        scratch_shapes=[pltpu.VMEM((3, gw // LANES, tm, LANES), _F32), pltpu.VMEM((3, tm, LANES), _F32)],
        compiler_params=_cparams(("parallel",)),
        name="combine_groups",
    )(*outs, *lses)


def _mem_attn_kernel(q_ref, k_ref, v_ref, o_ref, *, scale):
    for hh in range(N_MEM_HEADS):
        sl = slice(hh * HEAD_DIM, (hh + 1) * HEAD_DIM)
        s = lax.dot_general(q_ref[:, sl], k_ref[:, sl], _NT, preferred_element_type=_F32) * scale
        m = jnp.max(s, axis=1, keepdims=True)
        p = jnp.exp(s - m)
        l = jnp.sum(p, axis=1, keepdims=True)
        o = jnp.dot(p.astype(_BF16), v_ref[:, sl], preferred_element_type=_F32)
        o_ref[:, sl] = (o / l).astype(o_ref.dtype)


def _memory_attention(proj, kvm, *, batch, seq, mem_tokens, q_col_block, tq):
    nq = seq // tq
    mw = N_MEM_HEADS * HEAD_DIM
    return pl.pallas_call(
        functools.partial(_mem_attn_kernel, scale=HEAD_DIM ** -0.5),
        grid=(batch, nq),
        in_specs=[
            pl.BlockSpec((tq, mw), lambda b, i: (b * nq + i, q_col_block)),
            pl.BlockSpec((mem_tokens, mw), lambda b, i: (b, 0)),
            pl.BlockSpec((mem_tokens, mw), lambda b, i: (b, 1)),
        ],
        out_specs=pl.BlockSpec((tq, mw), lambda b, i: (b * nq + i, 0)),
        out_shape=jax.ShapeDtypeStruct((batch * seq, mw), _BF16),
        compiler_params=_cparams(("parallel", "parallel")),
        name="memory_attention",
    )(proj, kvm, kvm)


def _out_proj_kernel(x_ref, a_ref, b_ref, wa_ref, wb_ref, o_ref):
    acc = jnp.dot(a_ref[...], wa_ref[...], preferred_element_type=_F32)
    acc = acc + jnp.dot(b_ref[...], wb_ref[...], preferred_element_type=_F32)
    o_ref[...] = x_ref[...] + acc


def _out_proj(x, o_mix, o_mem, w, layer, *, tm):
    m, d = x.shape
    ka, kb = o_mix.shape[1], o_mem.shape[1]
    assert ka % kb == 0
    return pl.pallas_call(
        _out_proj_kernel,
        grid=(m // tm,),
        in_specs=[
            pl.BlockSpec((tm, d), lambda i: (i, 0)),
            pl.BlockSpec((tm, ka), lambda i: (i, 0)),
            pl.BlockSpec((tm, kb), lambda i: (i, 0)),
            pl.BlockSpec((None, ka, d), lambda i: (layer, 0, 0)),
            pl.BlockSpec((None, kb, d), lambda i: (layer, ka // kb, 0)),
        ],
        out_specs=pl.BlockSpec((tm, d), lambda i: (i, 0)),
        out_shape=jax.ShapeDtypeStruct((m, d), _F32),
        compiler_params=_cparams(("parallel",)),
        name="out_proj",
    )(x, o_mix, o_mem, w, w)


def _mlp_kernel(x_ref, g_ref, w1_ref, w2_ref, gf_ref, o_ref, hn_ref, *, final_norm):
    f = pl.program_id(1)

    def ffn(h):
        a = jnp.dot(h, w1_ref[...], preferred_element_type=_F32)
        a = jnp.square(jnp.maximum(a, 0.0)).astype(_BF16)
        return jnp.dot(a, w2_ref[...], preferred_element_type=_F32)

    @pl.when(f == 0)
    def _():
        x = x_ref[...]
        h = _rms(x, g_ref[...]).astype(hn_ref.dtype)
        hn_ref[...] = h
        o_ref[...] = x + ffn(h)

    @pl.when(f > 0)
    def _():
        o_ref[...] += ffn(hn_ref[...])

    if final_norm:
        @pl.when(f == pl.num_programs(1) - 1)
        def _():
            o_ref[...] = _rms(o_ref[...], gf_ref[...])


def _mlp(x, g, w1, w2, layer, g_final, *, final_norm, tm, tf):
    m, d = x.shape
    ff = w1.shape[2]
    return pl.pallas_call(
        functools.partial(_mlp_kernel, final_norm=final_norm),
        grid=(m // tm, ff // tf),
        in_specs=[
            pl.BlockSpec((tm, d), lambda i, f: (i, 0)),
            pl.BlockSpec((1, d), lambda i, f: (0, 0)),
            pl.BlockSpec((None, d, tf), lambda i, f: (layer, 0, f)),
            pl.BlockSpec((None, tf, d), lambda i, f: (layer, f, 0)),
            pl.BlockSpec((1, d), lambda i, f: (0, 0)),
        ],
        out_specs=pl.BlockSpec((tm, d), lambda i, f: (i, 0)),
        out_shape=jax.ShapeDtypeStruct((m, d), _F32),
        scratch_shapes=[pltpu.VMEM((tm, d), _BF16)],
        compiler_params=_cparams(("parallel", "arbitrary")),
        name="mlp",
    )(x, g.reshape(1, d), w1, w2, g_final.reshape(1, d))


def kernel(x, mem, g_attn, w_in, w_out, lambda_qk, diff_subln_g, g_mem, w_mem_kv, g_mlp, w_mlp1, w_mlp2, g_final):
    batch, seq, d_model = x.shape
    depth = w_in.shape[0]
    mem_tokens = mem.shape[1]
    in_width = w_in.shape[2]
    mem_width = N_MEM_HEADS * HEAD_DIM
    mix_width = (in_width - mem_width) // 3
    n_heads = mix_width // HEAD_DIM
    slopes = _alibi_slopes(n_heads)
    dilations = tuple(d for _, d in DILATED_GROUPS)
    gw = HEADS_PER_GROUP * HEAD_DIM
    n_qkv_tiles = 3 * mix_width // gw

    xf = x.reshape(batch * seq, d_model)
    memf = mem.reshape(batch * mem_tokens, d_model)
    mix_tiles = mix_width // gw
    q_scale = LOG2E * (HEAD_DIM // 2) ** -0.5
    for i in range(depth):
        if i % N_MIXERS == 0:
            j = i // N_MIXERS
            lam_init = 0.8 - 0.6 * math.exp(-0.3 * i)
            wq, wk, wv, wqm = (w_in[i, :, a:b] for a, b in ((0, mix_width), (mix_width, 2 * mix_width),
                                                           (2 * mix_width, 3 * mix_width), (3 * mix_width, in_width)))
            if i == 0:
                w_nat = jnp.concatenate([wk, wqm], axis=1).astype(_BF16)[None]
                nat_cols = list(range(mix_tiles + 1))
                layer = 0
            else:
                w_nat, layer = w_in_b, i
                nat_cols = [mix_tiles + t for t in range(mix_tiles)] + [3 * mix_tiles]
            proj, qvt = _norm_matmul_dual(
                xf, g_attn[i], w_nat, layer, jnp.transpose(jnp.concatenate([wq, wv], axis=1)).astype(_BF16),
                tm=1024, tn=gw, nat_cols=nat_cols,
                t_rows=[(t, q_scale) for t in range(mix_tiles)] + [(mix_tiles + t, 1.0) for t in range(mix_tiles)])
            casts = [w_mem_kv, w_out, w_mlp1, w_mlp2, w_in] if i == 0 else []
            o_mix, casted = _diff_attention(qvt, proj, lambda_qk[j], diff_subln_g[j], slopes, casts, batch=batch,
                                            seq=seq, n_heads=n_heads, lam_init=lam_init, tq=256, tk=512, hb=6)
            if i == 0:
                w_kv_b, w_out_b, w1_b, w2_b, w_in_b = casted
            qm_col_block = mix_tiles
        else:
            tile_slot = [t % len(dilations) if t < n_qkv_tiles else 0 for t in range(in_width // gw)]
            proj = _norm_matmul(xf, g_attn[i], w_in_b, i, tm=1024, tn=2 * gw, dilations=dilations,
                                tile_slot=tile_slot, slot_width=gw)
            outs, lses = [], []
            for g, (window, dilation) in enumerate(DILATED_GROUPS):
                sl = slice(g * HEADS_PER_GROUP, (g + 1) * HEADS_PER_GROUP)
                o, lse = _dilated_group(proj, slopes[sl], batch=batch, seq=seq, group=g, window=window,
                                        dilation=dilation, in_width=in_width, mix_width=mix_width)
                outs.append(o)
                lses.append(lse)
            o_mix = _combine_groups(outs, lses, dilations, tm=PERM_ROWS)
            qm_col_block = 3 * mix_tiles
        kvm = _norm_matmul(memf, g_mem, w_kv_b, i, tm=512, tn=512)
        o_mem = _memory_attention(proj, kvm, batch=batch, seq=seq, mem_tokens=mem_tokens,
                                  q_col_block=qm_col_block, tq=512)
        xf = _out_proj(xf, o_mix, o_mem, w_out_b, i, tm=512)
        xf = _mlp(xf, g_mlp[i], w1_b, w2_b, i, g_final, final_norm=(i == depth - 1), tm=1024, tf=512)
    return xf.reshape(batch, seq, d_model)
```

```python
import functools
import math

import numpy as np
import jax
import jax.numpy as jnp
from jax import lax
from jax.experimental import pallas as pl
from jax.experimental.pallas import tpu as pltpu

HEAD_DIM = 128
N_MEM_HEADS = 4
N_MIXERS = 2
DILATED_GROUPS = ((128, 1), (512, 4), (2048, 16))
HEADS_PER_GROUP = 4
NORM_EPS = 1e-6
NEG = -1e30
LANES = 128
VMEM_LIMIT = 56 * 1024 * 1024
PERM_ROWS = 512
LOG2E = math.log2(math.e)

_F32 = jnp.float32
_BF16 = jnp.bfloat16
_NT = (((1,), (1,)), ((), ()))


def _alibi_slopes(n):
    def pow2(m):
        start = 2.0 ** (-(2.0 ** -(math.log2(m) - 3)))
        return [start * start ** i for i in range(m)]

    def slopes(m):
        if math.log2(m).is_integer():
            return pow2(m)
        c = 2 ** math.floor(math.log2(m))
        return pow2(c) + slopes(2 * c)[0::2][: m - c]

    return np.asarray(sorted(slopes(n), reverse=True), dtype=np.float32)


def _cparams(semantics):
    return pltpu.CompilerParams(dimension_semantics=semantics, vmem_limit_bytes=VMEM_LIMIT)


def _rms(x, g):
    ms = jnp.mean(x * x, axis=-1, keepdims=True)
    return x * lax.rsqrt(ms + NORM_EPS) * g


def _norm_matmul_kernel(x_ref, g_ref, w_ref, o_ref, xn_ref, hn_ref, *, dilations, tile_slot, slot_width):
    j = pl.program_id(1)
    tm = x_ref.shape[0]
    parts = o_ref.shape[1] // slot_width

    def part(t, h):
        cols = slice(t * slot_width, (t + 1) * slot_width)
        o_ref[:, cols] = jnp.dot(h, w_ref[:, cols], preferred_element_type=_F32).astype(o_ref.dtype)

    def permute(s):
        d = dilations[s]
        n = PERM_ROWS // d
        for c in range(xn_ref.shape[0]):
            for grp in range(tm // PERM_ROWS):
                for r in range(d):
                    rows = xn_ref[c, pl.ds(grp * PERM_ROWS + r, n, stride=d), :]
                    dst = grp * PERM_ROWS + r * n
                    hn_ref[s, dst:dst + n, c * LANES:(c + 1) * LANES] = rows.astype(hn_ref.dtype)

    @pl.when(j == 0)
    def _():
        xn = _rms(x_ref[...], g_ref[...])
        h0 = xn.astype(hn_ref.dtype)
        hn_ref[0] = h0
        if len(dilations) > 1:
            for c in range(xn_ref.shape[0]):
                xn_ref[c] = xn[:, c * LANES:(c + 1) * LANES]
        done = {0}
        for t in range(parts):
            s = tile_slot[t]
            if s not in done:
                permute(s)
                done.add(s)
            part(t, h0 if s == 0 else hn_ref[s])
        for s in range(len(dilations)):
            if s not in done:
                permute(s)

    @pl.when(j > 0)
    def _():
        for t in range(parts):
            slot = _pick(j * parts + t, list(tile_slot)) if len(dilations) > 1 else 0
            part(t, hn_ref[slot])


def _norm_matmul(x, g, w, layer, *, tm, tn, dilations=(1,), tile_slot=None, slot_width=None):
    m, k = x.shape
    n_tiles = w.shape[2] // tn
    slot_width = tn if slot_width is None else slot_width
    tile_slot = (0,) * (n_tiles * tn // slot_width) if tile_slot is None else tuple(tile_slot)
    xn_shape = (k // LANES, tm, LANES) if len(dilations) > 1 else (1, 8, LANES)
    return pl.pallas_call(
        functools.partial(_norm_matmul_kernel, dilations=dilations, tile_slot=tile_slot, slot_width=slot_width),
        grid=(m // tm, n_tiles),
        in_specs=[
            pl.BlockSpec((tm, k), lambda i, j: (i, 0)),
            pl.BlockSpec((1, k), lambda i, j: (0, 0)),
            pl.BlockSpec((None, k, tn), lambda i, j: (layer, 0, j)),
        ],
        out_specs=pl.BlockSpec((tm, tn), lambda i, j: (i, j)),
        out_shape=jax.ShapeDtypeStruct((m, n_tiles * tn), _BF16),
        scratch_shapes=[pltpu.VMEM(xn_shape, _F32), pltpu.VMEM((len(dilations), tm, k), _BF16)],
        compiler_params=_cparams(("parallel", "arbitrary")),
        name="norm_matmul",
    )(x, g.reshape(1, k), w)


def _cast_tiles_kernel(w_ref, o_ref, *, transpose):
    w = w_ref[...]
    o_ref[...] = (w.T if transpose else w).astype(o_ref.dtype)


def _cast_tiles(w, layer, col_tiles, *, tn, transpose):
    k = w.shape[1]
    n = len(col_tiles)
    if transpose:
        out_spec, out_dims = pl.BlockSpec((tn, k), lambda j: (j, 0)), (n * tn, k)
    else:
        out_spec, out_dims = pl.BlockSpec((k, tn), lambda j: (0, j)), (k, n * tn)
    return pl.pallas_call(
        functools.partial(_cast_tiles_kernel, transpose=transpose),
        grid=(n,),
        in_specs=[pl.BlockSpec((None, k, tn), lambda j: (layer, 0, _pick(j, list(col_tiles))))],
        out_specs=out_spec,
        out_shape=jax.ShapeDtypeStruct(out_dims, _BF16),
        compiler_params=_cparams(("parallel",)),
        name="cast_tiles_t" if transpose else "cast_tiles",
    )(w)


def _pick(j, values):
    out = values[-1]
    for idx in range(len(values) - 2, -1, -1):
        out = jnp.where(j == idx, values[idx], out)
    return out


def _norm_matmul_dual_kernel(x_ref, g_ref, wa_ref, wb_ref, ta_ref, tb_ref, nat_ref, tr_ref, hn_ref,
                             *, nat_steps, scales_a, scales_b):
    j = pl.program_id(1)
    tn = wa_ref.shape[1]

    def natural(h):
        nat_ref[:, :tn] = jnp.dot(h, wa_ref[...], preferred_element_type=_F32).astype(nat_ref.dtype)
        nat_ref[:, tn:] = jnp.dot(h, wb_ref[...], preferred_element_type=_F32).astype(nat_ref.dtype)

    @pl.when(j == 0)
    def _():
        h = _rms(x_ref[...], g_ref[...]).astype(hn_ref.dtype)
        hn_ref[...] = h
        natural(h)

    @pl.when((j > 0) & (j < nat_steps))
    def _():
        natural(hn_ref[...])

    @pl.when(j >= nat_steps)
    def _():
        acc = lax.dot_general(ta_ref[...], hn_ref[...], _NT, preferred_element_type=_F32)
        tr_ref[:tn, :] = (acc * _pick(j, scales_a)).astype(tr_ref.dtype)
        acc = lax.dot_general(tb_ref[...], hn_ref[...], _NT, preferred_element_type=_F32)
        tr_ref[tn:, :] = (acc * _pick(j, scales_b)).astype(tr_ref.dtype)


def _norm_matmul_dual(x, g, w, layer, wt, *, tm, tn, nat_cols, t_rows):
    m, k = x.shape
    assert len(nat_cols) % 2 == 0 and len(t_rows) % 2 == 0
    nat_steps, t_steps = len(nat_cols) // 2, len(t_rows) // 2
    ca = [nat_cols[2 * s] for s in range(nat_steps)] + [nat_cols[-2]] * t_steps
    cb = [nat_cols[2 * s + 1] for s in range(nat_steps)] + [nat_cols[-1]] * t_steps
    ra = [t_rows[0][0]] * nat_steps + [t_rows[2 * s][0] for s in range(t_steps)]
    rb = [t_rows[1][0]] * nat_steps + [t_rows[2 * s + 1][0] for s in range(t_steps)]
    scales_a = [1.0] * nat_steps + [float(t_rows[2 * s][1]) for s in range(t_steps)]
    scales_b = [1.0] * nat_steps + [float(t_rows[2 * s + 1][1]) for s in range(t_steps)]
    return pl.pallas_call(
        functools.partial(_norm_matmul_dual_kernel, nat_steps=nat_steps, scales_a=scales_a, scales_b=scales_b),
        grid=(m // tm, nat_steps + t_steps),
        in_specs=[
            pl.BlockSpec((tm, k), lambda i, j: (i, 0)),
            pl.BlockSpec((1, k), lambda i, j: (0, 0)),
            pl.BlockSpec((None, k, tn), lambda i, j: (layer, 0, _pick(j, ca))),
            pl.BlockSpec((None, k, tn), lambda i, j: (layer, 0, _pick(j, cb))),
            pl.BlockSpec((tn, k), lambda i, j: (_pick(j, ra), 0)),
            pl.BlockSpec((tn, k), lambda i, j: (_pick(j, rb), 0)),
        ],
        out_specs=[
            pl.BlockSpec((tm, 2 * tn), lambda i, j: (i, jnp.minimum(j, nat_steps - 1))),
            pl.BlockSpec((2 * tn, tm), lambda i, j: (jnp.maximum(j - nat_steps, 0), i)),
        ],
        out_shape=[
            jax.ShapeDtypeStruct((m, len(nat_cols) * tn), _BF16),
            jax.ShapeDtypeStruct((len(t_rows) * tn, m), _BF16),
        ],
        scratch_shapes=[pltpu.VMEM((tm, k), _BF16)],
        compiler_params=_cparams(("parallel", "arbitrary")),
        name="norm_matmul_dual",
    )(x, g.reshape(1, k), w, w, wt, wt)


def _diff_attn_kernel(*refs, n_cast, tq, tk, hb, ones_rows, lam_init):
    lam_ref, sig_ref, sigp_ref, g_ref, kc_ref, qt_ref, k_ref, vt_ref = refs[:8]
    cast_in = refs[8:8 + n_cast]
    o_ref = refs[8 + n_cast]
    cast_out = refs[9 + n_cast:9 + 2 * n_cast]
    m_ref, acc_ref = refs[9 + 2 * n_cast:]
    for src, dst in zip(cast_in, cast_out):
        dst[...] = src[...].astype(dst.dtype)
    qi = pl.program_id(2)
    hd = HEAD_DIM
    half = hd // 2
    reps = 2 * tq // LANES

    lp = lam_ref[...]
    lam = (jnp.exp(jnp.sum(lp[0:1] * lp[1:2], axis=1, keepdims=True))
           - jnp.exp(jnp.sum(lp[2:3] * lp[3:4], axis=1, keepdims=True)) + lam_init)

    row = lax.broadcasted_iota(jnp.int32, (hd, tq), 0)
    qzts, sigs = [], []
    for hh in range(hb):
        qt = qt_ref[hh * hd:(hh + 1) * hd, :]
        zero = jnp.zeros_like(qt)
        top = jnp.concatenate([jnp.where(row < half, qt, zero), jnp.where(row >= half, qt, zero)], axis=1)
        ext = jnp.concatenate([sigp_ref[hh]] * reps, axis=1).astype(_BF16)
        qzts.append(jnp.concatenate([top, ext], axis=0))
        sigs.append(sig_ref[hh][:, :1])
    key_i = lax.broadcasted_iota(jnp.int32, (tq, 2 * tq), 0)
    qry_i = lax.broadcasted_iota(jnp.int32, (tq, 2 * tq), 1)
    causal = key_i <= jnp.where(qry_i >= tq, qry_i - tq, qry_i)

    m_ref[...] = jnp.full(m_ref.shape, NEG, _F32)
    acc_ref[...] = jnp.zeros(acc_ref.shape, _F32)

    def step(start, size, masked):
        off = (start - qi * tq).astype(_F32)
        kc = kc_ref[:size, :]
        ones = jnp.ones((ones_rows, size), _BF16)
        ps = []
        for hh in range(hb):
            k = jnp.concatenate([k_ref[pl.ds(start, size), hh * hd:(hh + 1) * hd], kc], axis=1)
            c = sigs[hh] * off
            s = jnp.dot(k, qzts[hh], preferred_element_type=_F32)
            if masked:
                s = jnp.where(causal, s, NEG)
            m_prev = m_ref[hh]
            m_new = jnp.maximum(m_prev, jnp.max(s, axis=0, keepdims=True) + c)
            alpha = jnp.exp2(m_prev - m_new)
            p = jnp.exp2(s - (m_new - c))
            m_ref[hh] = m_new
            ps.append((p.astype(_BF16), alpha))
        for hh in range(hb):
            p, alpha = ps[hh]
            vt = jnp.concatenate([vt_ref[hh * hd:(hh + 1) * hd, pl.ds(start, size)], ones], axis=0)
            acc_ref[hh] = acc_ref[hh] * alpha + jnp.dot(vt, p, preferred_element_type=_F32)

    nb = (qi * tq) // tk

    def body(j, carry):
        step(pl.multiple_of(2 * j * tk, tk), tk, False)
        step(pl.multiple_of((2 * j + 1) * tk, tk), tk, False)
        return carry

    lax.fori_loop(0, nb // 2, body, 0)

    @pl.when(nb % 2 == 1)
    def _():
        step(pl.multiple_of((nb - 1) * tk, tk), tk, False)
    if tk != tq:
        @pl.when(qi % (tk // tq) == 1)
        def _():
            step(pl.multiple_of((qi - 1) * tq, tq), tq, False)
    step(pl.multiple_of(qi * tq, tq), tq, True)

    for hh in range(hb):
        ot = acc_ref[hh, :hd, :] / acc_ref[hh, hd:hd + 1, :]
        odt = ot[:, :tq] - lam * ot[:, tq:]
        ms = jnp.mean(odt * odt, axis=0, keepdims=True)
        yt = odt * lax.rsqrt(ms + NORM_EPS) * g_ref[...] * (1.0 - lam_init)
        o_ref[:, hh * hd:(hh + 1) * hd] = yt.T.astype(o_ref.dtype)


def _bf16_pieces(x):
    hi = x.astype(_BF16).astype(np.float32)
    mid = (x - hi).astype(_BF16).astype(np.float32)
    lo = (x - hi - mid).astype(_BF16).astype(np.float32)
    return hi, mid, lo


def _diff_attention(qvt, knat, lam_params, subln_g, slopes, casts, *, batch, seq, n_heads, lam_init, tq, tk, hb):
    assert tk in (tq, 2 * tq)
    nq = seq // tq
    hd = HEAD_DIM
    ng = n_heads // hb
    sig = (slopes.astype(np.float64) * LOG2E).astype(np.float32)
    sig_arr = np.broadcast_to(sig.reshape(n_heads, 1, 1), (n_heads, 1, hd))
    sigp_arr = np.zeros((n_heads, hd, LANES), np.float32)
    for idx, piece in enumerate(_bf16_pieces(sig)):
        sigp_arr[:, idx, :] = piece[:, None]
        sigp_arr[:, 3 + idx, :] = piece[:, None]
    key = np.arange(tk)
    kc_arr = np.zeros((tk, LANES), np.float32)
    kc_arr[:, 0:3] = (key & 255)[:, None]
    kc_arr[:, 3:6] = (key - (key & 255))[:, None]
    ones_rows = 16
    steps = batch * ng * nq
    cast_2d = [c.reshape(-1, c.shape[-1]) for c in casts]
    assert all(c.shape[0] % (16 * steps) == 0 for c in cast_2d), "cast slabs must be whole bf16 sublane tiles"
    cast_specs = [pl.BlockSpec((c.shape[0] // steps, c.shape[1]), lambda b, h, i: ((b * ng + h) * nq + i, 0))
                  for c in cast_2d]
    kern = functools.partial(_diff_attn_kernel, n_cast=len(casts), tq=tq, tk=tk, hb=hb, ones_rows=ones_rows,
                             lam_init=lam_init)
    outs = pl.pallas_call(
        kern,
        grid=(batch, ng, nq),
        in_specs=[
            pl.BlockSpec(lam_params.shape, lambda b, h, i: (0, 0)),
            pl.BlockSpec((hb, 1, hd), lambda b, h, i: (h, 0, 0)),
            pl.BlockSpec((hb, hd, LANES), lambda b, h, i: (h, 0, 0)),
            pl.BlockSpec((hd, 1), lambda b, h, i: (0, 0)),
            pl.BlockSpec((tk, LANES), lambda b, h, i: (0, 0)),
            pl.BlockSpec((hb * hd, tq), lambda b, h, i: (h, b * nq + i)),
            pl.BlockSpec((seq, hb * hd), lambda b, h, i: (b, h)),
            pl.BlockSpec((hb * hd, seq), lambda b, h, i: (ng + h, b)),
        ] + cast_specs,
        out_specs=[pl.BlockSpec((tq, hb * hd), lambda b, h, i: (b * nq + i, h))] + cast_specs,
        out_shape=[jax.ShapeDtypeStruct((batch * seq, n_heads * hd), _BF16)]
        + [jax.ShapeDtypeStruct(c.shape, _BF16) for c in cast_2d],
        scratch_shapes=[
            pltpu.VMEM((hb, 1, 2 * tq), _F32),
            pltpu.VMEM((hb, hd + ones_rows, 2 * tq), _F32),
        ],
        compiler_params=_cparams(("parallel", "parallel", "arbitrary")),
        name="diff_attention",
    )(lam_params, jnp.asarray(sig_arr), jnp.asarray(sigp_arr), subln_g.reshape(hd, 1),
      jnp.asarray(kc_arr, dtype=_BF16), qvt, knat, qvt, *cast_2d)
    return outs[0], [o.reshape(c.shape) for o, c in zip(outs[1:], casts)]


def _rows(ref, n, st, start, size, sl):
    if len(ref.shape) == 2:
        return ref[start:start + size, sl]
    return ref[start // n:(start + size) // n, st * n:(st + 1) * n, sl].reshape(size, sl.stop - sl.start)


def _store_rows(ref, n, st, start, size, sl, val):
    if len(ref.shape) == 2:
        ref[start:start + size, sl] = val
    else:
        ref[start // n:(start + size) // n, st * n:(st + 1) * n, sl] = val.reshape(size // n, n, sl.stop - sl.start)


def _dilated_kernel(*refs, tq, span, n, streams, slopes2, scale2, has_prev):
    if has_prev:
        q_ref, kp_ref, kc_ref, vp_ref, vc_ref, o_ref, lse_ref = refs
    else:
        q_ref, kc_ref, vc_ref, o_ref, lse_ref = refs
    length = q_ref.shape[0] if len(q_ref.shape) == 2 else q_ref.shape[0] * n
    first = pl.program_id(2) == 0
    nh = len(slopes2)
    lane_grp = lax.broadcasted_iota(jnp.int32, (tq, LANES), 1) // (LANES // nh)

    def masked_bias(nk, shift, first_block):
        row = lax.broadcasted_iota(jnp.int32, (tq, nk), 0)
        col = lax.broadcasted_iota(jnp.int32, (tq, nk), 1)
        dist = row - col + shift
        valid = (dist >= 0) & (dist <= span)
        if first_block and has_prev:
            valid = valid & ((col >= span) | jnp.logical_not(first))
        distf = dist.astype(_F32)
        return [jnp.where(valid, -s2 * distf, NEG) for s2 in slopes2]

    bias_head = masked_bias(tq + span if has_prev else tq, span if has_prev else 0, True)
    bias_body = masked_bias(tq + span, span, False) if length > tq else None

    for st in range(streams):
        for qb in range(length // tq):
            u0 = qb * tq
            lses = []
            for hh in range(nh):
                sl = slice(hh * HEAD_DIM, (hh + 1) * HEAD_DIM)
                q = _rows(q_ref, n, st, u0, tq, sl)
                if qb == 0 and has_prev:
                    k = jnp.concatenate([kp_ref[:, sl], _rows(kc_ref, n, st, 0, tq, sl)], axis=0)
                    v = jnp.concatenate([vp_ref[:, sl], _rows(vc_ref, n, st, 0, tq, sl)], axis=0)
                elif qb == 0:
                    k, v = _rows(kc_ref, n, st, 0, tq, sl), _rows(vc_ref, n, st, 0, tq, sl)
                else:
                    k = _rows(kc_ref, n, st, u0 - span, tq + span, sl)
                    v = _rows(vc_ref, n, st, u0 - span, tq + span, sl)
                bias = (bias_head if qb == 0 else bias_body)[hh]
                s = lax.dot_general(q, k, _NT, preferred_element_type=_F32) * scale2 + bias
                m = jnp.max(s, axis=1, keepdims=True)
                p = jnp.exp2(s - m)
                l = jnp.sum(p, axis=1, keepdims=True)
                o = jnp.dot(p.astype(_BF16), v, preferred_element_type=_F32) / l
                _store_rows(o_ref, n, st, u0, tq, sl, o.astype(o_ref.dtype))
                lses.append(m + jnp.log2(l))
            packed = jnp.broadcast_to(lses[nh - 1], (tq, LANES))
            for hh in range(nh - 2, -1, -1):
                packed = jnp.where(lane_grp == hh, lses[hh], packed)
            _store_rows(lse_ref, n, st, u0, tq, slice(0, LANES), packed)


def _dilated_group(proj, slopes_g, *, batch, seq, group, window, dilation, in_width, mix_width):
    span = window // dilation
    assert span == LANES, "key window per stream must be one 128-row block"
    gw = HEADS_PER_GROUP * HEAD_DIM
    koff = mix_width // gw
    rows = batch * seq
    n = PERM_ROWS // dilation
    tq = span if n >= span else 2 * span
    streams = max(1, LANES // n) if dilation > 1 else 1
    kern = functools.partial(
        _dilated_kernel, tq=tq, span=span, n=n, streams=streams,
        slopes2=tuple(float(s) * dilation * LOG2E for s in slopes_g), scale2=HEAD_DIM ** -0.5 * LOG2E,
        has_prev=dilation == 1)
    if dilation == 1:
        chunk = 1024
        nc = seq // chunk
        cur = lambda c: pl.BlockSpec((chunk, gw), lambda b, r, u: (b * nc + u, c))
        prev = lambda c: pl.BlockSpec(
            (span, gw), lambda b, r, u: (b * (seq // span) + jnp.maximum(u * (chunk // span) - 1, 0), c))
        in_specs = [cur(group), prev(koff + group), cur(koff + group), prev(2 * koff + group), cur(2 * koff + group)]
        operands = [proj] * 5
        out_specs = [pl.BlockSpec((chunk, w), lambda b, r, u: (b * nc + u, 0)) for w in (gw, LANES)]
        out_dims = [(rows, gw), (rows, LANES)]
        grid = (batch, 1, nc)
    else:
        tiles = seq // PERM_ROWS
        proj3 = proj.reshape(rows // PERM_ROWS, PERM_ROWS, in_width)
        blk = lambda c, w: pl.BlockSpec((tiles, streams * n, w), lambda b, r, u: (b, r, c))
        in_specs = [blk(group, gw), blk(koff + group, gw), blk(2 * koff + group, gw)]
        operands = [proj3] * 3
        out_specs = [blk(0, gw), blk(0, LANES)]
        out_dims = [(rows // PERM_ROWS, PERM_ROWS, gw), (rows // PERM_ROWS, PERM_ROWS, LANES)]
        grid = (batch, dilation // streams, 1)
    o, lse = pl.pallas_call(
        kern,
        grid=grid,
        in_specs=in_specs,
        out_specs=out_specs,
        out_shape=[jax.ShapeDtypeStruct(out_dims[0], _BF16), jax.ShapeDtypeStruct(out_dims[1], _F32)],
        compiler_params=_cparams(("parallel", "parallel", "arbitrary")),
        name=f"dilated_attention_d{dilation}",
    )(*operands)
    return o.reshape(rows, gw), lse.reshape(rows, LANES)


def _combine_kernel(o0_ref, o1_ref, o2_ref, l0_ref, l1_ref, l2_ref, out_ref, on_ref, ln_ref, *, dilations):
    tm = out_ref.shape[0]
    gw = o0_ref.shape[1]
    nh = gw // LANES
    for g, (o_ref, l_ref, d) in enumerate(zip((o0_ref, o1_ref, o2_ref), (l0_ref, l1_ref, l2_ref), dilations)):
        n = PERM_ROWS // d
        if d == 1:
            ln_ref[g] = l_ref[...]
            for hh in range(nh):
                on_ref[g, hh] = o_ref[:, hh * LANES:(hh + 1) * LANES].astype(_F32)
            continue
        for grp in range(tm // PERM_ROWS):
            for r in range(d):
                src = slice(grp * PERM_ROWS + r * n, grp * PERM_ROWS + (r + 1) * n)
                dst = pl.ds(grp * PERM_ROWS + r, n, stride=d)
                ln_ref[g, dst, :] = l_ref[src, :]
                for hh in range(nh):
                    on_ref[g, hh, dst, :] = o_ref[src, hh * LANES:(hh + 1) * LANES].astype(_F32)
    l0, l1, l2 = ln_ref[0], ln_ref[1], ln_ref[2]
    mx = jnp.maximum(jnp.maximum(l0, l1), l2)
    es = (jnp.exp2(l0 - mx), jnp.exp2(l1 - mx), jnp.exp2(l2 - mx))
    den = es[0] + es[1] + es[2]
    for g in range(3):
        wg = es[g] / den
        for hh in range(nh):
            lane0 = hh * (LANES // nh)
            col = g * gw + hh * LANES
            out_ref[:, col:col + LANES] = (on_ref[g, hh] * wg[:, lane0:lane0 + 1]).astype(out_ref.dtype)


def _combine_groups(outs, lses, dilations, *, tm):
    m, gw = outs[0].shape
    spec = pl.BlockSpec((tm, gw), lambda i: (i, 0))
    return pl.pallas_call(
        functools.partial(_combine_kernel, dilations=dilations),
        grid=(m // tm,),
        in_specs=[spec] * 3 + [pl.BlockSpec((tm, LANES), lambda i: (i, 0))] * 3,
        out_specs=pl.BlockSpec((tm, 3 * gw), lambda i: (i, 0)),
        out_shape=jax.ShapeDtypeStruct((m, 3 * gw), _BF16),
        scratch_shapes=[pltpu.VMEM((3, gw // LANES, tm, LANES), _F32), pltpu.VMEM((3, tm, LANES), _F32)],
        compiler_params=_cparams(("parallel",)),
        name="combine_groups",
    )(*outs, *lses)


def _mem_attn_kernel(q_ref, k_ref, v_ref, o_ref, *, scale):
    for hh in range(N_MEM_HEADS):
        sl = slice(hh * HEAD_DIM, (hh + 1) * HEAD_DIM)
        s = lax.dot_general(q_ref[:, sl], k_ref[:, sl], _NT, preferred_element_type=_F32) * scale
        m = jnp.max(s, axis=1, keepdims=True)
        p = jnp.exp(s - m)
        l = jnp.sum(p, axis=1, keepdims=True)
        o = jnp.dot(p.astype(_BF16), v_ref[:, sl], preferred_element_type=_F32)
        o_ref[:, sl] = (o / l).astype(o_ref.dtype)


def _memory_attention(proj, kvm, *, batch, seq, mem_tokens, q_col_block, tq):
    nq = seq // tq
    mw = N_MEM_HEADS * HEAD_DIM
    return pl.pallas_call(
        functools.partial(_mem_attn_kernel, scale=HEAD_DIM ** -0.5),
        grid=(batch, nq),
        in_specs=[
            pl.BlockSpec((tq, mw), lambda b, i: (b * nq + i, q_col_block)),
            pl.BlockSpec((mem_tokens, mw), lambda b, i: (b, 0)),
            pl.BlockSpec((mem_tokens, mw), lambda b, i: (b, 1)),
        ],
        out_specs=pl.BlockSpec((tq, mw), lambda b, i: (b * nq + i, 0)),
        out_shape=jax.ShapeDtypeStruct((batch * seq, mw), _BF16),
        compiler_params=_cparams(("parallel", "parallel")),
        name="memory_attention",
    )(proj, kvm, kvm)


def _out_proj_kernel(x_ref, a_ref, b_ref, wa_ref, wb_ref, o_ref):
    acc = jnp.dot(a_ref[...], wa_ref[...], preferred_element_type=_F32)
    acc = acc + jnp.dot(b_ref[...], wb_ref[...], preferred_element_type=_F32)
    o_ref[...] = x_ref[...] + acc


def _out_proj(x, o_mix, o_mem, w, layer, *, tm):
    m, d = x.shape
    ka, kb = o_mix.shape[1], o_mem.shape[1]
    assert ka % kb == 0
    return pl.pallas_call(
        _out_proj_kernel,
        grid=(m // tm,),
        in_specs=[
            pl.BlockSpec((tm, d), lambda i: (i, 0)),
            pl.BlockSpec((tm, ka), lambda i: (i, 0)),
            pl.BlockSpec((tm, kb), lambda i: (i, 0)),
            pl.BlockSpec((None, ka, d), lambda i: (layer, 0, 0)),
            pl.BlockSpec((None, kb, d), lambda i: (layer, ka // kb, 0)),
        ],
        out_specs=pl.BlockSpec((tm, d), lambda i: (i, 0)),
        out_shape=jax.ShapeDtypeStruct((m, d), _F32),
        compiler_params=_cparams(("parallel",)),
        name="out_proj",
    )(x, o_mix, o_mem, w, w)


def _mlp_kernel(x_ref, g_ref, w1_ref, w2_ref, gf_ref, o_ref, hn_ref, *, final_norm):
    f = pl.program_id(1)

    def ffn(h):
        a = jnp.dot(h, w1_ref[...], preferred_element_type=_F32)
        a = jnp.square(jnp.maximum(a, 0.0)).astype(_BF16)
        return jnp.dot(a, w2_ref[...], preferred_element_type=_F32)

    @pl.when(f == 0)
    def _():
        x = x_ref[...]
        h = _rms(x, g_ref[...]).astype(hn_ref.dtype)
        hn_ref[...] = h
        o_ref[...] = x + ffn(h)

    @pl.when(f > 0)
    def _():
        o_ref[...] += ffn(hn_ref[...])

    if final_norm:
        @pl.when(f == pl.num_programs(1) - 1)
        def _():
            o_ref[...] = _rms(o_ref[...], gf_ref[...])


def _mlp(x, g, w1, w2, layer, g_final, *, final_norm, tm, tf):
    m, d = x.shape
    ff = w1.shape[2]
    return pl.pallas_call(
        functools.partial(_mlp_kernel, final_norm=final_norm),
        grid=(m // tm, ff // tf),
        in_specs=[
            pl.BlockSpec((tm, d), lambda i, f: (i, 0)),
            pl.BlockSpec((1, d), lambda i, f: (0, 0)),
            pl.BlockSpec((None, d, tf), lambda i, f: (layer, 0, f)),
            pl.BlockSpec((None, tf, d), lambda i, f: (layer, f, 0)),
            pl.BlockSpec((1, d), lambda i, f: (0, 0)),
        ],
        out_specs=pl.BlockSpec((tm, d), lambda i, f: (i, 0)),
        out_shape=jax.ShapeDtypeStruct((m, d), _F32),
        scratch_shapes=[pltpu.VMEM((tm, d), _BF16)],
        compiler_params=_cparams(("parallel", "arbitrary")),
        name="mlp",
    )(x, g.reshape(1, d), w1, w2, g_final.reshape(1, d))


def kernel(x, mem, g_attn, w_in, w_out, lambda_qk, diff_subln_g, g_mem, w_mem_kv, g_mlp, w_mlp1, w_mlp2, g_final):
    batch, seq, d_model = x.shape
    depth = w_in.shape[0]
    mem_tokens = mem.shape[1]
    in_width = w_in.shape[2]
    mem_width = N_MEM_HEADS * HEAD_DIM
    mix_width = (in_width - mem_width) // 3
    n_heads = mix_width // HEAD_DIM
    slopes = _alibi_slopes(n_heads)
    dilations = tuple(d for _, d in DILATED_GROUPS)
    gw = HEADS_PER_GROUP * HEAD_DIM
    n_qkv_tiles = 3 * mix_width // gw

    xf = x.reshape(batch * seq, d_model)
    memf = mem.reshape(batch * mem_tokens, d_model)
    mix_tiles = mix_width // gw
    q_scale = LOG2E * (HEAD_DIM // 2) ** -0.5
    for i in range(depth):
        if i % N_MIXERS == 0:
            j = i // N_MIXERS
            lam_init = 0.8 - 0.6 * math.exp(-0.3 * i)
            k_qm_tiles = [mix_tiles + t for t in range(mix_tiles)] + [3 * mix_tiles]
            q_v_tiles = list(range(mix_tiles)) + [2 * mix_tiles + t for t in range(mix_tiles)]
            if i == 0:
                w_nat = _cast_tiles(w_in, i, k_qm_tiles, tn=gw, transpose=False)[None]
                nat_cols, layer = list(range(mix_tiles + 1)), 0
            else:
                w_nat, nat_cols, layer = w_in_b, k_qm_tiles, i
            proj, qvt = _norm_matmul_dual(
                xf, g_attn[i], w_nat, layer, _cast_tiles(w_in, i, q_v_tiles, tn=gw, transpose=True),
                tm=1024, tn=gw, nat_cols=nat_cols,
                t_rows=[(t, q_scale) for t in range(mix_tiles)] + [(mix_tiles + t, 1.0) for t in range(mix_tiles)])
            casts = [w_mem_kv, w_out, w_mlp1, w_mlp2, w_in] if i == 0 else []
            o_mix, casted = _diff_attention(qvt, proj, lambda_qk[j], diff_subln_g[j], slopes, casts, batch=batch,
                                            seq=seq, n_heads=n_heads, lam_init=lam_init, tq=256, tk=512, hb=6)
            if i == 0:
                w_kv_b, w_out_b, w1_b, w2_b, w_in_b = casted
            qm_col_block = mix_tiles
        else:
            tile_slot = [t % len(dilations) if t < n_qkv_tiles else 0 for t in range(in_width // gw)]
            proj = _norm_matmul(xf, g_attn[i], w_in_b, i, tm=1024, tn=2 * gw, dilations=dilations,
                                tile_slot=tile_slot, slot_width=gw)
            outs, lses = [], []
            for g, (window, dilation) in enumerate(DILATED_GROUPS):
                sl = slice(g * HEADS_PER_GROUP, (g + 1) * HEADS_PER_GROUP)
                o, lse = _dilated_group(proj, slopes[sl], batch=batch, seq=seq, group=g, window=window,
                                        dilation=dilation, in_width=in_width, mix_width=mix_width)
                outs.append(o)
                lses.append(lse)
            o_mix = _combine_groups(outs, lses, dilations, tm=PERM_ROWS)
            qm_col_block = 3 * mix_tiles
        kvm = _norm_matmul(memf, g_mem, w_kv_b, i, tm=512, tn=512)
        o_mem = _memory_attention(proj, kvm, batch=batch, seq=seq, mem_tokens=mem_tokens,
                                  q_col_block=qm_col_block, tq=1024)
        xf = _out_proj(xf, o_mix, o_mem, w_out_b, i, tm=512)
        xf = _mlp(xf, g_mlp[i], w1_b, w2_b, i, g_final, final_norm=(i == depth - 1), tm=1024, tf=512)
    return xf.reshape(batch, seq, d_model)
```

```python
import functools
import math

import numpy as np
import jax
import jax.numpy as jnp
from jax import lax
from jax.experimental import pallas as pl
from jax.experimental.pallas import tpu as pltpu

HEAD_DIM = 128
N_MEM_HEADS = 4
N_MIXERS = 2
DILATED_GROUPS = ((128, 1), (512, 4), (2048, 16))
HEADS_PER_GROUP = 4
NORM_EPS = 1e-6
NEG = -1e30
LANES = 128
VMEM_LIMIT = 56 * 1024 * 1024
PERM_ROWS = 512
LOG2E = math.log2(math.e)

_F32 = jnp.float32
_BF16 = jnp.bfloat16
_NT = (((1,), (1,)), ((), ()))


def _alibi_slopes(n):
    def pow2(m):
        start = 2.0 ** (-(2.0 ** -(math.log2(m) - 3)))
        return [start * start ** i for i in range(m)]

    def slopes(m):
        if math.log2(m).is_integer():
            return pow2(m)
        c = 2 ** math.floor(math.log2(m))
        return pow2(c) + slopes(2 * c)[0::2][: m - c]

    return np.asarray(sorted(slopes(n), reverse=True), dtype=np.float32)


def _cparams(semantics):
    return pltpu.CompilerParams(dimension_semantics=semantics, vmem_limit_bytes=VMEM_LIMIT)


def _rms(x, g):
    ms = jnp.mean(x * x, axis=-1, keepdims=True)
    return x * lax.rsqrt(ms + NORM_EPS) * g


def _norm_matmul_kernel(x_ref, g_ref, w_ref, o_ref, xn_ref, hn_ref, *, dilations, tile_slot, slot_width):
    j = pl.program_id(1)
    tm = x_ref.shape[0]
    parts = o_ref.shape[1] // slot_width

    def part(t, h):
        cols = slice(t * slot_width, (t + 1) * slot_width)
        o_ref[:, cols] = jnp.dot(h, w_ref[:, cols], preferred_element_type=_F32).astype(o_ref.dtype)

    def permute(s):
        d = dilations[s]
        n = PERM_ROWS // d
        for c in range(xn_ref.shape[0]):
            for grp in range(tm // PERM_ROWS):
                for r in range(d):
                    rows = xn_ref[c, pl.ds(grp * PERM_ROWS + r, n, stride=d), :]
                    dst = grp * PERM_ROWS + r * n
                    hn_ref[s, dst:dst + n, c * LANES:(c + 1) * LANES] = rows.astype(hn_ref.dtype)

    @pl.when(j == 0)
    def _():
        xn = _rms(x_ref[...], g_ref[...])
        h0 = xn.astype(hn_ref.dtype)
        hn_ref[0] = h0
        if len(dilations) > 1:
            for c in range(xn_ref.shape[0]):
                xn_ref[c] = xn[:, c * LANES:(c + 1) * LANES]
        done = {0}
        for t in range(parts):
            s = tile_slot[t]
            if s not in done:
                permute(s)
                done.add(s)
            part(t, h0 if s == 0 else hn_ref[s])
        for s in range(len(dilations)):
            if s not in done:
                permute(s)

    @pl.when(j > 0)
    def _():
        for t in range(parts):
            slot = _pick(j * parts + t, list(tile_slot)) if len(dilations) > 1 else 0
            part(t, hn_ref[slot])


def _norm_matmul(x, g, w, layer, *, tm, tn, dilations=(1,), tile_slot=None, slot_width=None):
    m, k = x.shape
    n_tiles = w.shape[2] // tn
    slot_width = tn if slot_width is None else slot_width
    tile_slot = (0,) * (n_tiles * tn // slot_width) if tile_slot is None else tuple(tile_slot)
    xn_shape = (k // LANES, tm, LANES) if len(dilations) > 1 else (1, 8, LANES)
    return pl.pallas_call(
        functools.partial(_norm_matmul_kernel, dilations=dilations, tile_slot=tile_slot, slot_width=slot_width),
        grid=(m // tm, n_tiles),
        in_specs=[
            pl.BlockSpec((tm, k), lambda i, j: (i, 0)),
            pl.BlockSpec((1, k), lambda i, j: (0, 0)),
            pl.BlockSpec((None, k, tn), lambda i, j: (layer, 0, j)),
        ],
        out_specs=pl.BlockSpec((tm, tn), lambda i, j: (i, j)),
        out_shape=jax.ShapeDtypeStruct((m, n_tiles * tn), _BF16),
        scratch_shapes=[pltpu.VMEM(xn_shape, _F32), pltpu.VMEM((len(dilations), tm, k), _BF16)],
        compiler_params=_cparams(("parallel", "arbitrary")),
        name="norm_matmul",
    )(x, g.reshape(1, k), w)


def _cast_tiles_kernel(w_ref, o_ref, *, transpose):
    w = w_ref[...]
    o_ref[...] = (w.T if transpose else w).astype(o_ref.dtype)


def _cast_tiles(w, layer, col_tiles, *, tn, transpose):
    k = w.shape[1]
    n = len(col_tiles)
    if transpose:
        out_spec, out_dims = pl.BlockSpec((tn, k), lambda j: (j, 0)), (n * tn, k)
    else:
        out_spec, out_dims = pl.BlockSpec((k, tn), lambda j: (0, j)), (k, n * tn)
    return pl.pallas_call(
        functools.partial(_cast_tiles_kernel, transpose=transpose),
        grid=(n,),
        in_specs=[pl.BlockSpec((None, k, tn), lambda j: (layer, 0, _pick(j, list(col_tiles))))],
        out_specs=out_spec,
        out_shape=jax.ShapeDtypeStruct(out_dims, _BF16),
        compiler_params=_cparams(("parallel",)),
        name="cast_tiles_t" if transpose else "cast_tiles",
    )(w)


def _pick(j, values):
    out = values[-1]
    for idx in range(len(values) - 2, -1, -1):
        out = jnp.where(j == idx, values[idx], out)
    return out


def _norm_matmul_dual_kernel(x_ref, g_ref, w_ref, wt_ref, nat_ref, tr_ref, hn_ref, *, nat_steps, scaled_rows, scale):
    j = pl.program_id(1)
    tn = wt_ref.shape[0]

    def natural(h):
        nat_ref[...] = jnp.dot(h, w_ref[...], preferred_element_type=_F32).astype(nat_ref.dtype)

    @pl.when(j == 0)
    def _():
        h = _rms(x_ref[...], g_ref[...]).astype(hn_ref.dtype)
        hn_ref[...] = h
        natural(h)

    @pl.when((j > 0) & (j < nat_steps))
    def _():
        natural(hn_ref[...])

    @pl.when(j >= nat_steps)
    def _():
        acc = lax.dot_general(wt_ref[...], hn_ref[...], _NT, preferred_element_type=_F32)
        row = (j - nat_steps) * tn + lax.broadcasted_iota(jnp.int32, (tn, 1), 0)
        tr_ref[...] = (acc * jnp.where(row < scaled_rows, scale, 1.0)).astype(tr_ref.dtype)


def _norm_matmul_dual(x, g, w, wt, *, tm, tn, scaled_rows, scale):
    m, k = x.shape
    nat_steps, t_steps = w.shape[1] // tn, wt.shape[0] // tn
    return pl.pallas_call(
        functools.partial(_norm_matmul_dual_kernel, nat_steps=nat_steps, scaled_rows=scaled_rows, scale=scale),
        grid=(m // tm, nat_steps + t_steps),
        in_specs=[
            pl.BlockSpec((tm, k), lambda i, j: (i, 0)),
            pl.BlockSpec((1, k), lambda i, j: (0, 0)),
            pl.BlockSpec((k, tn), lambda i, j: (0, jnp.minimum(j, nat_steps - 1))),
            pl.BlockSpec((tn, k), lambda i, j: (jnp.maximum(j - nat_steps, 0), 0)),
        ],
        out_specs=[
            pl.BlockSpec((tm, tn), lambda i, j: (i, jnp.minimum(j, nat_steps - 1))),
            pl.BlockSpec((tn, tm), lambda i, j: (jnp.maximum(j - nat_steps, 0), i)),
        ],
        out_shape=[
            jax.ShapeDtypeStruct((m, w.shape[1]), _BF16),
            jax.ShapeDtypeStruct((wt.shape[0], m), _BF16),
        ],
        scratch_shapes=[pltpu.VMEM((tm, k), _BF16)],
        compiler_params=_cparams(("parallel", "arbitrary")),
        name="norm_matmul_dual",
    )(x, g.reshape(1, k), w, wt)


def _diff_attn_kernel(*refs, n_cast, tq, tk, hb, ones_rows, lam_init):
    lam_ref, sig_ref, sigp_ref, g_ref, kc_ref, qt_ref, k_ref, vt_ref = refs[:8]
    cast_in = refs[8:8 + n_cast]
    o_ref = refs[8 + n_cast]
    cast_out = refs[9 + n_cast:9 + 2 * n_cast]
    m_ref, acc_ref = refs[9 + 2 * n_cast:]
    for src, dst in zip(cast_in, cast_out):
        dst[...] = src[...].astype(dst.dtype)
    qi = pl.program_id(2)
    hd = HEAD_DIM
    half = hd // 2
    reps = 2 * tq // LANES

    lp = lam_ref[...]
    lam = (jnp.exp(jnp.sum(lp[0:1] * lp[1:2], axis=1, keepdims=True))
           - jnp.exp(jnp.sum(lp[2:3] * lp[3:4], axis=1, keepdims=True)) + lam_init)

    row = lax.broadcasted_iota(jnp.int32, (hd, tq), 0)
    qzts, sigs = [], []
    for hh in range(hb):
        qt = qt_ref[hh * hd:(hh + 1) * hd, :]
        zero = jnp.zeros_like(qt)
        top = jnp.concatenate([jnp.where(row < half, qt, zero), jnp.where(row >= half, qt, zero)], axis=1)
        ext = jnp.concatenate([sigp_ref[hh]] * reps, axis=1).astype(_BF16)
        qzts.append(jnp.concatenate([top, ext], axis=0))
        sigs.append(sig_ref[hh][:, :1])
    key_i = lax.broadcasted_iota(jnp.int32, (tq, 2 * tq), 0)
    qry_i = lax.broadcasted_iota(jnp.int32, (tq, 2 * tq), 1)
    causal = key_i <= jnp.where(qry_i >= tq, qry_i - tq, qry_i)

    m_ref[...] = jnp.full(m_ref.shape, NEG, _F32)
    acc_ref[...] = jnp.zeros(acc_ref.shape, _F32)

    def step(start, size, masked):
        off = (start - qi * tq).astype(_F32)
        kc = kc_ref[:size, :]
        ones = jnp.ones((ones_rows, size), _BF16)
        ps = []
        for hh in range(hb):
            k = jnp.concatenate([k_ref[pl.ds(start, size), hh * hd:(hh + 1) * hd], kc], axis=1)
            c = sigs[hh] * off
            s = jnp.dot(k, qzts[hh], preferred_element_type=_F32)
            if masked:
                s = jnp.where(causal, s, NEG)
            m_prev = m_ref[hh]
            m_new = jnp.maximum(m_prev, jnp.max(s, axis=0, keepdims=True) + c)
            alpha = jnp.exp2(m_prev - m_new)
            p = jnp.exp2(s - (m_new - c))
            m_ref[hh] = m_new
            ps.append((p.astype(_BF16), alpha))
        for hh in range(hb):
            p, alpha = ps[hh]
            vt = jnp.concatenate([vt_ref[hh * hd:(hh + 1) * hd, pl.ds(start, size)], ones], axis=0)
            acc_ref[hh] = acc_ref[hh] * alpha + jnp.dot(vt, p, preferred_element_type=_F32)

    nb = (qi * tq) // tk

    def body(j, carry):
        step(pl.multiple_of(2 * j * tk, tk), tk, False)
        step(pl.multiple_of((2 * j + 1) * tk, tk), tk, False)
        return carry

    lax.fori_loop(0, nb // 2, body, 0)

    @pl.when(nb % 2 == 1)
    def _():
        step(pl.multiple_of((nb - 1) * tk, tk), tk, False)
    if tk != tq:
        @pl.when(qi % (tk // tq) == 1)
        def _():
            step(pl.multiple_of((qi - 1) * tq, tq), tq, False)
    step(pl.multiple_of(qi * tq, tq), tq, True)

    for hh in range(hb):
        ot = acc_ref[hh, :hd, :] / acc_ref[hh, hd:hd + 1, :]
        odt = ot[:, :tq] - lam * ot[:, tq:]
        ms = jnp.mean(odt * odt, axis=0, keepdims=True)
        yt = odt * lax.rsqrt(ms + NORM_EPS) * g_ref[...] * (1.0 - lam_init)
        o_ref[:, hh * hd:(hh + 1) * hd] = yt.T.astype(o_ref.dtype)


def _bf16_pieces(x):
    hi = x.astype(_BF16).astype(np.float32)
    mid = (x - hi).astype(_BF16).astype(np.float32)
    lo = (x - hi - mid).astype(_BF16).astype(np.float32)
    return hi, mid, lo


def _diff_attention(qvt, knat, lam_params, subln_g, slopes, casts, *, batch, seq, n_heads, lam_init, tq, tk, hb):
    assert tk in (tq, 2 * tq)
    nq = seq // tq
    hd = HEAD_DIM
    ng = n_heads // hb
    sig = (slopes.astype(np.float64) * LOG2E).astype(np.float32)
    sig_arr = np.broadcast_to(sig.reshape(n_heads, 1, 1), (n_heads, 1, hd))
    sigp_arr = np.zeros((n_heads, hd, LANES), np.float32)
    for idx, piece in enumerate(_bf16_pieces(sig)):
        sigp_arr[:, idx, :] = piece[:, None]
        sigp_arr[:, 3 + idx, :] = piece[:, None]
    key = np.arange(tk)
    kc_arr = np.zeros((tk, LANES), np.float32)
    kc_arr[:, 0:3] = (key & 255)[:, None]
    kc_arr[:, 3:6] = (key - (key & 255))[:, None]
    ones_rows = 16
    steps = batch * ng * nq
    cast_2d = [c.reshape(-1, c.shape[-1]) for c in casts]
    assert all(c.shape[0] % (16 * steps) == 0 for c in cast_2d), "cast slabs must be whole bf16 sublane tiles"
    cast_specs = [pl.BlockSpec((c.shape[0] // steps, c.shape[1]), lambda b, h, i: ((b * ng + h) * nq + i, 0))
                  for c in cast_2d]
    kern = functools.partial(_diff_attn_kernel, n_cast=len(casts), tq=tq, tk=tk, hb=hb, ones_rows=ones_rows,
                             lam_init=lam_init)
    outs = pl.pallas_call(
        kern,
        grid=(batch, ng, nq),
        in_specs=[
            pl.BlockSpec(lam_params.shape, lambda b, h, i: (0, 0)),
            pl.BlockSpec((hb, 1, hd), lambda b, h, i: (h, 0, 0)),
            pl.BlockSpec((hb, hd, LANES), lambda b, h, i: (h, 0, 0)),
            pl.BlockSpec((hd, 1), lambda b, h, i: (0, 0)),
            pl.BlockSpec((tk, LANES), lambda b, h, i: (0, 0)),
            pl.BlockSpec((hb * hd, tq), lambda b, h, i: (h, b * nq + i)),
            pl.BlockSpec((seq, hb * hd), lambda b, h, i: (b, h)),
            pl.BlockSpec((hb * hd, seq), lambda b, h, i: (ng + h, b)),
        ] + cast_specs,
        out_specs=[pl.BlockSpec((tq, hb * hd), lambda b, h, i: (b * nq + i, h))] + cast_specs,
        out_shape=[jax.ShapeDtypeStruct((batch * seq, n_heads * hd), _BF16)]
        + [jax.ShapeDtypeStruct(c.shape, _BF16) for c in cast_2d],
        scratch_shapes=[
            pltpu.VMEM((hb, 1, 2 * tq), _F32),
            pltpu.VMEM((hb, hd + ones_rows, 2 * tq), _F32),
        ],
        compiler_params=_cparams(("parallel", "parallel", "arbitrary")),
        name="diff_attention",
    )(lam_params, jnp.asarray(sig_arr), jnp.asarray(sigp_arr), subln_g.reshape(hd, 1),
      jnp.asarray(kc_arr, dtype=_BF16), qvt, knat, qvt, *cast_2d)
    return outs[0], [o.reshape(c.shape) for o, c in zip(outs[1:], casts)]


def _rows(ref, n, st, start, size, sl):
    if len(ref.shape) == 2:
        return ref[start:start + size, sl]
    return ref[start // n:(start + size) // n, st * n:(st + 1) * n, sl].reshape(size, sl.stop - sl.start)


def _store_rows(ref, n, st, start, size, sl, val):
    if len(ref.shape) == 2:
        ref[start:start + size, sl] = val
    else:
        ref[start // n:(start + size) // n, st * n:(st + 1) * n, sl] = val.reshape(size // n, n, sl.stop - sl.start)


def _dilated_kernel(*refs, tq, span, n, streams, slopes2, scale2, has_prev):
    if has_prev:
        q_ref, kp_ref, kc_ref, vp_ref, vc_ref, o_ref, lse_ref = refs
    else:
        q_ref, kc_ref, vc_ref, o_ref, lse_ref = refs
    length = q_ref.shape[0] if len(q_ref.shape) == 2 else q_ref.shape[0] * n
    first = pl.program_id(2) == 0
    nh = len(slopes2)
    lane_grp = lax.broadcasted_iota(jnp.int32, (tq, LANES), 1) // (LANES // nh)

    def masked_bias(nk, shift, first_block):
        row = lax.broadcasted_iota(jnp.int32, (tq, nk), 0)
        col = lax.broadcasted_iota(jnp.int32, (tq, nk), 1)
        dist = row - col + shift
        valid = (dist >= 0) & (dist <= span)
        if first_block and has_prev:
            valid = valid & ((col >= span) | jnp.logical_not(first))
        distf = dist.astype(_F32)
        return [jnp.where(valid, -s2 * distf, NEG) for s2 in slopes2]

    bias_head = masked_bias(tq + span if has_prev else tq, span if has_prev else 0, True)
    bias_body = masked_bias(tq + span, span, False) if length > tq else None

    for st in range(streams):
        for qb in range(length // tq):
            u0 = qb * tq
            lses = []
            for hh in range(nh):
                sl = slice(hh * HEAD_DIM, (hh + 1) * HEAD_DIM)
                q = _rows(q_ref, n, st, u0, tq, sl)
                if qb == 0 and has_prev:
                    k = jnp.concatenate([kp_ref[:, sl], _rows(kc_ref, n, st, 0, tq, sl)], axis=0)
                    v = jnp.concatenate([vp_ref[:, sl], _rows(vc_ref, n, st, 0, tq, sl)], axis=0)
                elif qb == 0:
                    k, v = _rows(kc_ref, n, st, 0, tq, sl), _rows(vc_ref, n, st, 0, tq, sl)
                else:
                    k = _rows(kc_ref, n, st, u0 - span, tq + span, sl)
                    v = _rows(vc_ref, n, st, u0 - span, tq + span, sl)
                bias = (bias_head if qb == 0 else bias_body)[hh]
                s = lax.dot_general(q, k, _NT, preferred_element_type=_F32) * scale2 + bias
                m = jnp.max(s, axis=1, keepdims=True)
                p = jnp.exp2(s - m)
                l = jnp.sum(p, axis=1, keepdims=True)
                o = jnp.dot(p.astype(_BF16), v, preferred_element_type=_F32) / l
                _store_rows(o_ref, n, st, u0, tq, sl, o.astype(o_ref.dtype))
                lses.append(m + jnp.log2(l))
            packed = jnp.broadcast_to(lses[nh - 1], (tq, LANES))
            for hh in range(nh - 2, -1, -1):
                packed = jnp.where(lane_grp == hh, lses[hh], packed)
            _store_rows(lse_ref, n, st, u0, tq, slice(0, LANES), packed)


def _dilated_group(proj, slopes_g, *, batch, seq, group, window, dilation, in_width, mix_width):
    span = window // dilation
    assert span == LANES, "key window per stream must be one 128-row block"
    gw = HEADS_PER_GROUP * HEAD_DIM
    koff = mix_width // gw
    rows = batch * seq
    n = PERM_ROWS // dilation
    tq = span if n >= span else 2 * span
    streams = max(1, LANES // n) if dilation > 1 else 1
    kern = functools.partial(
        _dilated_kernel, tq=tq, span=span, n=n, streams=streams,
        slopes2=tuple(float(s) * dilation * LOG2E for s in slopes_g), scale2=HEAD_DIM ** -0.5 * LOG2E,
        has_prev=dilation == 1)
    if dilation == 1:
        chunk = 1024
        nc = seq // chunk
        cur = lambda c: pl.BlockSpec((chunk, gw), lambda b, r, u: (b * nc + u, c))
        prev = lambda c: pl.BlockSpec(
            (span, gw), lambda b, r, u: (b * (seq // span) + jnp.maximum(u * (chunk // span) - 1, 0), c))
        in_specs = [cur(group), prev(koff + group), cur(koff + group), prev(2 * koff + group), cur(2 * koff + group)]
        operands = [proj] * 5
        out_specs = [pl.BlockSpec((chunk, w), lambda b, r, u: (b * nc + u, 0)) for w in (gw, LANES)]
        out_dims = [(rows, gw), (rows, LANES)]
        grid = (batch, 1, nc)
    else:
        tiles = seq // PERM_ROWS
        proj3 = proj.reshape(rows // PERM_ROWS, PERM_ROWS, in_width)
        blk = lambda c, w: pl.BlockSpec((tiles, streams * n, w), lambda b, r, u: (b, r, c))
        in_specs = [blk(group, gw), blk(koff + group, gw), blk(2 * koff + group, gw)]
        operands = [proj3] * 3
        out_specs = [blk(0, gw), blk(0, LANES)]
        out_dims = [(rows // PERM_ROWS, PERM_ROWS, gw), (rows // PERM_ROWS, PERM_ROWS, LANES)]
        grid = (batch, dilation // streams, 1)
    o, lse = pl.pallas_call(
        kern,
        grid=grid,
        in_specs=in_specs,
        out_specs=out_specs,
        out_shape=[jax.ShapeDtypeStruct(out_dims[0], _BF16), jax.ShapeDtypeStruct(out_dims[1], _F32)],
        compiler_params=_cparams(("parallel", "parallel", "arbitrary")),
        name=f"dilated_attention_d{dilation}",
    )(*operands)
    return o.reshape(rows, gw), lse.reshape(rows, LANES)


def _combine_kernel(o0_ref, o1_ref, o2_ref, l0_ref, l1_ref, l2_ref, out_ref, on_ref, ln_ref, *, dilations):
    tm = out_ref.shape[0]
    gw = o0_ref.shape[1]
    nh = gw // LANES
    for g, (o_ref, l_ref, d) in enumerate(zip((o0_ref, o1_ref, o2_ref), (l0_ref, l1_ref, l2_ref), dilations)):
        n = PERM_ROWS // d
        if d == 1:
            ln_ref[g] = l_ref[...]
            for hh in range(nh):
                on_ref[g, hh] = o_ref[:, hh * LANES:(hh + 1) * LANES].astype(_F32)
            continue
        for grp in range(tm // PERM_ROWS):
            for r in range(d):
                src = slice(grp * PERM_ROWS + r * n, grp * PERM_ROWS + (r + 1) * n)
                dst = pl.ds(grp * PERM_ROWS + r, n, stride=d)
                ln_ref[g, dst, :] = l_ref[src, :]
                for hh in range(nh):
                    on_ref[g, hh, dst, :] = o_ref[src, hh * LANES:(hh + 1) * LANES].astype(_F32)
    l0, l1, l2 = ln_ref[0], ln_ref[1], ln_ref[2]
    mx = jnp.maximum(jnp.maximum(l0, l1), l2)
    es = (jnp.exp2(l0 - mx), jnp.exp2(l1 - mx), jnp.exp2(l2 - mx))
    den = es[0] + es[1] + es[2]
    for g in range(3):
        wg = es[g] / den
        for hh in range(nh):
            lane0 = hh * (LANES // nh)
            col = g * gw + hh * LANES
            out_ref[:, col:col + LANES] = (on_ref[g, hh] * wg[:, lane0:lane0 + 1]).astype(out_ref.dtype)


def _combine_groups(outs, lses, dilations, *, tm):
    m, gw = outs[0].shape
    spec = pl.BlockSpec((tm, gw), lambda i: (i, 0))
    return pl.pallas_call(
        functools.partial(_combine_kernel, dilations=dilations),
        grid=(m // tm,),
        in_specs=[spec] * 3 + [pl.BlockSpec((tm, LANES), lambda i: (i, 0))] * 3,
        out_specs=pl.BlockSpec((tm, 3 * gw), lambda i: (i, 0)),
        out_shape=jax.ShapeDtypeStruct((m, 3 * gw), _BF16),
        scratch_shapes=[pltpu.VMEM((3, gw // LANES, tm, LANES), _F32), pltpu.VMEM((3, tm, LANES), _F32)],
        compiler_params=_cparams(("parallel",)),
        name="combine_groups",
    )(*outs, *lses)


def _mem_attn_kernel(q_ref, k_ref, v_ref, o_ref, *, scale):
    for hh in range(N_MEM_HEADS):
        sl = slice(hh * HEAD_DIM, (hh + 1) * HEAD_DIM)
        s = lax.dot_general(q_ref[:, sl], k_ref[:, sl], _NT, preferred_element_type=_F32) * scale
        m = jnp.max(s, axis=1, keepdims=True)
        p = jnp.exp(s - m)
        l = jnp.sum(p, axis=1, keepdims=True)
        o = jnp.dot(p.astype(_BF16), v_ref[:, sl], preferred_element_type=_F32)
        o_ref[:, sl] = (o / l).astype(o_ref.dtype)


def _memory_attention(proj, kvm, *, batch, seq, mem_tokens, q_col_block, tq):
    nq = seq // tq
    mw = N_MEM_HEADS * HEAD_DIM
    return pl.pallas_call(
        functools.partial(_mem_attn_kernel, scale=HEAD_DIM ** -0.5),
        grid=(batch, nq),
        in_specs=[
            pl.BlockSpec((tq, mw), lambda b, i: (b * nq + i, q_col_block)),
            pl.BlockSpec((mem_tokens, mw), lambda b, i: (b, 0)),
            pl.BlockSpec((mem_tokens, mw), lambda b, i: (b, 1)),
        ],
        out_specs=pl.BlockSpec((tq, mw), lambda b, i: (b * nq + i, 0)),
        out_shape=jax.ShapeDtypeStruct((batch * seq, mw), _BF16),
        compiler_params=_cparams(("parallel", "parallel")),
        name="memory_attention",
    )(proj, kvm, kvm)


def _out_proj_kernel(x_ref, a_ref, b_ref, wa_ref, wb_ref, o_ref):
    acc = jnp.dot(a_ref[...], wa_ref[...], preferred_element_type=_F32)
    acc = acc + jnp.dot(b_ref[...], wb_ref[...], preferred_element_type=_F32)
    o_ref[...] = x_ref[...] + acc


def _out_proj(x, o_mix, o_mem, w, layer, *, tm):
    m, d = x.shape
    ka, kb = o_mix.shape[1], o_mem.shape[1]
    assert ka % kb == 0
    return pl.pallas_call(
        _out_proj_kernel,
        grid=(m // tm,),
        in_specs=[
            pl.BlockSpec((tm, d), lambda i: (i, 0)),
            pl.BlockSpec((tm, ka), lambda i: (i, 0)),
            pl.BlockSpec((tm, kb), lambda i: (i, 0)),
            pl.BlockSpec((None, ka, d), lambda i: (layer, 0, 0)),
            pl.BlockSpec((None, kb, d), lambda i: (layer, ka // kb, 0)),
        ],
        out_specs=pl.BlockSpec((tm, d), lambda i: (i, 0)),
        out_shape=jax.ShapeDtypeStruct((m, d), _F32),
        compiler_params=_cparams(("parallel",)),
        name="out_proj",
    )(x, o_mix, o_mem, w, w)


def _mlp_kernel(x_ref, g_ref, w1_ref, w2_ref, gf_ref, o_ref, hn_ref, *, final_norm):
    f = pl.program_id(1)

    def ffn(h):
        a = jnp.dot(h, w1_ref[...], preferred_element_type=_F32)
        a = jnp.square(jnp.maximum(a, 0.0)).astype(_BF16)
        return jnp.dot(a, w2_ref[...], preferred_element_type=_F32)

    @pl.when(f == 0)
    def _():
        x = x_ref[...]
        h = _rms(x, g_ref[...]).astype(hn_ref.dtype)
        hn_ref[...] = h
        o_ref[...] = x + ffn(h)

    @pl.when(f > 0)
    def _():
        o_ref[...] += ffn(hn_ref[...])

    if final_norm:
        @pl.when(f == pl.num_programs(1) - 1)
        def _():
            o_ref[...] = _rms(o_ref[...], gf_ref[...])


def _mlp(x, g, w1, w2, layer, g_final, *, final_norm, tm, tf):
    m, d = x.shape
    ff = w1.shape[2]
    return pl.pallas_call(
        functools.partial(_mlp_kernel, final_norm=final_norm),
        grid=(m // tm, ff // tf),
        in_specs=[
            pl.BlockSpec((tm, d), lambda i, f: (i, 0)),
            pl.BlockSpec((1, d), lambda i, f: (0, 0)),
            pl.BlockSpec((None, d, tf), lambda i, f: (layer, 0, f)),
            pl.BlockSpec((None, tf, d), lambda i, f: (layer, f, 0)),
            pl.BlockSpec((1, d), lambda i, f: (0, 0)),
        ],
        out_specs=pl.BlockSpec((tm, d), lambda i, f: (i, 0)),
        out_shape=jax.ShapeDtypeStruct((m, d), _F32),
        scratch_shapes=[pltpu.VMEM((tm, d), _BF16)],
        compiler_params=_cparams(("parallel", "arbitrary")),
        name="mlp",
    )(x, g.reshape(1, d), w1, w2, g_final.reshape(1, d))


def kernel(x, mem, g_attn, w_in, w_out, lambda_qk, diff_subln_g, g_mem, w_mem_kv, g_mlp, w_mlp1, w_mlp2, g_final):
    batch, seq, d_model = x.shape
    depth = w_in.shape[0]
    mem_tokens = mem.shape[1]
    in_width = w_in.shape[2]
    mem_width = N_MEM_HEADS * HEAD_DIM
    mix_width = (in_width - mem_width) // 3
    n_heads = mix_width // HEAD_DIM
    slopes = _alibi_slopes(n_heads)
    dilations = tuple(d for _, d in DILATED_GROUPS)
    gw = HEADS_PER_GROUP * HEAD_DIM
    n_qkv_tiles = 3 * mix_width // gw

    xf = x.reshape(batch * seq, d_model)
    memf = mem.reshape(batch * mem_tokens, d_model)
    mix_tiles = mix_width // gw
    q_scale = LOG2E * (HEAD_DIM // 2) ** -0.5
    for i in range(depth):
        if i % N_MIXERS == 0:
            j = i // N_MIXERS
            lam_init = 0.8 - 0.6 * math.exp(-0.3 * i)
            k_qm_tiles = [mix_tiles + t for t in range(mix_tiles)] + [3 * mix_tiles]
            q_v_tiles = list(range(mix_tiles)) + [2 * mix_tiles + t for t in range(mix_tiles)]
            proj, qvt = _norm_matmul_dual(
                xf, g_attn[i], _cast_tiles(w_in, i, k_qm_tiles, tn=gw, transpose=False),
                _cast_tiles(w_in, i, q_v_tiles, tn=gw, transpose=True),
                tm=1024, tn=2 * gw, scaled_rows=mix_width, scale=q_scale)
            casts = [w_mem_kv, w_out, w_mlp1, w_mlp2, w_in] if i == 0 else []
            o_mix, casted = _diff_attention(qvt, proj, lambda_qk[j], diff_subln_g[j], slopes, casts, batch=batch,
                                            seq=seq, n_heads=n_heads, lam_init=lam_init, tq=256, tk=512, hb=6)
            if i == 0:
                w_kv_b, w_out_b, w1_b, w2_b, w_in_b = casted
            qm_col_block = mix_tiles
        else:
            tile_slot = [t % len(dilations) if t < n_qkv_tiles else 0 for t in range(in_width // gw)]
            proj = _norm_matmul(xf, g_attn[i], w_in_b, i, tm=1024, tn=2 * gw, dilations=dilations,
                                tile_slot=tile_slot, slot_width=gw)
            outs, lses = [], []
            for g, (window, dilation) in enumerate(DILATED_GROUPS):
                sl = slice(g * HEADS_PER_GROUP, (g + 1) * HEADS_PER_GROUP)
                o, lse = _dilated_group(proj, slopes[sl], batch=batch, seq=seq, group=g, window=window,
                                        dilation=dilation, in_width=in_width, mix_width=mix_width)
                outs.append(o)
                lses.append(lse)
            o_mix = _combine_groups(outs, lses, dilations, tm=PERM_ROWS)
            qm_col_block = 3 * mix_tiles
        kvm = _norm_matmul(memf, g_mem, w_kv_b, i, tm=512, tn=512)
        o_mem = _memory_attention(proj, kvm, batch=batch, seq=seq, mem_tokens=mem_tokens,
                                  q_col_block=qm_col_block, tq=1024)
        xf = _out_proj(xf, o_mix, o_mem, w_out_b, i, tm=512)
        xf = _mlp(xf, g_mlp[i], w1_b, w2_b, i, g_final, final_norm=(i == depth - 1), tm=1024, tf=512)
    return xf.reshape(batch, seq, d_model)
```

```python
import functools
import math

import numpy as np
import jax
import jax.numpy as jnp
from jax import lax
from jax.experimental import pallas as pl
from jax.experimental.pallas import tpu as pltpu

HEAD_DIM = 128
N_MEM_HEADS = 4
N_MIXERS = 2
DILATED_GROUPS = ((128, 1), (512, 4), (2048, 16))
HEADS_PER_GROUP = 4
NORM_EPS = 1e-6
NEG = -1e30
LANES = 128
VMEM_LIMIT = 56 * 1024 * 1024
PERM_ROWS = 512
LOG2E = math.log2(math.e)

_F32 = jnp.float32
_BF16 = jnp.bfloat16
_NT = (((1,), (1,)), ((), ()))


def _alibi_slopes(n):
    def pow2(m):
        start = 2.0 ** (-(2.0 ** -(math.log2(m) - 3)))
        return [start * start ** i for i in range(m)]

    def slopes(m):
        if math.log2(m).is_integer():
            return pow2(m)
        c = 2 ** math.floor(math.log2(m))
        return pow2(c) + slopes(2 * c)[0::2][: m - c]

    return np.asarray(sorted(slopes(n), reverse=True), dtype=np.float32)


def _cparams(semantics):
    return pltpu.CompilerParams(dimension_semantics=semantics, vmem_limit_bytes=VMEM_LIMIT)


def _rms(x, g):
    ms = jnp.mean(x * x, axis=-1, keepdims=True)
    return x * lax.rsqrt(ms + NORM_EPS) * g


def _norm_matmul_kernel(x_ref, g_ref, w_ref, o_ref, xn_ref, hn_ref, *, dilations, tile_slot, slot_width):
    j = pl.program_id(1)
    tm = x_ref.shape[0]
    parts = o_ref.shape[1] // slot_width

    def part(t, h):
        cols = slice(t * slot_width, (t + 1) * slot_width)
        o_ref[:, cols] = jnp.dot(h, w_ref[:, cols], preferred_element_type=_F32).astype(o_ref.dtype)

    def permute(s):
        d = dilations[s]
        n = PERM_ROWS // d
        for c in range(xn_ref.shape[0]):
            for grp in range(tm // PERM_ROWS):
                for r in range(d):
                    rows = xn_ref[c, pl.ds(grp * PERM_ROWS + r, n, stride=d), :]
                    dst = grp * PERM_ROWS + r * n
                    hn_ref[s, dst:dst + n, c * LANES:(c + 1) * LANES] = rows.astype(hn_ref.dtype)

    @pl.when(j == 0)
    def _():
        xn = _rms(x_ref[...], g_ref[...])
        h0 = xn.astype(hn_ref.dtype)
        hn_ref[0] = h0
        if len(dilations) > 1:
            for c in range(xn_ref.shape[0]):
                xn_ref[c] = xn[:, c * LANES:(c + 1) * LANES]
        done = {0}
        for t in range(parts):
            s = tile_slot[t]
            if s not in done:
                permute(s)
                done.add(s)
            part(t, h0 if s == 0 else hn_ref[s])
        for s in range(len(dilations)):
            if s not in done:
                permute(s)

    @pl.when(j > 0)
    def _():
        for t in range(parts):
            slot = _pick(j * parts + t, list(tile_slot)) if len(dilations) > 1 else 0
            part(t, hn_ref[slot])


def _norm_matmul(x, g, w, layer, *, tm, tn, dilations=(1,), tile_slot=None, slot_width=None):
    m, k = x.shape
    n_tiles = w.shape[2] // tn
    slot_width = tn if slot_width is None else slot_width
    tile_slot = (0,) * (n_tiles * tn // slot_width) if tile_slot is None else tuple(tile_slot)
    xn_shape = (k // LANES, tm, LANES) if len(dilations) > 1 else (1, 8, LANES)
    return pl.pallas_call(
        functools.partial(_norm_matmul_kernel, dilations=dilations, tile_slot=tile_slot, slot_width=slot_width),
        grid=(m // tm, n_tiles),
        in_specs=[
            pl.BlockSpec((tm, k), lambda i, j: (i, 0)),
            pl.BlockSpec((1, k), lambda i, j: (0, 0)),
            pl.BlockSpec((None, k, tn), lambda i, j: (layer, 0, j)),
        ],
        out_specs=pl.BlockSpec((tm, tn), lambda i, j: (i, j)),
        out_shape=jax.ShapeDtypeStruct((m, n_tiles * tn), _BF16),
        scratch_shapes=[pltpu.VMEM(xn_shape, _F32), pltpu.VMEM((len(dilations), tm, k), _BF16)],
        compiler_params=_cparams(("parallel", "arbitrary")),
        name="norm_matmul",
    )(x, g.reshape(1, k), w)


def _cast_tiles_kernel(w_ref, o_ref, *, transpose):
    w = w_ref[...]
    o_ref[...] = (w.T if transpose else w).astype(o_ref.dtype)


def _cast_tiles(w, layer, col_tiles, *, tn, transpose):
    k = w.shape[1]
    n = len(col_tiles)
    if transpose:
        out_spec, out_dims = pl.BlockSpec((tn, k), lambda j: (j, 0)), (n * tn, k)
    else:
        out_spec, out_dims = pl.BlockSpec((k, tn), lambda j: (0, j)), (k, n * tn)
    return pl.pallas_call(
        functools.partial(_cast_tiles_kernel, transpose=transpose),
        grid=(n,),
        in_specs=[pl.BlockSpec((None, k, tn), lambda j: (layer, 0, _pick(j, list(col_tiles))))],
        out_specs=out_spec,
        out_shape=jax.ShapeDtypeStruct(out_dims, _BF16),
        compiler_params=_cparams(("parallel",)),
        name="cast_tiles_t" if transpose else "cast_tiles",
    )(w)


def _pick(j, values):
    out = values[-1]
    for idx in range(len(values) - 2, -1, -1):
        out = jnp.where(j == idx, values[idx], out)
    return out


def _norm_matmul_dual_kernel(x_ref, g_ref, w_ref, wt_ref, nat_ref, tr_ref, hn_ref, *, nat_steps, scaled_rows, scale):
    j = pl.program_id(1)
    tn = wt_ref.shape[1]

    def natural(h):
        nat_ref[...] = jnp.dot(h, w_ref[...], preferred_element_type=_F32).astype(nat_ref.dtype)

    @pl.when(j == 0)
    def _():
        h = _rms(x_ref[...], g_ref[...]).astype(hn_ref.dtype)
        hn_ref[...] = h
        natural(h)

    @pl.when((j > 0) & (j < nat_steps))
    def _():
        natural(hn_ref[...])

    @pl.when(j >= nat_steps)
    def _():
        acc = jnp.dot(hn_ref[...], wt_ref[...], preferred_element_type=_F32)
        col = (j - nat_steps) * tn + lax.broadcasted_iota(jnp.int32, (1, tn), 1)
        tr_ref[...] = (acc * jnp.where(col < scaled_rows, scale, 1.0)).T.astype(tr_ref.dtype)


def _norm_matmul_dual(x, g, w, wt, *, tm, tn, scaled_rows, scale):
    m, k = x.shape
    nat_steps, t_steps = w.shape[1] // tn, wt.shape[1] // tn
    return pl.pallas_call(
        functools.partial(_norm_matmul_dual_kernel, nat_steps=nat_steps, scaled_rows=scaled_rows, scale=scale),
        grid=(m // tm, nat_steps + t_steps),
        in_specs=[
            pl.BlockSpec((tm, k), lambda i, j: (i, 0)),
            pl.BlockSpec((1, k), lambda i, j: (0, 0)),
            pl.BlockSpec((k, tn), lambda i, j: (0, jnp.minimum(j, nat_steps - 1))),
            pl.BlockSpec((k, tn), lambda i, j: (0, jnp.maximum(j - nat_steps, 0))),
        ],
        out_specs=[
            pl.BlockSpec((tm, tn), lambda i, j: (i, jnp.minimum(j, nat_steps - 1))),
            pl.BlockSpec((tn, tm), lambda i, j: (jnp.maximum(j - nat_steps, 0), i)),
        ],
        out_shape=[
            jax.ShapeDtypeStruct((m, w.shape[1]), _BF16),
            jax.ShapeDtypeStruct((wt.shape[1], m), _BF16),
        ],
        scratch_shapes=[pltpu.VMEM((tm, k), _BF16)],
        compiler_params=_cparams(("parallel", "arbitrary")),
        name="norm_matmul_dual",
    )(x, g.reshape(1, k), w, wt)


def _diff_attn_kernel(*refs, n_cast, tq, tk, hb, ones_rows, lam_init):
    lam_ref, sig_ref, sigp_ref, g_ref, kc_ref, qt_ref, k_ref, vt_ref = refs[:8]
    cast_in = refs[8:8 + n_cast]
    o_ref = refs[8 + n_cast]
    cast_out = refs[9 + n_cast:9 + 2 * n_cast]
    m_ref, acc_ref = refs[9 + 2 * n_cast:]
    for src, dst in zip(cast_in, cast_out):
        dst[...] = src[...].astype(dst.dtype)
    qi = pl.program_id(2)
    hd = HEAD_DIM
    half = hd // 2
    reps = 2 * tq // LANES

    lp = lam_ref[...]
    lam = (jnp.exp(jnp.sum(lp[0:1] * lp[1:2], axis=1, keepdims=True))
           - jnp.exp(jnp.sum(lp[2:3] * lp[3:4], axis=1, keepdims=True)) + lam_init)

    row = lax.broadcasted_iota(jnp.int32, (hd, tq), 0)
    qzts, sigs = [], []
    for hh in range(hb):
        qt = qt_ref[hh * hd:(hh + 1) * hd, :]
        zero = jnp.zeros_like(qt)
        top = jnp.concatenate([jnp.where(row < half, qt, zero), jnp.where(row >= half, qt, zero)], axis=1)
        ext = jnp.concatenate([sigp_ref[hh]] * reps, axis=1).astype(_BF16)
        qzts.append(jnp.concatenate([top, ext], axis=0))
        sigs.append(sig_ref[hh][:, :1])
    key_i = lax.broadcasted_iota(jnp.int32, (tq, 2 * tq), 0)
    qry_i = lax.broadcasted_iota(jnp.int32, (tq, 2 * tq), 1)
    causal = key_i <= jnp.where(qry_i >= tq, qry_i - tq, qry_i)

    m_ref[...] = jnp.full(m_ref.shape, NEG, _F32)
    acc_ref[...] = jnp.zeros(acc_ref.shape, _F32)

    def step(start, size, masked):
        off = (start - qi * tq).astype(_F32)
        kc = kc_ref[:size, :]
        ones = jnp.ones((ones_rows, size), _BF16)
        ps = []
        for hh in range(hb):
            k = jnp.concatenate([k_ref[pl.ds(start, size), hh * hd:(hh + 1) * hd], kc], axis=1)
            c = sigs[hh] * off
            s = jnp.dot(k, qzts[hh], preferred_element_type=_F32)
            if masked:
                s = jnp.where(causal, s, NEG)
            m_prev = m_ref[hh]
            m_new = jnp.maximum(m_prev, jnp.max(s, axis=0, keepdims=True) + c)
            alpha = jnp.exp2(m_prev - m_new)
            p = jnp.exp2(s - (m_new - c))
            m_ref[hh] = m_new
            ps.append((p.astype(_BF16), alpha))
        for hh in range(hb):
            p, alpha = ps[hh]
            vt = jnp.concatenate([vt_ref[hh * hd:(hh + 1) * hd, pl.ds(start, size)], ones], axis=0)
            acc_ref[hh] = acc_ref[hh] * alpha + jnp.dot(vt, p, preferred_element_type=_F32)

    nb = (qi * tq) // tk

    def body(j, carry):
        step(pl.multiple_of(2 * j * tk, tk), tk, False)
        step(pl.multiple_of((2 * j + 1) * tk, tk), tk, False)
        return carry

    lax.fori_loop(0, nb // 2, body, 0)

    @pl.when(nb % 2 == 1)
    def _():
        step(pl.multiple_of((nb - 1) * tk, tk), tk, False)
    if tk != tq:
        @pl.when(qi % (tk // tq) == 1)
        def _():
            step(pl.multiple_of((qi - 1) * tq, tq), tq, False)
    step(pl.multiple_of(qi * tq, tq), tq, True)

    for hh in range(hb):
        ot = acc_ref[hh, :hd, :] / acc_ref[hh, hd:hd + 1, :]
        odt = ot[:, :tq] - lam * ot[:, tq:]
        ms = jnp.mean(odt * odt, axis=0, keepdims=True)
        yt = odt * lax.rsqrt(ms + NORM_EPS) * g_ref[...] * (1.0 - lam_init)
        o_ref[:, hh * hd:(hh + 1) * hd] = yt.T.astype(o_ref.dtype)


def _bf16_pieces(x):
    hi = x.astype(_BF16).astype(np.float32)
    mid = (x - hi).astype(_BF16).astype(np.float32)
    lo = (x - hi - mid).astype(_BF16).astype(np.float32)
    return hi, mid, lo


def _diff_attention(qvt, knat, lam_params, subln_g, slopes, casts, *, batch, seq, n_heads, lam_init, tq, tk, hb):
    assert tk in (tq, 2 * tq)
    nq = seq // tq
    hd = HEAD_DIM
    ng = n_heads // hb
    sig = (slopes.astype(np.float64) * LOG2E).astype(np.float32)
    sig_arr = np.broadcast_to(sig.reshape(n_heads, 1, 1), (n_heads, 1, hd))
    sigp_arr = np.zeros((n_heads, hd, LANES), np.float32)
    for idx, piece in enumerate(_bf16_pieces(sig)):
        sigp_arr[:, idx, :] = piece[:, None]
        sigp_arr[:, 3 + idx, :] = piece[:, None]
    key = np.arange(tk)
    kc_arr = np.zeros((tk, LANES), np.float32)
    kc_arr[:, 0:3] = (key & 255)[:, None]
    kc_arr[:, 3:6] = (key - (key & 255))[:, None]
    ones_rows = 16
    steps = batch * ng * nq
    cast_2d = [c.reshape(-1, c.shape[-1]) for c in casts]
    assert all(c.shape[0] % (16 * steps) == 0 for c in cast_2d), "cast slabs must be whole bf16 sublane tiles"
    cast_specs = [pl.BlockSpec((c.shape[0] // steps, c.shape[1]), lambda b, h, i: ((b * ng + h) * nq + i, 0))
                  for c in cast_2d]
    kern = functools.partial(_diff_attn_kernel, n_cast=len(casts), tq=tq, tk=tk, hb=hb, ones_rows=ones_rows,
                             lam_init=lam_init)
    outs = pl.pallas_call(
        kern,
        grid=(batch, ng, nq),
        in_specs=[
            pl.BlockSpec(lam_params.shape, lambda b, h, i: (0, 0)),
            pl.BlockSpec((hb, 1, hd), lambda b, h, i: (h, 0, 0)),
            pl.BlockSpec((hb, hd, LANES), lambda b, h, i: (h, 0, 0)),
            pl.BlockSpec((hd, 1), lambda b, h, i: (0, 0)),
            pl.BlockSpec((tk, LANES), lambda b, h, i: (0, 0)),
            pl.BlockSpec((hb * hd, tq), lambda b, h, i: (h, b * nq + i)),
            pl.BlockSpec((seq, hb * hd), lambda b, h, i: (b, h)),
            pl.BlockSpec((hb * hd, seq), lambda b, h, i: (ng + h, b)),
        ] + cast_specs,
        out_specs=[pl.BlockSpec((tq, hb * hd), lambda b, h, i: (b * nq + i, h))] + cast_specs,
        out_shape=[jax.ShapeDtypeStruct((batch * seq, n_heads * hd), _BF16)]
        + [jax.ShapeDtypeStruct(c.shape, _BF16) for c in cast_2d],
        scratch_shapes=[
            pltpu.VMEM((hb, 1, 2 * tq), _F32),
            pltpu.VMEM((hb, hd + ones_rows, 2 * tq), _F32),
        ],
        compiler_params=_cparams(("parallel", "parallel", "arbitrary")),
        name="diff_attention",
    )(lam_params, jnp.asarray(sig_arr), jnp.asarray(sigp_arr), subln_g.reshape(hd, 1),
      jnp.asarray(kc_arr, dtype=_BF16), qvt, knat, qvt, *cast_2d)
    return outs[0], [o.reshape(c.shape) for o, c in zip(outs[1:], casts)]


def _rows(ref, n, st, start, size, sl):
    if len(ref.shape) == 2:
        return ref[start:start + size, sl]
    return ref[start // n:(start + size) // n, st * n:(st + 1) * n, sl].reshape(size, sl.stop - sl.start)


def _store_rows(ref, n, st, start, size, sl, val):
    if len(ref.shape) == 2:
        ref[start:start + size, sl] = val
    else:
        ref[start // n:(start + size) // n, st * n:(st + 1) * n, sl] = val.reshape(size // n, n, sl.stop - sl.start)


def _dilated_kernel(*refs, tq, span, n, streams, slopes2, scale2, has_prev):
    if has_prev:
        q_ref, kp_ref, kc_ref, vp_ref, vc_ref, o_ref, lse_ref = refs
    else:
        q_ref, kc_ref, vc_ref, o_ref, lse_ref = refs
    length = q_ref.shape[0] if len(q_ref.shape) == 2 else q_ref.shape[0] * n
    first = pl.program_id(2) == 0
    nh = len(slopes2)
    lane_grp = lax.broadcasted_iota(jnp.int32, (tq, LANES), 1) // (LANES // nh)

    def masked_bias(nk, shift, first_block):
        row = lax.broadcasted_iota(jnp.int32, (tq, nk), 0)
        col = lax.broadcasted_iota(jnp.int32, (tq, nk), 1)
        dist = row - col + shift
        valid = (dist >= 0) & (dist <= span)
        if first_block and has_prev:
            valid = valid & ((col >= span) | jnp.logical_not(first))
        distf = dist.astype(_F32)
        return [jnp.where(valid, -s2 * distf, NEG) for s2 in slopes2]

    bias_head = masked_bias(tq + span if has_prev else tq, span if has_prev else 0, True)
    bias_body = masked_bias(tq + span, span, False) if length > tq else None

    for st in range(streams):
        for qb in range(length // tq):
            u0 = qb * tq
            lses = []
            for hh in range(nh):
                sl = slice(hh * HEAD_DIM, (hh + 1) * HEAD_DIM)
                q = _rows(q_ref, n, st, u0, tq, sl)
                if qb == 0 and has_prev:
                    k = jnp.concatenate([kp_ref[:, sl], _rows(kc_ref, n, st, 0, tq, sl)], axis=0)
                    v = jnp.concatenate([vp_ref[:, sl], _rows(vc_ref, n, st, 0, tq, sl)], axis=0)
                elif qb == 0:
                    k, v = _rows(kc_ref, n, st, 0, tq, sl), _rows(vc_ref, n, st, 0, tq, sl)
                else:
                    k = _rows(kc_ref, n, st, u0 - span, tq + span, sl)
                    v = _rows(vc_ref, n, st, u0 - span, tq + span, sl)
                bias = (bias_head if qb == 0 else bias_body)[hh]
                s = lax.dot_general(q, k, _NT, preferred_element_type=_F32) * scale2 + bias
                m = jnp.max(s, axis=1, keepdims=True)
                p = jnp.exp2(s - m)
                l = jnp.sum(p, axis=1, keepdims=True)
                o = jnp.dot(p.astype(_BF16), v, preferred_element_type=_F32) / l
                _store_rows(o_ref, n, st, u0, tq, sl, o.astype(o_ref.dtype))
                lses.append(m + jnp.log2(l))
            packed = jnp.broadcast_to(lses[nh - 1], (tq, LANES))
            for hh in range(nh - 2, -1, -1):
                packed = jnp.where(lane_grp == hh, lses[hh], packed)
            _store_rows(lse_ref, n, st, u0, tq, slice(0, LANES), packed)


def _dilated_group(proj, slopes_g, *, batch, seq, group, window, dilation, in_width, mix_width):
    span = window // dilation
    assert span == LANES, "key window per stream must be one 128-row block"
    gw = HEADS_PER_GROUP * HEAD_DIM
    koff = mix_width // gw
    rows = batch * seq
    n = PERM_ROWS // dilation
    tq = span if n >= span else 2 * span
    streams = max(1, LANES // n) if dilation > 1 else 1
    kern = functools.partial(
        _dilated_kernel, tq=tq, span=span, n=n, streams=streams,
        slopes2=tuple(float(s) * dilation * LOG2E for s in slopes_g), scale2=HEAD_DIM ** -0.5 * LOG2E,
        has_prev=dilation == 1)
    if dilation == 1:
        chunk = 1024
        nc = seq // chunk
        cur = lambda c: pl.BlockSpec((chunk, gw), lambda b, r, u: (b * nc + u, c))
        prev = lambda c: pl.BlockSpec(
            (span, gw), lambda b, r, u: (b * (seq // span) + jnp.maximum(u * (chunk // span) - 1, 0), c))
        in_specs = [cur(group), prev(koff + group), cur(koff + group), prev(2 * koff + group), cur(2 * koff + group)]
        operands = [proj] * 5
        out_specs = [pl.BlockSpec((chunk, w), lambda b, r, u: (b * nc + u, 0)) for w in (gw, LANES)]
        out_dims = [(rows, gw), (rows, LANES)]
        grid = (batch, 1, nc)
    else:
        tiles = seq // PERM_ROWS
        proj3 = proj.reshape(rows // PERM_ROWS, PERM_ROWS, in_width)
        blk = lambda c, w: pl.BlockSpec((tiles, streams * n, w), lambda b, r, u: (b, r, c))
        in_specs = [blk(group, gw), blk(koff + group, gw), blk(2 * koff + group, gw)]
        operands = [proj3] * 3
        out_specs = [blk(0, gw), blk(0, LANES)]
        out_dims = [(rows // PERM_ROWS, PERM_ROWS, gw), (rows // PERM_ROWS, PERM_ROWS, LANES)]
        grid = (batch, dilation // streams, 1)
    o, lse = pl.pallas_call(
        kern,
        grid=grid,
        in_specs=in_specs,
        out_specs=out_specs,
        out_shape=[jax.ShapeDtypeStruct(out_dims[0], _BF16), jax.ShapeDtypeStruct(out_dims[1], _F32)],
        compiler_params=_cparams(("parallel", "parallel", "arbitrary")),
        name=f"dilated_attention_d{dilation}",
    )(*operands)
    return o.reshape(rows, gw), lse.reshape(rows, LANES)


def _combine_kernel(o0_ref, o1_ref, o2_ref, l0_ref, l1_ref, l2_ref, out_ref, on_ref, ln_ref, *, dilations):
    tm = out_ref.shape[0]
    gw = o0_ref.shape[1]
    nh = gw // LANES
    for g, (o_ref, l_ref, d) in enumerate(zip((o0_ref, o1_ref, o2_ref), (l0_ref, l1_ref, l2_ref), dilations)):
        n = PERM_ROWS // d
        if d == 1:
            ln_ref[g] = l_ref[...]
            for hh in range(nh):
                on_ref[g, hh] = o_ref[:, hh * LANES:(hh + 1) * LANES].astype(_F32)
            continue
        for grp in range(tm // PERM_ROWS):
            for r in range(d):
                src = slice(grp * PERM_ROWS + r * n, grp * PERM_ROWS + (r + 1) * n)
                dst = pl.ds(grp * PERM_ROWS + r, n, stride=d)
                ln_ref[g, dst, :] = l_ref[src, :]
                for hh in range(nh):
                    on_ref[g, hh, dst, :] = o_ref[src, hh * LANES:(hh + 1) * LANES].astype(_F32)
    l0, l1, l2 = ln_ref[0], ln_ref[1], ln_ref[2]
    mx = jnp.maximum(jnp.maximum(l0, l1), l2)
    es = (jnp.exp2(l0 - mx), jnp.exp2(l1 - mx), jnp.exp2(l2 - mx))
    den = es[0] + es[1] + es[2]
    for g in range(3):
        wg = es[g] / den
        for hh in range(nh):
            lane0 = hh * (LANES // nh)
            col = g * gw + hh * LANES
            out_ref[:, col:col + LANES] = (on_ref[g, hh] * wg[:, lane0:lane0 + 1]).astype(out_ref.dtype)


def _combine_groups(outs, lses, dilations, *, tm):
    m, gw = outs[0].shape
    spec = pl.BlockSpec((tm, gw), lambda i: (i, 0))
    return pl.pallas_call(
        functools.partial(_combine_kernel, dilations=dilations),
        grid=(m // tm,),
        in_specs=[spec] * 3 + [pl.BlockSpec((tm, LANES), lambda i: (i, 0))] * 3,
        out_specs=pl.BlockSpec((tm, 3 * gw), lambda i: (i, 0)),
        out_shape=jax.ShapeDtypeStruct((m, 3 * gw), _BF16),
        scratch_shapes=[pltpu.VMEM((3, gw // LANES, tm, LANES), _F32), pltpu.VMEM((3, tm, LANES), _F32)],
        compiler_params=_cparams(("parallel",)),
        name="combine_groups",
    )(*outs, *lses)


def _mem_attn_kernel(q_ref, k_ref, v_ref, o_ref, *, scale):
    for hh in range(N_MEM_HEADS):
        sl = slice(hh * HEAD_DIM, (hh + 1) * HEAD_DIM)
        s = lax.dot_general(q_ref[:, sl], k_ref[:, sl], _NT, preferred_element_type=_F32) * scale
        m = jnp.max(s, axis=1, keepdims=True)
        p = jnp.exp(s - m)
        l = jnp.sum(p, axis=1, keepdims=True)
        o = jnp.dot(p.astype(_BF16), v_ref[:, sl], preferred_element_type=_F32)
        o_ref[:, sl] = (o / l).astype(o_ref.dtype)


def _memory_attention(proj, kvm, *, batch, seq, mem_tokens, q_col_block, tq):
    nq = seq // tq
    mw = N_MEM_HEADS * HEAD_DIM
    return pl.pallas_call(
        functools.partial(_mem_attn_kernel, scale=HEAD_DIM ** -0.5),
        grid=(batch, nq),
        in_specs=[
            pl.BlockSpec((tq, mw), lambda b, i: (b * nq + i, q_col_block)),
            pl.BlockSpec((mem_tokens, mw), lambda b, i: (b, 0)),
            pl.BlockSpec((mem_tokens, mw), lambda b, i: (b, 1)),
        ],
        out_specs=pl.BlockSpec((tq, mw), lambda b, i: (b * nq + i, 0)),
        out_shape=jax.ShapeDtypeStruct((batch * seq, mw), _BF16),
        compiler_params=_cparams(("parallel", "parallel")),
        name="memory_attention",
    )(proj, kvm, kvm)


def _out_proj_kernel(x_ref, a_ref, b_ref, wa_ref, wb_ref, o_ref):
    acc = jnp.dot(a_ref[...], wa_ref[...], preferred_element_type=_F32)
    acc = acc + jnp.dot(b_ref[...], wb_ref[...], preferred_element_type=_F32)
    o_ref[...] = x_ref[...] + acc


def _out_proj(x, o_mix, o_mem, w, layer, *, tm):
    m, d = x.shape
    ka, kb = o_mix.shape[1], o_mem.shape[1]
    assert ka % kb == 0
    return pl.pallas_call(
        _out_proj_kernel,
        grid=(m // tm,),
        in_specs=[
            pl.BlockSpec((tm, d), lambda i: (i, 0)),
            pl.BlockSpec((tm, ka), lambda i: (i, 0)),
            pl.BlockSpec((tm, kb), lambda i: (i, 0)),
            pl.BlockSpec((None, ka, d), lambda i: (layer, 0, 0)),
            pl.BlockSpec((None, kb, d), lambda i: (layer, ka // kb, 0)),
        ],
        out_specs=pl.BlockSpec((tm, d), lambda i: (i, 0)),
        out_shape=jax.ShapeDtypeStruct((m, d), _F32),
        compiler_params=_cparams(("parallel",)),
        name="out_proj",
    )(x, o_mix, o_mem, w, w)


def _mlp_kernel(x_ref, g_ref, w1_ref, w2_ref, gf_ref, o_ref, hn_ref, *, final_norm):
    f = pl.program_id(1)

    def ffn(h):
        a = jnp.dot(h, w1_ref[...], preferred_element_type=_F32)
        a = jnp.square(jnp.maximum(a, 0.0)).astype(_BF16)
        return jnp.dot(a, w2_ref[...], preferred_element_type=_F32)

    @pl.when(f == 0)
    def _():
        x = x_ref[...]
        h = _rms(x, g_ref[...]).astype(hn_ref.dtype)
        hn_ref[...] = h
        o_ref[...] = x + ffn(h)

    @pl.when(f > 0)
    def _():
        o_ref[...] += ffn(hn_ref[...])

    if final_norm:
        @pl.when(f == pl.num_programs(1) - 1)
        def _():
            o_ref[...] = _rms(o_ref[...], gf_ref[...])


def _mlp(x, g, w1, w2, layer, g_final, *, final_norm, tm, tf):
    m, d = x.shape
    ff = w1.shape[2]
    return pl.pallas_call(
        functools.partial(_mlp_kernel, final_norm=final_norm),
        grid=(m // tm, ff // tf),
        in_specs=[
            pl.BlockSpec((tm, d), lambda i, f: (i, 0)),
            pl.BlockSpec((1, d), lambda i, f: (0, 0)),
            pl.BlockSpec((None, d, tf), lambda i, f: (layer, 0, f)),
            pl.BlockSpec((None, tf, d), lambda i, f: (layer, f, 0)),
            pl.BlockSpec((1, d), lambda i, f: (0, 0)),
        ],
        out_specs=pl.BlockSpec((tm, d), lambda i, f: (i, 0)),
        out_shape=jax.ShapeDtypeStruct((m, d), _F32),
        scratch_shapes=[pltpu.VMEM((tm, d), _BF16)],
        compiler_params=_cparams(("parallel", "arbitrary")),
        name="mlp",
    )(x, g.reshape(1, d), w1, w2, g_final.reshape(1, d))


def kernel(x, mem, g_attn, w_in, w_out, lambda_qk, diff_subln_g, g_mem, w_mem_kv, g_mlp, w_mlp1, w_mlp2, g_final):
    batch, seq, d_model = x.shape
    depth = w_in.shape[0]
    mem_tokens = mem.shape[1]
    in_width = w_in.shape[2]
    mem_width = N_MEM_HEADS * HEAD_DIM
    mix_width = (in_width - mem_width) // 3
    n_heads = mix_width // HEAD_DIM
    slopes = _alibi_slopes(n_heads)
    dilations = tuple(d for _, d in DILATED_GROUPS)
    gw = HEADS_PER_GROUP * HEAD_DIM
    n_qkv_tiles = 3 * mix_width // gw

    xf = x.reshape(batch * seq, d_model)
    memf = mem.reshape(batch * mem_tokens, d_model)
    mix_tiles = mix_width // gw
    q_scale = LOG2E * (HEAD_DIM // 2) ** -0.5
    for i in range(depth):
        if i % N_MIXERS == 0:
            j = i // N_MIXERS
            lam_init = 0.8 - 0.6 * math.exp(-0.3 * i)
            k_qm_tiles = [mix_tiles + t for t in range(mix_tiles)] + [3 * mix_tiles]
            q_v_tiles = list(range(mix_tiles)) + [2 * mix_tiles + t for t in range(mix_tiles)]
            proj, qvt = _norm_matmul_dual(
                xf, g_attn[i], _cast_tiles(w_in, i, k_qm_tiles, tn=gw, transpose=False),
                _cast_tiles(w_in, i, q_v_tiles, tn=gw, transpose=False),
                tm=1024, tn=2 * gw, scaled_rows=mix_width, scale=q_scale)
            casts = [w_mem_kv, w_out, w_mlp1, w_mlp2, w_in] if i == 0 else []
            o_mix, casted = _diff_attention(qvt, proj, lambda_qk[j], diff_subln_g[j], slopes, casts, batch=batch,
                                            seq=seq, n_heads=n_heads, lam_init=lam_init, tq=256, tk=512, hb=6)
            if i == 0:
                w_kv_b, w_out_b, w1_b, w2_b, w_in_b = casted
            qm_col_block = mix_tiles
        else:
            tile_slot = [t % len(dilations) if t < n_qkv_tiles else 0 for t in range(in_width // gw)]
            proj = _norm_matmul(xf, g_attn[i], w_in_b, i, tm=1024, tn=2 * gw, dilations=dilations,
                                tile_slot=tile_slot, slot_width=gw)
            outs, lses = [], []
            for g, (window, dilation) in enumerate(DILATED_GROUPS):
                sl = slice(g * HEADS_PER_GROUP, (g + 1) * HEADS_PER_GROUP)
                o, lse = _dilated_group(proj, slopes[sl], batch=batch, seq=seq, group=g, window=window,
                                        dilation=dilation, in_width=in_width, mix_width=mix_width)
                outs.append(o)
                lses.append(lse)
            o_mix = _combine_groups(outs, lses, dilations, tm=PERM_ROWS)
            qm_col_block = 3 * mix_tiles
        kvm = _norm_matmul(memf, g_mem, w_kv_b, i, tm=512, tn=512)
        o_mem = _memory_attention(proj, kvm, batch=batch, seq=seq, mem_tokens=mem_tokens,
                                  q_col_block=qm_col_block, tq=1024)
        xf = _out_proj(xf, o_mix, o_mem, w_out_b, i, tm=512)
        xf = _mlp(xf, g_mlp[i], w1_b, w2_b, i, g_final, final_norm=(i == depth - 1), tm=1024, tf=512)
    return xf.reshape(batch, seq, d_model)
```

```python
import functools
import math

import numpy as np
import jax
import jax.numpy as jnp
from jax import lax
from jax.experimental import pallas as pl
from jax.experimental.pallas import tpu as pltpu

HEAD_DIM = 128
N_MEM_HEADS = 4
N_MIXERS = 2
DILATED_GROUPS = ((128, 1), (512, 4), (2048, 16))
HEADS_PER_GROUP = 4
NORM_EPS = 1e-6
NEG = -1e30
LANES = 128
VMEM_LIMIT = 56 * 1024 * 1024
PERM_ROWS = 512
LOG2E = math.log2(math.e)

_F32 = jnp.float32
_BF16 = jnp.bfloat16
_NT = (((1,), (1,)), ((), ()))


def _alibi_slopes(n):
    def pow2(m):
        start = 2.0 ** (-(2.0 ** -(math.log2(m) - 3)))
        return [start * start ** i for i in range(m)]

    def slopes(m):
        if math.log2(m).is_integer():
            return pow2(m)
        c = 2 ** math.floor(math.log2(m))
        return pow2(c) + slopes(2 * c)[0::2][: m - c]

    return np.asarray(sorted(slopes(n), reverse=True), dtype=np.float32)


def _cparams(semantics):
    return pltpu.CompilerParams(dimension_semantics=semantics, vmem_limit_bytes=VMEM_LIMIT)


def _rms(x, g):
    ms = jnp.mean(x * x, axis=-1, keepdims=True)
    return x * lax.rsqrt(ms + NORM_EPS) * g


def _norm_matmul_kernel(x_ref, g_ref, w_ref, o_ref, xn_ref, hn_ref, *, dilations, tile_slot, slot_width):
    j = pl.program_id(1)
    tm = x_ref.shape[0]
    parts = o_ref.shape[1] // slot_width

    def part(t, h):
        cols = slice(t * slot_width, (t + 1) * slot_width)
        o_ref[:, cols] = jnp.dot(h, w_ref[:, cols], preferred_element_type=_F32).astype(o_ref.dtype)

    def permute(s):
        d = dilations[s]
        n = PERM_ROWS // d
        for c in range(xn_ref.shape[0]):
            for grp in range(tm // PERM_ROWS):
                for r in range(d):
                    rows = xn_ref[c, pl.ds(grp * PERM_ROWS + r, n, stride=d), :]
                    dst = grp * PERM_ROWS + r * n
                    hn_ref[s, dst:dst + n, c * LANES:(c + 1) * LANES] = rows.astype(hn_ref.dtype)

    @pl.when(j == 0)
    def _():
        xn = _rms(x_ref[...], g_ref[...])
        h0 = xn.astype(hn_ref.dtype)
        hn_ref[0] = h0
        if len(dilations) > 1:
            for c in range(xn_ref.shape[0]):
                xn_ref[c] = xn[:, c * LANES:(c + 1) * LANES]
        done = {0}
        for t in range(parts):
            s = tile_slot[t]
            if s not in done:
                permute(s)
                done.add(s)
            part(t, h0 if s == 0 else hn_ref[s])
        for s in range(len(dilations)):
            if s not in done:
                permute(s)

    @pl.when(j > 0)
    def _():
        for t in range(parts):
            slot = _pick(j * parts + t, list(tile_slot)) if len(dilations) > 1 else 0
            part(t, hn_ref[slot])


def _norm_matmul(x, g, w, layer, *, tm, tn, dilations=(1,), tile_slot=None, slot_width=None):
    m, k = x.shape
    n_tiles = w.shape[2] // tn
    slot_width = tn if slot_width is None else slot_width
    tile_slot = (0,) * (n_tiles * tn // slot_width) if tile_slot is None else tuple(tile_slot)
    xn_shape = (k // LANES, tm, LANES) if len(dilations) > 1 else (1, 8, LANES)
    return pl.pallas_call(
        functools.partial(_norm_matmul_kernel, dilations=dilations, tile_slot=tile_slot, slot_width=slot_width),
        grid=(m // tm, n_tiles),
        in_specs=[
            pl.BlockSpec((tm, k), lambda i, j: (i, 0)),
            pl.BlockSpec((1, k), lambda i, j: (0, 0)),
            pl.BlockSpec((None, k, tn), lambda i, j: (layer, 0, j)),
        ],
        out_specs=pl.BlockSpec((tm, tn), lambda i, j: (i, j)),
        out_shape=jax.ShapeDtypeStruct((m, n_tiles * tn), _BF16),
        scratch_shapes=[pltpu.VMEM(xn_shape, _F32), pltpu.VMEM((len(dilations), tm, k), _BF16)],
        compiler_params=_cparams(("parallel", "arbitrary")),
        name="norm_matmul",
    )(x, g.reshape(1, k), w)


def _cast_tiles_kernel(w_ref, o_ref, *, transpose):
    w = w_ref[...]
    o_ref[...] = (w.T if transpose else w).astype(o_ref.dtype)


def _cast_tiles(w, layer, col_tiles, *, tn, transpose):
    k = w.shape[1]
    n = len(col_tiles)
    if transpose:
        out_spec, out_dims = pl.BlockSpec((tn, k), lambda j: (j, 0)), (n * tn, k)
    else:
        out_spec, out_dims = pl.BlockSpec((k, tn), lambda j: (0, j)), (k, n * tn)
    return pl.pallas_call(
        functools.partial(_cast_tiles_kernel, transpose=transpose),
        grid=(n,),
        in_specs=[pl.BlockSpec((None, k, tn), lambda j: (layer, 0, _pick(j, list(col_tiles))))],
        out_specs=out_spec,
        out_shape=jax.ShapeDtypeStruct(out_dims, _BF16),
        compiler_params=_cparams(("parallel",)),
        name="cast_tiles_t" if transpose else "cast_tiles",
    )(w)


def _pick(j, values):
    out = values[-1]
    for idx in range(len(values) - 2, -1, -1):
        out = jnp.where(j == idx, values[idx], out)
    return out


def _norm_matmul_dual_kernel(x_ref, g_ref, w_ref, wt_ref, nat_ref, tr_ref, hn_ref, *, nat_steps, scaled_rows, scale):
    j = pl.program_id(1)
    tn = wt_ref.shape[1]

    def natural(h):
        nat_ref[...] = jnp.dot(h, w_ref[...], preferred_element_type=_F32).astype(nat_ref.dtype)

    @pl.when(j == 0)
    def _():
        h = _rms(x_ref[...], g_ref[...]).astype(hn_ref.dtype)
        hn_ref[...] = h
        natural(h)

    @pl.when((j > 0) & (j < nat_steps))
    def _():
        natural(hn_ref[...])

    @pl.when(j >= nat_steps)
    def _():
        acc = jnp.dot(hn_ref[...], wt_ref[...], preferred_element_type=_F32)
        col = (j - nat_steps) * tn + lax.broadcasted_iota(jnp.int32, (1, tn), 1)
        tr_ref[...] = (acc * jnp.where(col < scaled_rows, scale, 1.0)).T.astype(tr_ref.dtype)


def _norm_matmul_dual(x, g, w, wt, *, tm, tn, scaled_rows, scale):
    m, k = x.shape
    nat_steps, t_steps = w.shape[1] // tn, wt.shape[1] // tn
    return pl.pallas_call(
        functools.partial(_norm_matmul_dual_kernel, nat_steps=nat_steps, scaled_rows=scaled_rows, scale=scale),
        grid=(m // tm, nat_steps + t_steps),
        in_specs=[
            pl.BlockSpec((tm, k), lambda i, j: (i, 0)),
            pl.BlockSpec((1, k), lambda i, j: (0, 0)),
            pl.BlockSpec((k, tn), lambda i, j: (0, jnp.minimum(j, nat_steps - 1))),
            pl.BlockSpec((k, tn), lambda i, j: (0, jnp.maximum(j - nat_steps, 0))),
        ],
        out_specs=[
            pl.BlockSpec((tm, tn), lambda i, j: (i, jnp.minimum(j, nat_steps - 1))),
            pl.BlockSpec((None, tn, tm), lambda i, j: (i, jnp.maximum(j - nat_steps, 0), 0)),
        ],
        out_shape=[
            jax.ShapeDtypeStruct((m, w.shape[1]), _BF16),
            jax.ShapeDtypeStruct((m // tm, wt.shape[1], tm), _BF16),
        ],
        scratch_shapes=[pltpu.VMEM((tm, k), _BF16)],
        compiler_params=_cparams(("parallel", "arbitrary")),
        name="norm_matmul_dual",
    )(x, g.reshape(1, k), w, wt)


def _diff_attn_kernel(*refs, n_cast, tq, tk, hb, ones_rows, lam_init):
    lam_ref, sig_ref, sigp_ref, g_ref, kc_ref, qt_ref, k_ref, vt_ref = refs[:8]
    cast_in = refs[8:8 + n_cast]
    o_ref = refs[8 + n_cast]
    cast_out = refs[9 + n_cast:9 + 2 * n_cast]
    m_ref, acc_ref = refs[9 + 2 * n_cast:]
    for src, dst in zip(cast_in, cast_out):
        dst[...] = src[...].astype(dst.dtype)
    qi = pl.program_id(2)
    hd = HEAD_DIM
    half = hd // 2
    reps = 2 * tq // LANES

    lp = lam_ref[...]
    lam = (jnp.exp(jnp.sum(lp[0:1] * lp[1:2], axis=1, keepdims=True))
           - jnp.exp(jnp.sum(lp[2:3] * lp[3:4], axis=1, keepdims=True)) + lam_init)

    row = lax.broadcasted_iota(jnp.int32, (hd, tq), 0)
    qzts, sigs = [], []
    for hh in range(hb):
        qt = qt_ref[hh * hd:(hh + 1) * hd, :]
        zero = jnp.zeros_like(qt)
        top = jnp.concatenate([jnp.where(row < half, qt, zero), jnp.where(row >= half, qt, zero)], axis=1)
        ext = jnp.concatenate([sigp_ref[hh]] * reps, axis=1).astype(_BF16)
        qzts.append(jnp.concatenate([top, ext], axis=0))
        sigs.append(sig_ref[hh][:, :1])
    key_i = lax.broadcasted_iota(jnp.int32, (tq, 2 * tq), 0)
    qry_i = lax.broadcasted_iota(jnp.int32, (tq, 2 * tq), 1)
    causal = key_i <= jnp.where(qry_i >= tq, qry_i - tq, qry_i)

    m_ref[...] = jnp.full(m_ref.shape, NEG, _F32)
    acc_ref[...] = jnp.zeros(acc_ref.shape, _F32)

    def step(start, size, masked):
        off = (start - qi * tq).astype(_F32)
        kc = kc_ref[:size, :]
        ones = jnp.ones((ones_rows, size), _BF16)
        ps = []
        for hh in range(hb):
            k = jnp.concatenate([k_ref[pl.ds(start, size), hh * hd:(hh + 1) * hd], kc], axis=1)
            c = sigs[hh] * off
            s = jnp.dot(k, qzts[hh], preferred_element_type=_F32)
            if masked:
                s = jnp.where(causal, s, NEG)
            m_prev = m_ref[hh]
            m_new = jnp.maximum(m_prev, jnp.max(s, axis=0, keepdims=True) + c)
            alpha = jnp.exp2(m_prev - m_new)
            p = jnp.exp2(s - (m_new - c))
            m_ref[hh] = m_new
            ps.append((p.astype(_BF16), alpha))
        for hh in range(hb):
            p, alpha = ps[hh]
            tile, col = start // vt_ref.shape[2], pl.multiple_of(start % vt_ref.shape[2], tq)
            vt = jnp.concatenate([vt_ref[tile, hh * hd:(hh + 1) * hd, pl.ds(col, size)], ones], axis=0)
            acc_ref[hh] = acc_ref[hh] * alpha + jnp.dot(vt, p, preferred_element_type=_F32)

    nb = (qi * tq) // tk

    def body(j, carry):
        step(pl.multiple_of(2 * j * tk, tk), tk, False)
        step(pl.multiple_of((2 * j + 1) * tk, tk), tk, False)
        return carry

    lax.fori_loop(0, nb // 2, body, 0)

    @pl.when(nb % 2 == 1)
    def _():
        step(pl.multiple_of((nb - 1) * tk, tk), tk, False)
    if tk != tq:
        @pl.when(qi % (tk // tq) == 1)
        def _():
            step(pl.multiple_of((qi - 1) * tq, tq), tq, False)
    step(pl.multiple_of(qi * tq, tq), tq, True)

    for hh in range(hb):
        ot = acc_ref[hh, :hd, :] / acc_ref[hh, hd:hd + 1, :]
        odt = ot[:, :tq] - lam * ot[:, tq:]
        ms = jnp.mean(odt * odt, axis=0, keepdims=True)
        yt = odt * lax.rsqrt(ms + NORM_EPS) * g_ref[...] * (1.0 - lam_init)
        o_ref[:, hh * hd:(hh + 1) * hd] = yt.T.astype(o_ref.dtype)


def _bf16_pieces(x):
    hi = x.astype(_BF16).astype(np.float32)
    mid = (x - hi).astype(_BF16).astype(np.float32)
    lo = (x - hi - mid).astype(_BF16).astype(np.float32)
    return hi, mid, lo


def _diff_attention(qvt, knat, lam_params, subln_g, slopes, casts, *, batch, seq, n_heads, lam_init, tq, tk, hb):
    assert tk in (tq, 2 * tq)
    nq = seq // tq
    hd = HEAD_DIM
    ng = n_heads // hb
    tmt = qvt.shape[2]
    tpb, qpt = seq // tmt, tmt // tq
    assert seq % tmt == 0 and tmt % tk == 0
    sig = (slopes.astype(np.float64) * LOG2E).astype(np.float32)
    sig_arr = np.broadcast_to(sig.reshape(n_heads, 1, 1), (n_heads, 1, hd))
    sigp_arr = np.zeros((n_heads, hd, LANES), np.float32)
    for idx, piece in enumerate(_bf16_pieces(sig)):
        sigp_arr[:, idx, :] = piece[:, None]
        sigp_arr[:, 3 + idx, :] = piece[:, None]
    key = np.arange(tk)
    kc_arr = np.zeros((tk, LANES), np.float32)
    kc_arr[:, 0:3] = (key & 255)[:, None]
    kc_arr[:, 3:6] = (key - (key & 255))[:, None]
    ones_rows = 16
    steps = batch * ng * nq
    cast_2d = [c.reshape(-1, c.shape[-1]) for c in casts]
    assert all(c.shape[0] % (16 * steps) == 0 for c in cast_2d), "cast slabs must be whole bf16 sublane tiles"
    cast_specs = [pl.BlockSpec((c.shape[0] // steps, c.shape[1]), lambda b, h, i: ((b * ng + h) * nq + i, 0))
                  for c in cast_2d]
    kern = functools.partial(_diff_attn_kernel, n_cast=len(casts), tq=tq, tk=tk, hb=hb, ones_rows=ones_rows,
                             lam_init=lam_init)
    outs = pl.pallas_call(
        kern,
        grid=(batch, ng, nq),
        in_specs=[
            pl.BlockSpec(lam_params.shape, lambda b, h, i: (0, 0)),
            pl.BlockSpec((hb, 1, hd), lambda b, h, i: (h, 0, 0)),
            pl.BlockSpec((hb, hd, LANES), lambda b, h, i: (h, 0, 0)),
            pl.BlockSpec((hd, 1), lambda b, h, i: (0, 0)),
            pl.BlockSpec((tk, LANES), lambda b, h, i: (0, 0)),
            pl.BlockSpec((None, hb * hd, tq), lambda b, h, i: (b * tpb + i // qpt, h, i % qpt)),
            pl.BlockSpec((seq, hb * hd), lambda b, h, i: (b, h)),
            pl.BlockSpec((tpb, hb * hd, tmt), lambda b, h, i: (b, ng + h, 0)),
        ] + cast_specs,
        out_specs=[pl.BlockSpec((tq, hb * hd), lambda b, h, i: (b * nq + i, h))] + cast_specs,
        out_shape=[jax.ShapeDtypeStruct((batch * seq, n_heads * hd), _BF16)]
        + [jax.ShapeDtypeStruct(c.shape, _BF16) for c in cast_2d],
        scratch_shapes=[
            pltpu.VMEM((hb, 1, 2 * tq), _F32),
            pltpu.VMEM((hb, hd + ones_rows, 2 * tq), _F32),
        ],
        compiler_params=_cparams(("parallel", "parallel", "arbitrary")),
        name="diff_attention",
    )(lam_params, jnp.asarray(sig_arr), jnp.asarray(sigp_arr), subln_g.reshape(hd, 1),
      jnp.asarray(kc_arr, dtype=_BF16), qvt, knat, qvt, *cast_2d)
    return outs[0], [o.reshape(c.shape) for o, c in zip(outs[1:], casts)]


def _rows(ref, n, st, start, size, sl):
    if len(ref.shape) == 2:
        return ref[start:start + size, sl]
    return ref[start // n:(start + size) // n, st * n:(st + 1) * n, sl].reshape(size, sl.stop - sl.start)


def _store_rows(ref, n, st, start, size, sl, val):
    if len(ref.shape) == 2:
        ref[start:start + size, sl] = val
    else:
        ref[start // n:(start + size) // n, st * n:(st + 1) * n, sl] = val.reshape(size // n, n, sl.stop - sl.start)


def _dilated_kernel(*refs, tq, span, n, streams, slopes2, scale2, has_prev):
    if has_prev:
        q_ref, kp_ref, kc_ref, vp_ref, vc_ref, o_ref, lse_ref = refs
    else:
        q_ref, kc_ref, vc_ref, o_ref, lse_ref = refs
    length = q_ref.shape[0] if len(q_ref.shape) == 2 else q_ref.shape[0] * n
    first = pl.program_id(2) == 0
    nh = len(slopes2)
    lane_grp = lax.broadcasted_iota(jnp.int32, (tq, LANES), 1) // (LANES // nh)

    def masked_bias(nk, shift, first_block):
        row = lax.broadcasted_iota(jnp.int32, (tq, nk), 0)
        col = lax.broadcasted_iota(jnp.int32, (tq, nk), 1)
        dist = row - col + shift
        valid = (dist >= 0) & (dist <= span)
        if first_block and has_prev:
            valid = valid & ((col >= span) | jnp.logical_not(first))
        distf = dist.astype(_F32)
        return [jnp.where(valid, -s2 * distf, NEG) for s2 in slopes2]

    bias_head = masked_bias(tq + span if has_prev else tq, span if has_prev else 0, True)
    bias_body = masked_bias(tq + span, span, False) if length > tq else None

    for st in range(streams):
        for qb in range(length // tq):
            u0 = qb * tq
            lses = []
            for hh in range(nh):
                sl = slice(hh * HEAD_DIM, (hh + 1) * HEAD_DIM)
                q = _rows(q_ref, n, st, u0, tq, sl)
                if qb == 0 and has_prev:
                    k = jnp.concatenate([kp_ref[:, sl], _rows(kc_ref, n, st, 0, tq, sl)], axis=0)
                    v = jnp.concatenate([vp_ref[:, sl], _rows(vc_ref, n, st, 0, tq, sl)], axis=0)
                elif qb == 0:
                    k, v = _rows(kc_ref, n, st, 0, tq, sl), _rows(vc_ref, n, st, 0, tq, sl)
                else:
                    k = _rows(kc_ref, n, st, u0 - span, tq + span, sl)
                    v = _rows(vc_ref, n, st, u0 - span, tq + span, sl)
                bias = (bias_head if qb == 0 else bias_body)[hh]
                s = lax.dot_general(q, k, _NT, preferred_element_type=_F32) * scale2 + bias
                m = jnp.max(s, axis=1, keepdims=True)
                p = jnp.exp2(s - m)
                l = jnp.sum(p, axis=1, keepdims=True)
                o = jnp.dot(p.astype(_BF16), v, preferred_element_type=_F32) / l
                _store_rows(o_ref, n, st, u0, tq, sl, o.astype(o_ref.dtype))
                lses.append(m + jnp.log2(l))
            packed = jnp.broadcast_to(lses[nh - 1], (tq, LANES))
            for hh in range(nh - 2, -1, -1):
                packed = jnp.where(lane_grp == hh, lses[hh], packed)
            _store_rows(lse_ref, n, st, u0, tq, slice(0, LANES), packed)


def _dilated_group(proj, slopes_g, *, batch, seq, group, window, dilation, in_width, mix_width):
    span = window // dilation
    assert span == LANES, "key window per stream must be one 128-row block"
    gw = HEADS_PER_GROUP * HEAD_DIM
    koff = mix_width // gw
    rows = batch * seq
    n = PERM_ROWS // dilation
    tq = span if n >= span else 2 * span
    streams = max(1, LANES // n) if dilation > 1 else 1
    kern = functools.partial(
        _dilated_kernel, tq=tq, span=span, n=n, streams=streams,
        slopes2=tuple(float(s) * dilation * LOG2E for s in slopes_g), scale2=HEAD_DIM ** -0.5 * LOG2E,
        has_prev=dilation == 1)
    if dilation == 1:
        chunk = 1024
        nc = seq // chunk
        cur = lambda c: pl.BlockSpec((chunk, gw), lambda b, r, u: (b * nc + u, c))
        prev = lambda c: pl.BlockSpec(
            (span, gw), lambda b, r, u: (b * (seq // span) + jnp.maximum(u * (chunk // span) - 1, 0), c))
        in_specs = [cur(group), prev(koff + group), cur(koff + group), prev(2 * koff + group), cur(2 * koff + group)]
        operands = [proj] * 5
        out_specs = [pl.BlockSpec((chunk, w), lambda b, r, u: (b * nc + u, 0)) for w in (gw, LANES)]
        out_dims = [(rows, gw), (rows, LANES)]
        grid = (batch, 1, nc)
    else:
        tiles = seq // PERM_ROWS
        proj3 = proj.reshape(rows // PERM_ROWS, PERM_ROWS, in_width)
        blk = lambda c, w: pl.BlockSpec((tiles, streams * n, w), lambda b, r, u: (b, r, c))
        in_specs = [blk(group, gw), blk(koff + group, gw), blk(2 * koff + group, gw)]
        operands = [proj3] * 3
        out_specs = [blk(0, gw), blk(0, LANES)]
        out_dims = [(rows // PERM_ROWS, PERM_ROWS, gw), (rows // PERM_ROWS, PERM_ROWS, LANES)]
        grid = (batch, dilation // streams, 1)
    o, lse = pl.pallas_call(
        kern,
        grid=grid,
        in_specs=in_specs,
        out_specs=out_specs,
        out_shape=[jax.ShapeDtypeStruct(out_dims[0], _BF16), jax.ShapeDtypeStruct(out_dims[1], _F32)],
        compiler_params=_cparams(("parallel", "parallel", "arbitrary")),
        name=f"dilated_attention_d{dilation}",
    )(*operands)
    return o.reshape(rows, gw), lse.reshape(rows, LANES)


def _combine_kernel(o0_ref, o1_ref, o2_ref, l0_ref, l1_ref, l2_ref, out_ref, on_ref, ln_ref, *, dilations):
    tm = out_ref.shape[0]
    gw = o0_ref.shape[1]
    nh = gw // LANES
    for g, (o_ref, l_ref, d) in enumerate(zip((o0_ref, o1_ref, o2_ref), (l0_ref, l1_ref, l2_ref), dilations)):
        n = PERM_ROWS // d
        if d == 1:
            ln_ref[g] = l_ref[...]
            for hh in range(nh):
                on_ref[g, hh] = o_ref[:, hh * LANES:(hh + 1) * LANES].astype(_F32)
            continue
        for grp in range(tm // PERM_ROWS):
            for r in range(d):
                src = slice(grp * PERM_ROWS + r * n, grp * PERM_ROWS + (r + 1) * n)
                dst = pl.ds(grp * PERM_ROWS + r, n, stride=d)
                ln_ref[g, dst, :] = l_ref[src, :]
                for hh in range(nh):
                    on_ref[g, hh, dst, :] = o_ref[src, hh * LANES:(hh + 1) * LANES].astype(_F32)
    l0, l1, l2 = ln_ref[0], ln_ref[1], ln_ref[2]
    mx = jnp.maximum(jnp.maximum(l0, l1), l2)
    es = (jnp.exp2(l0 - mx), jnp.exp2(l1 - mx), jnp.exp2(l2 - mx))
    den = es[0] + es[1] + es[2]
    for g in range(3):
        wg = es[g] / den
        for hh in range(nh):
            lane0 = hh * (LANES // nh)
            col = g * gw + hh * LANES
            out_ref[:, col:col + LANES] = (on_ref[g, hh] * wg[:, lane0:lane0 + 1]).astype(out_ref.dtype)


def _combine_groups(outs, lses, dilations, *, tm):
    m, gw = outs[0].shape
    spec = pl.BlockSpec((tm, gw), lambda i: (i, 0))
    return pl.pallas_call(
        functools.partial(_combine_kernel, dilations=dilations),
        grid=(m // tm,),
        in_specs=[spec] * 3 + [pl.BlockSpec((tm, LANES), lambda i: (i, 0))] * 3,
        out_specs=pl.BlockSpec((tm, 3 * gw), lambda i: (i, 0)),
        out_shape=jax.ShapeDtypeStruct((m, 3 * gw), _BF16),
        scratch_shapes=[pltpu.VMEM((3, gw // LANES, tm, LANES), _F32), pltpu.VMEM((3, tm, LANES), _F32)],
        compiler_params=_cparams(("parallel",)),
        name="combine_groups",
    )(*outs, *lses)


def _mem_attn_kernel(q_ref, k_ref, v_ref, o_ref, *, scale):
    for hh in range(N_MEM_HEADS):
        sl = slice(hh * HEAD_DIM, (hh + 1) * HEAD_DIM)
        s = lax.dot_general(q_ref[:, sl], k_ref[:, sl], _NT, preferred_element_type=_F32) * scale
        m = jnp.max(s, axis=1, keepdims=True)
        p = jnp.exp(s - m)
        l = jnp.sum(p, axis=1, keepdims=True)
        o = jnp.dot(p.astype(_BF16), v_ref[:, sl], preferred_element_type=_F32)
        o_ref[:, sl] = (o / l).astype(o_ref.dtype)


def _memory_attention(proj, kvm, *, batch, seq, mem_tokens, q_col_block, tq):
    nq = seq // tq
    mw = N_MEM_HEADS * HEAD_DIM
    return pl.pallas_call(
        functools.partial(_mem_attn_kernel, scale=HEAD_DIM ** -0.5),
        grid=(batch, nq),
        in_specs=[
            pl.BlockSpec((tq, mw), lambda b, i: (b * nq + i, q_col_block)),
            pl.BlockSpec((mem_tokens, mw), lambda b, i: (b, 0)),
            pl.BlockSpec((mem_tokens, mw), lambda b, i: (b, 1)),
        ],
        out_specs=pl.BlockSpec((tq, mw), lambda b, i: (b * nq + i, 0)),
        out_shape=jax.ShapeDtypeStruct((batch * seq, mw), _BF16),
        compiler_params=_cparams(("parallel", "parallel")),
        name="memory_attention",
    )(proj, kvm, kvm)


def _out_proj_kernel(x_ref, a_ref, b_ref, wa_ref, wb_ref, o_ref):
    acc = jnp.dot(a_ref[...], wa_ref[...], preferred_element_type=_F32)
    acc = acc + jnp.dot(b_ref[...], wb_ref[...], preferred_element_type=_F32)
    o_ref[...] = x_ref[...] + acc


def _out_proj(x, o_mix, o_mem, w, layer, *, tm):
    m, d = x.shape
    ka, kb = o_mix.shape[1], o_mem.shape[1]
    assert ka % kb == 0
    return pl.pallas_call(
        _out_proj_kernel,
        grid=(m // tm,),
        in_specs=[
            pl.BlockSpec((tm, d), lambda i: (i, 0)),
            pl.BlockSpec((tm, ka), lambda i: (i, 0)),
            pl.BlockSpec((tm, kb), lambda i: (i, 0)),
            pl.BlockSpec((None, ka, d), lambda i: (layer, 0, 0)),
            pl.BlockSpec((None, kb, d), lambda i: (layer, ka // kb, 0)),
        ],
        out_specs=pl.BlockSpec((tm, d), lambda i: (i, 0)),
        out_shape=jax.ShapeDtypeStruct((m, d), _F32),
        compiler_params=_cparams(("parallel",)),
        name="out_proj",
    )(x, o_mix, o_mem, w, w)


def _mlp_kernel(x_ref, g_ref, w1_ref, w2_ref, gf_ref, o_ref, hn_ref, *, final_norm):
    f = pl.program_id(1)

    def ffn(h):
        a = jnp.dot(h, w1_ref[...], preferred_element_type=_F32)
        a = jnp.square(jnp.maximum(a, 0.0)).astype(_BF16)
        return jnp.dot(a, w2_ref[...], preferred_element_type=_F32)

    @pl.when(f == 0)
    def _():
        x = x_ref[...]
        h = _rms(x, g_ref[...]).astype(hn_ref.dtype)
        hn_ref[...] = h
        o_ref[...] = x + ffn(h)

    @pl.when(f > 0)
    def _():
        o_ref[...] += ffn(hn_ref[...])

    if final_norm:
        @pl.when(f == pl.num_programs(1) - 1)
        def _():
            o_ref[...] = _rms(o_ref[...], gf_ref[...])


def _mlp(x, g, w1, w2, layer, g_final, *, final_norm, tm, tf):
    m, d = x.shape
    ff = w1.shape[2]
    return pl.pallas_call(
        functools.partial(_mlp_kernel, final_norm=final_norm),
        grid=(m // tm, ff // tf),
        in_specs=[
            pl.BlockSpec((tm, d), lambda i, f: (i, 0)),
            pl.BlockSpec((1, d), lambda i, f: (0, 0)),
            pl.BlockSpec((None, d, tf), lambda i, f: (layer, 0, f)),
            pl.BlockSpec((None, tf, d), lambda i, f: (layer, f, 0)),
            pl.BlockSpec((1, d), lambda i, f: (0, 0)),
        ],
        out_specs=pl.BlockSpec((tm, d), lambda i, f: (i, 0)),
        out_shape=jax.ShapeDtypeStruct((m, d), _F32),
        scratch_shapes=[pltpu.VMEM((tm, d), _BF16)],
        compiler_params=_cparams(("parallel", "arbitrary")),
        name="mlp",
    )(x, g.reshape(1, d), w1, w2, g_final.reshape(1, d))


def kernel(x, mem, g_attn, w_in, w_out, lambda_qk, diff_subln_g, g_mem, w_mem_kv, g_mlp, w_mlp1, w_mlp2, g_final):
    batch, seq, d_model = x.shape
    depth = w_in.shape[0]
    mem_tokens = mem.shape[1]
    in_width = w_in.shape[2]
    mem_width = N_MEM_HEADS * HEAD_DIM
    mix_width = (in_width - mem_width) // 3
    n_heads = mix_width // HEAD_DIM
    slopes = _alibi_slopes(n_heads)
    dilations = tuple(d for _, d in DILATED_GROUPS)
    gw = HEADS_PER_GROUP * HEAD_DIM
    n_qkv_tiles = 3 * mix_width // gw

    xf = x.reshape(batch * seq, d_model)
    memf = mem.reshape(batch * mem_tokens, d_model)
    mix_tiles = mix_width // gw
    q_scale = LOG2E * (HEAD_DIM // 2) ** -0.5
    for i in range(depth):
        if i % N_MIXERS == 0:
            j = i // N_MIXERS
            lam_init = 0.8 - 0.6 * math.exp(-0.3 * i)
            k_qm_tiles = [mix_tiles + t for t in range(mix_tiles)] + [3 * mix_tiles]
            q_v_tiles = list(range(mix_tiles)) + [2 * mix_tiles + t for t in range(mix_tiles)]
            proj, qvt = _norm_matmul_dual(
                xf, g_attn[i], _cast_tiles(w_in, i, k_qm_tiles, tn=gw, transpose=False),
                _cast_tiles(w_in, i, q_v_tiles, tn=gw, transpose=False),
                tm=1024, tn=2 * gw, scaled_rows=mix_width, scale=q_scale)
            casts = [w_mem_kv, w_out, w_mlp1, w_mlp2, w_in] if i == 0 else []
            o_mix, casted = _diff_attention(qvt, proj, lambda_qk[j], diff_subln_g[j], slopes, casts, batch=batch,
                                            seq=seq, n_heads=n_heads, lam_init=lam_init, tq=256, tk=512, hb=6)
            if i == 0:
                w_kv_b, w_out_b, w1_b, w2_b, w_in_b = casted
            qm_col_block = mix_tiles
        else:
            tile_slot = [t % len(dilations) if t < n_qkv_tiles else 0 for t in range(in_width // gw)]
            proj = _norm_matmul(xf, g_attn[i], w_in_b, i, tm=1024, tn=2 * gw, dilations=dilations,
                                tile_slot=tile_slot, slot_width=gw)
            outs, lses = [], []
            for g, (window, dilation) in enumerate(DILATED_GROUPS):
                sl = slice(g * HEADS_PER_GROUP, (g + 1) * HEADS_PER_GROUP)
                o, lse = _dilated_group(proj, slopes[sl], batch=batch, seq=seq, group=g, window=window,
                                        dilation=dilation, in_width=in_width, mix_width=mix_width)
                outs.append(o)
                lses.append(lse)
            o_mix = _combine_groups(outs, lses, dilations, tm=PERM_ROWS)
            qm_col_block = 3 * mix_tiles
        kvm = _norm_matmul(memf, g_mem, w_kv_b, i, tm=512, tn=512)
        o_mem = _memory_attention(proj, kvm, batch=batch, seq=seq, mem_tokens=mem_tokens,
                                  q_col_block=qm_col_block, tq=1024)
        xf = _out_proj(xf, o_mix, o_mem, w_out_b, i, tm=512)
        xf = _mlp(xf, g_mlp[i], w1_b, w2_b, i, g_final, final_norm=(i == depth - 1), tm=1024, tf=512)
    return xf.reshape(batch, seq, d_model)
```

```python
import functools
import math

import numpy as np
import jax
import jax.numpy as jnp
from jax import lax
from jax.experimental import pallas as pl
from jax.experimental.pallas import tpu as pltpu

HEAD_DIM = 128
N_MEM_HEADS = 4
N_MIXERS = 2
DILATED_GROUPS = ((128, 1), (512, 4), (2048, 16))
HEADS_PER_GROUP = 4
NORM_EPS = 1e-6
NEG = -1e30
LANES = 128
VMEM_LIMIT = 56 * 1024 * 1024
PERM_ROWS = 512
LOG2E = math.log2(math.e)

_F32 = jnp.float32
_BF16 = jnp.bfloat16
_NT = (((1,), (1,)), ((), ()))


def _alibi_slopes(n):
    def pow2(m):
        start = 2.0 ** (-(2.0 ** -(math.log2(m) - 3)))
        return [start * start ** i for i in range(m)]

    def slopes(m):
        if math.log2(m).is_integer():
            return pow2(m)
        c = 2 ** math.floor(math.log2(m))
        return pow2(c) + slopes(2 * c)[0::2][: m - c]

    return np.asarray(sorted(slopes(n), reverse=True), dtype=np.float32)


def _cparams(semantics):
    return pltpu.CompilerParams(dimension_semantics=semantics, vmem_limit_bytes=VMEM_LIMIT)


def _rms(x, g):
    ms = jnp.mean(x * x, axis=-1, keepdims=True)
    return x * lax.rsqrt(ms + NORM_EPS) * g


def _norm_matmul_kernel(x_ref, g_ref, w_ref, o_ref, xn_ref, hn_ref, *, dilations, tile_slot, slot_width):
    j = pl.program_id(1)
    tm = x_ref.shape[0]
    parts = o_ref.shape[1] // slot_width

    def part(t, h):
        cols = slice(t * slot_width, (t + 1) * slot_width)
        o_ref[:, cols] = jnp.dot(h, w_ref[:, cols], preferred_element_type=_F32).astype(o_ref.dtype)

    def permute(s):
        d = dilations[s]
        n = PERM_ROWS // d
        for c in range(xn_ref.shape[0]):
            for grp in range(tm // PERM_ROWS):
                for r in range(d):
                    rows = xn_ref[c, pl.ds(grp * PERM_ROWS + r, n, stride=d), :]
                    dst = grp * PERM_ROWS + r * n
                    hn_ref[s, dst:dst + n, c * LANES:(c + 1) * LANES] = rows.astype(hn_ref.dtype)

    @pl.when(j == 0)
    def _():
        xn = _rms(x_ref[...], g_ref[...])
        h0 = xn.astype(hn_ref.dtype)
        hn_ref[0] = h0
        if len(dilations) > 1:
            for c in range(xn_ref.shape[0]):
                xn_ref[c] = xn[:, c * LANES:(c + 1) * LANES]
        done = {0}
        for t in range(parts):
            s = tile_slot[t]
            if s not in done:
                permute(s)
                done.add(s)
            part(t, h0 if s == 0 else hn_ref[s])
        for s in range(len(dilations)):
            if s not in done:
                permute(s)

    @pl.when(j > 0)
    def _():
        for t in range(parts):
            slot = _pick(j * parts + t, list(tile_slot)) if len(dilations) > 1 else 0
            part(t, hn_ref[slot])


def _norm_matmul(x, g, w, layer, *, tm, tn, dilations=(1,), tile_slot=None, slot_width=None):
    m, k = x.shape
    n_tiles = w.shape[2] // tn
    slot_width = tn if slot_width is None else slot_width
    tile_slot = (0,) * (n_tiles * tn // slot_width) if tile_slot is None else tuple(tile_slot)
    xn_shape = (k // LANES, tm, LANES) if len(dilations) > 1 else (1, 8, LANES)
    return pl.pallas_call(
        functools.partial(_norm_matmul_kernel, dilations=dilations, tile_slot=tile_slot, slot_width=slot_width),
        grid=(m // tm, n_tiles),
        in_specs=[
            pl.BlockSpec((tm, k), lambda i, j: (i, 0)),
            pl.BlockSpec((1, k), lambda i, j: (0, 0)),
            pl.BlockSpec((None, k, tn), lambda i, j: (layer, 0, j)),
        ],
        out_specs=pl.BlockSpec((tm, tn), lambda i, j: (i, j)),
        out_shape=jax.ShapeDtypeStruct((m, n_tiles * tn), _BF16),
        scratch_shapes=[pltpu.VMEM(xn_shape, _F32), pltpu.VMEM((len(dilations), tm, k), _BF16)],
        compiler_params=_cparams(("parallel", "arbitrary")),
        name="norm_matmul",
    )(x, g.reshape(1, k), w)


def _cast_tiles_kernel(w_ref, o_ref, *, transpose):
    w = w_ref[...]
    o_ref[...] = (w.T if transpose else w).astype(o_ref.dtype)


def _cast_tiles(w, layer, col_tiles, *, tn, transpose):
    k = w.shape[1]
    n = len(col_tiles)
    if transpose:
        out_spec, out_dims = pl.BlockSpec((tn, k), lambda j: (j, 0)), (n * tn, k)
    else:
        out_spec, out_dims = pl.BlockSpec((k, tn), lambda j: (0, j)), (k, n * tn)
    return pl.pallas_call(
        functools.partial(_cast_tiles_kernel, transpose=transpose),
        grid=(n,),
        in_specs=[pl.BlockSpec((None, k, tn), lambda j: (layer, 0, _pick(j, list(col_tiles))))],
        out_specs=out_spec,
        out_shape=jax.ShapeDtypeStruct(out_dims, _BF16),
        compiler_params=_cparams(("parallel",)),
        name="cast_tiles_t" if transpose else "cast_tiles",
    )(w)


def _pick(j, values):
    out = values[-1]
    for idx in range(len(values) - 2, -1, -1):
        out = jnp.where(j == idx, values[idx], out)
    return out


def _norm_matmul_dual_kernel(x_ref, g_ref, wa_ref, wb_ref, ta_ref, tb_ref, nat_ref, tr_ref, hn_ref,
                             *, nat_steps, scales_a, scales_b):
    j = pl.program_id(1)
    tn = wa_ref.shape[1]

    def natural(h):
        nat_ref[:, :tn] = jnp.dot(h, wa_ref[...], preferred_element_type=_F32).astype(nat_ref.dtype)
        nat_ref[:, tn:] = jnp.dot(h, wb_ref[...], preferred_element_type=_F32).astype(nat_ref.dtype)

    @pl.when(j == 0)
    def _():
        h = _rms(x_ref[...], g_ref[...]).astype(hn_ref.dtype)
        hn_ref[...] = h
        natural(h)

    @pl.when((j > 0) & (j < nat_steps))
    def _():
        natural(hn_ref[...])

    @pl.when(j >= nat_steps)
    def _():
        acc = lax.dot_general(ta_ref[...], hn_ref[...], _NT, preferred_element_type=_F32)
        tr_ref[:tn, :] = (acc * _pick(j, scales_a)).astype(tr_ref.dtype)
        acc = lax.dot_general(tb_ref[...], hn_ref[...], _NT, preferred_element_type=_F32)
        tr_ref[tn:, :] = (acc * _pick(j, scales_b)).astype(tr_ref.dtype)


def _norm_matmul_dual(x, g, w, layer, wt, *, tm, tn, nat_cols, t_rows):
    m, k = x.shape
    assert len(nat_cols) % 2 == 0 and len(t_rows) % 2 == 0
    nat_steps, t_steps = len(nat_cols) // 2, len(t_rows) // 2
    ca = [nat_cols[2 * s] for s in range(nat_steps)] + [nat_cols[-2]] * t_steps
    cb = [nat_cols[2 * s + 1] for s in range(nat_steps)] + [nat_cols[-1]] * t_steps
    ra = [t_rows[0][0]] * nat_steps + [t_rows[2 * s][0] for s in range(t_steps)]
    rb = [t_rows[1][0]] * nat_steps + [t_rows[2 * s + 1][0] for s in range(t_steps)]
    scales_a = [1.0] * nat_steps + [float(t_rows[2 * s][1]) for s in range(t_steps)]
    scales_b = [1.0] * nat_steps + [float(t_rows[2 * s + 1][1]) for s in range(t_steps)]
    return pl.pallas_call(
        functools.partial(_norm_matmul_dual_kernel, nat_steps=nat_steps, scales_a=scales_a, scales_b=scales_b),
        grid=(m // tm, nat_steps + t_steps),
        in_specs=[
            pl.BlockSpec((tm, k), lambda i, j: (i, 0)),
            pl.BlockSpec((1, k), lambda i, j: (0, 0)),
            pl.BlockSpec((None, k, tn), lambda i, j: (layer, 0, _pick(j, ca))),
            pl.BlockSpec((None, k, tn), lambda i, j: (layer, 0, _pick(j, cb))),
            pl.BlockSpec((tn, k), lambda i, j: (_pick(j, ra), 0)),
            pl.BlockSpec((tn, k), lambda i, j: (_pick(j, rb), 0)),
        ],
        out_specs=[
            pl.BlockSpec((tm, 2 * tn), lambda i, j: (i, jnp.minimum(j, nat_steps - 1))),
            pl.BlockSpec((2 * tn, tm), lambda i, j: (jnp.maximum(j - nat_steps, 0), i)),
        ],
        out_shape=[
            jax.ShapeDtypeStruct((m, len(nat_cols) * tn), _BF16),
            jax.ShapeDtypeStruct((len(t_rows) * tn, m), _BF16),
        ],
        scratch_shapes=[pltpu.VMEM((tm, k), _BF16)],
        compiler_params=_cparams(("parallel", "arbitrary")),
        name="norm_matmul_dual",
    )(x, g.reshape(1, k), w, w, wt, wt)


def _diff_attn_kernel(*refs, n_cast, tq, tk, hb, ones_rows, lam_init):
    lam_ref, sig_ref, sigp_ref, g_ref, kc_ref, qt_ref, k_ref, vt_ref = refs[:8]
    cast_in = refs[8:8 + n_cast]
    o_ref = refs[8 + n_cast]
    cast_out = refs[9 + n_cast:9 + 2 * n_cast]
    m_ref, acc_ref = refs[9 + 2 * n_cast:]
    for src, dst in zip(cast_in, cast_out):
        dst[...] = src[...].astype(dst.dtype)
    qi = pl.program_id(2)
    hd = HEAD_DIM
    half = hd // 2
    reps = 2 * tq // LANES

    lp = lam_ref[...]
    lam = (jnp.exp(jnp.sum(lp[0:1] * lp[1:2], axis=1, keepdims=True))
           - jnp.exp(jnp.sum(lp[2:3] * lp[3:4], axis=1, keepdims=True)) + lam_init)

    row = lax.broadcasted_iota(jnp.int32, (hd, tq), 0)
    qzts, sigs = [], []
    for hh in range(hb):
        qt = qt_ref[hh * hd:(hh + 1) * hd, :]
        zero = jnp.zeros_like(qt)
        top = jnp.concatenate([jnp.where(row < half, qt, zero), jnp.where(row >= half, qt, zero)], axis=1)
        ext = jnp.concatenate([sigp_ref[hh]] * reps, axis=1).astype(_BF16)
        qzts.append(jnp.concatenate([top, ext], axis=0))
        sigs.append(sig_ref[hh][:, :1])
    key_i = lax.broadcasted_iota(jnp.int32, (tq, 2 * tq), 0)
    qry_i = lax.broadcasted_iota(jnp.int32, (tq, 2 * tq), 1)
    causal = key_i <= jnp.where(qry_i >= tq, qry_i - tq, qry_i)

    m_ref[...] = jnp.full(m_ref.shape, NEG, _F32)
    acc_ref[...] = jnp.zeros(acc_ref.shape, _F32)

    def step(start, size, masked):
        off = (start - qi * tq).astype(_F32)
        kc = kc_ref[:size, :]
        ones = jnp.ones((ones_rows, size), _BF16)
        ps = []
        for hh in range(hb):
            k = jnp.concatenate([k_ref[pl.ds(start, size), hh * hd:(hh + 1) * hd], kc], axis=1)
            c = sigs[hh] * off
            s = jnp.dot(k, qzts[hh], preferred_element_type=_F32)
            if masked:
                s = jnp.where(causal, s, NEG)
            m_prev = m_ref[hh]
            m_new = jnp.maximum(m_prev, jnp.max(s, axis=0, keepdims=True) + c)
            alpha = jnp.exp2(m_prev - m_new)
            p = jnp.exp2(s - (m_new - c))
            m_ref[hh] = m_new
            ps.append((p.astype(_BF16), alpha))
        for hh in range(hb):
            p, alpha = ps[hh]
            vt = jnp.concatenate([vt_ref[hh * hd:(hh + 1) * hd, pl.ds(start, size)], ones], axis=0)
            acc_ref[hh] = acc_ref[hh] * alpha + jnp.dot(vt, p, preferred_element_type=_F32)

    nb = (qi * tq) // tk

    def body(j, carry):
        step(pl.multiple_of(2 * j * tk, tk), tk, False)
        step(pl.multiple_of((2 * j + 1) * tk, tk), tk, False)
        return carry

    lax.fori_loop(0, nb // 2, body, 0)

    @pl.when(nb % 2 == 1)
    def _():
        step(pl.multiple_of((nb - 1) * tk, tk), tk, False)
    if tk != tq:
        @pl.when(qi % (tk // tq) == 1)
        def _():
            step(pl.multiple_of((qi - 1) * tq, tq), tq, False)
    step(pl.multiple_of(qi * tq, tq), tq, True)

    for hh in range(hb):
        ot = acc_ref[hh, :hd, :] / acc_ref[hh, hd:hd + 1, :]
        odt = ot[:, :tq] - lam * ot[:, tq:]
        ms = jnp.mean(odt * odt, axis=0, keepdims=True)
        yt = odt * lax.rsqrt(ms + NORM_EPS) * g_ref[...] * (1.0 - lam_init)
        o_ref[:, hh * hd:(hh + 1) * hd] = yt.T.astype(o_ref.dtype)


def _bf16_pieces(x):
    hi = x.astype(_BF16).astype(np.float32)
    mid = (x - hi).astype(_BF16).astype(np.float32)
    lo = (x - hi - mid).astype(_BF16).astype(np.float32)
    return hi, mid, lo


def _diff_attention(qvt, knat, lam_params, subln_g, slopes, casts, *, batch, seq, n_heads, lam_init, tq, tk, hb):
    assert tk in (tq, 2 * tq)
    nq = seq // tq
    hd = HEAD_DIM
    ng = n_heads // hb
    sig = (slopes.astype(np.float64) * LOG2E).astype(np.float32)
    sig_arr = np.broadcast_to(sig.reshape(n_heads, 1, 1), (n_heads, 1, hd))
    sigp_arr = np.zeros((n_heads, hd, LANES), np.float32)
    for idx, piece in enumerate(_bf16_pieces(sig)):
        sigp_arr[:, idx, :] = piece[:, None]
        sigp_arr[:, 3 + idx, :] = piece[:, None]
    key = np.arange(tk)
    kc_arr = np.zeros((tk, LANES), np.float32)
    kc_arr[:, 0:3] = (key & 255)[:, None]
    kc_arr[:, 3:6] = (key - (key & 255))[:, None]
    ones_rows = 16
    steps = batch * ng * nq
    cast_2d = [c.reshape(-1, c.shape[-1]) for c in casts]
    assert all(c.shape[0] % (16 * steps) == 0 for c in cast_2d), "cast slabs must be whole bf16 sublane tiles"
    cast_specs = [pl.BlockSpec((c.shape[0] // steps, c.shape[1]), lambda b, h, i: ((b * ng + h) * nq + i, 0))
                  for c in cast_2d]
    kern = functools.partial(_diff_attn_kernel, n_cast=len(casts), tq=tq, tk=tk, hb=hb, ones_rows=ones_rows,
                             lam_init=lam_init)
    outs = pl.pallas_call(
        kern,
        grid=(batch, ng, nq),
        in_specs=[
            pl.BlockSpec(lam_params.shape, lambda b, h, i: (0, 0)),
            pl.BlockSpec((hb, 1, hd), lambda b, h, i: (h, 0, 0)),
            pl.BlockSpec((hb, hd, LANES), lambda b, h, i: (h, 0, 0)),
            pl.BlockSpec((hd, 1), lambda b, h, i: (0, 0)),
            pl.BlockSpec((tk, LANES), lambda b, h, i: (0, 0)),
            pl.BlockSpec((hb * hd, tq), lambda b, h, i: (h, b * nq + i)),
            pl.BlockSpec((seq, hb * hd), lambda b, h, i: (b, h)),
            pl.BlockSpec((hb * hd, seq), lambda b, h, i: (ng + h, b)),
        ] + cast_specs,
        out_specs=[pl.BlockSpec((tq, hb * hd), lambda b, h, i: (b * nq + i, h))] + cast_specs,
        out_shape=[jax.ShapeDtypeStruct((batch * seq, n_heads * hd), _BF16)]
        + [jax.ShapeDtypeStruct(c.shape, _BF16) for c in cast_2d],
        scratch_shapes=[
            pltpu.VMEM((hb, 1, 2 * tq), _F32),
            pltpu.VMEM((hb, hd + ones_rows, 2 * tq), _F32),
        ],
        compiler_params=_cparams(("parallel", "parallel", "arbitrary")),
        name="diff_attention",
    )(lam_params, jnp.asarray(sig_arr), jnp.asarray(sigp_arr), subln_g.reshape(hd, 1),
      jnp.asarray(kc_arr, dtype=_BF16), qvt, knat, qvt, *cast_2d)
    return outs[0], [o.reshape(c.shape) for o, c in zip(outs[1:], casts)]


def _rows(ref, n, st, start, size, sl):
    if len(ref.shape) == 2:
        return ref[start:start + size, sl]
    return ref[start // n:(start + size) // n, st * n:(st + 1) * n, sl].reshape(size, sl.stop - sl.start)


def _store_rows(ref, n, st, start, size, sl, val):
    if len(ref.shape) == 2:
        ref[start:start + size, sl] = val
    else:
        ref[start // n:(start + size) // n, st * n:(st + 1) * n, sl] = val.reshape(size // n, n, sl.stop - sl.start)


def _dilated_kernel(*refs, tq, span, n, streams, slopes2, scale2, has_prev):
    if has_prev:
        q_ref, kp_ref, kc_ref, vp_ref, vc_ref, o_ref, lse_ref = refs
    else:
        q_ref, kc_ref, vc_ref, o_ref, lse_ref = refs
    length = q_ref.shape[0] if len(q_ref.shape) == 2 else q_ref.shape[0] * n
    first = pl.program_id(2) == 0
    nh = len(slopes2)
    lane_grp = lax.broadcasted_iota(jnp.int32, (tq, LANES), 1) // (LANES // nh)

    def masked_bias(nk, shift, first_block):
        row = lax.broadcasted_iota(jnp.int32, (tq, nk), 0)
        col = lax.broadcasted_iota(jnp.int32, (tq, nk), 1)
        dist = row - col + shift
        valid = (dist >= 0) & (dist <= span)
        if first_block and has_prev:
            valid = valid & ((col >= span) | jnp.logical_not(first))
        distf = dist.astype(_F32)
        return [jnp.where(valid, -s2 * distf, NEG) for s2 in slopes2]

    bias_head = masked_bias(tq + span if has_prev else tq, span if has_prev else 0, True)
    bias_body = masked_bias(tq + span, span, False) if length > tq else None

    for st in range(streams):
        for qb in range(length // tq):
            u0 = qb * tq
            lses = []
            for hh in range(nh):
                sl = slice(hh * HEAD_DIM, (hh + 1) * HEAD_DIM)
                q = _rows(q_ref, n, st, u0, tq, sl)
                if qb == 0 and has_prev:
                    k = jnp.concatenate([kp_ref[:, sl], _rows(kc_ref, n, st, 0, tq, sl)], axis=0)
                    v = jnp.concatenate([vp_ref[:, sl], _rows(vc_ref, n, st, 0, tq, sl)], axis=0)
                elif qb == 0:
                    k, v = _rows(kc_ref, n, st, 0, tq, sl), _rows(vc_ref, n, st, 0, tq, sl)
                else:
                    k = _rows(kc_ref, n, st, u0 - span, tq + span, sl)
                    v = _rows(vc_ref, n, st, u0 - span, tq + span, sl)
                bias = (bias_head if qb == 0 else bias_body)[hh]
                s = lax.dot_general(q, k, _NT, preferred_element_type=_F32) * scale2 + bias
                m = jnp.max(s, axis=1, keepdims=True)
                p = jnp.exp2(s - m)
                l = jnp.sum(p, axis=1, keepdims=True)
                o = jnp.dot(p.astype(_BF16), v, preferred_element_type=_F32) / l
                _store_rows(o_ref, n, st, u0, tq, sl, o.astype(o_ref.dtype))
                lses.append(m + jnp.log2(l))
            packed = jnp.broadcast_to(lses[nh - 1], (tq, LANES))
            for hh in range(nh - 2, -1, -1):
                packed = jnp.where(lane_grp == hh, lses[hh], packed)
            _store_rows(lse_ref, n, st, u0, tq, slice(0, LANES), packed)


def _dilated_group(proj, slopes_g, *, batch, seq, group, window, dilation, in_width, mix_width):
    span = window // dilation
    assert span == LANES, "key window per stream must be one 128-row block"
    gw = HEADS_PER_GROUP * HEAD_DIM
    koff = mix_width // gw
    rows = batch * seq
    n = PERM_ROWS // dilation
    tq = span if n >= span else 2 * span
    streams = max(1, LANES // n) if dilation > 1 else 1
    kern = functools.partial(
        _dilated_kernel, tq=tq, span=span, n=n, streams=streams,
        slopes2=tuple(float(s) * dilation * LOG2E for s in slopes_g), scale2=HEAD_DIM ** -0.5 * LOG2E,
        has_prev=dilation == 1)
    if dilation == 1:
        chunk = 1024
        nc = seq // chunk
        cur = lambda c: pl.BlockSpec((chunk, gw), lambda b, r, u: (b * nc + u, c))
        prev = lambda c: pl.BlockSpec(
            (span, gw), lambda b, r, u: (b * (seq // span) + jnp.maximum(u * (chunk // span) - 1, 0), c))
        in_specs = [cur(group), prev(koff + group), cur(koff + group), prev(2 * koff + group), cur(2 * koff + group)]
        operands = [proj] * 5
        out_specs = [pl.BlockSpec((chunk, w), lambda b, r, u: (b * nc + u, 0)) for w in (gw, LANES)]
        out_dims = [(rows, gw), (rows, LANES)]
        grid = (batch, 1, nc)
    else:
        tiles = seq // PERM_ROWS
        proj3 = proj.reshape(rows // PERM_ROWS, PERM_ROWS, in_width)
        blk = lambda c, w: pl.BlockSpec((tiles, streams * n, w), lambda b, r, u: (b, r, c))
        in_specs = [blk(group, gw), blk(koff + group, gw), blk(2 * koff + group, gw)]
        operands = [proj3] * 3
        out_specs = [blk(0, gw), blk(0, LANES)]
        out_dims = [(rows // PERM_ROWS, PERM_ROWS, gw), (rows // PERM_ROWS, PERM_ROWS, LANES)]
        grid = (batch, dilation // streams, 1)
    o, lse = pl.pallas_call(
        kern,
        grid=grid,
        in_specs=in_specs,
        out_specs=out_specs,
        out_shape=[jax.ShapeDtypeStruct(out_dims[0], _BF16), jax.ShapeDtypeStruct(out_dims[1], _F32)],
        compiler_params=_cparams(("parallel", "parallel", "arbitrary")),
        name=f"dilated_attention_d{dilation}",
    )(*operands)
    return o.reshape(rows, gw), lse.reshape(rows, LANES)


def _combine_kernel(o0_ref, o1_ref, o2_ref, l0_ref, l1_ref, l2_ref, out_ref, on_ref, ln_ref, *, dilations):
    tm = out_ref.shape[0]
    gw = o0_ref.shape[1]
    nh = gw // LANES
    for g, (o_ref, l_ref, d) in enumerate(zip((o0_ref, o1_ref, o2_ref), (l0_ref, l1_ref, l2_ref), dilations)):
        n = PERM_ROWS // d
        if d == 1:
            ln_ref[g] = l_ref[...]
            for hh in range(nh):
                on_ref[g, hh] = o_ref[:, hh * LANES:(hh + 1) * LANES].astype(_F32)
            continue
        for grp in range(tm // PERM_ROWS):
            for r in range(d):
                src = slice(grp * PERM_ROWS + r * n, grp * PERM_ROWS + (r + 1) * n)
                dst = pl.ds(grp * PERM_ROWS + r, n, stride=d)
                ln_ref[g, dst, :] = l_ref[src, :]
                for hh in range(nh):
                    on_ref[g, hh, dst, :] = o_ref[src, hh * LANES:(hh + 1) * LANES].astype(_F32)
    l0, l1, l2 = ln_ref[0], ln_ref[1], ln_ref[2]
    mx = jnp.maximum(jnp.maximum(l0, l1), l2)
    es = (jnp.exp2(l0 - mx), jnp.exp2(l1 - mx), jnp.exp2(l2 - mx))
    den = es[0] + es[1] + es[2]
    for g in range(3):
        wg = es[g] / den
        for hh in range(nh):
            lane0 = hh * (LANES // nh)
            col = g * gw + hh * LANES
            out_ref[:, col:col + LANES] = (on_ref[g, hh] * wg[:, lane0:lane0 + 1]).astype(out_ref.dtype)


def _combine_groups(outs, lses, dilations, *, tm):
    m, gw = outs[0].shape
    spec = pl.BlockSpec((tm, gw), lambda i: (i, 0))
    return pl.pallas_call(
        functools.partial(_combine_kernel, dilations=dilations),
        grid=(m // tm,),
        in_specs=[spec] * 3 + [pl.BlockSpec((tm, LANES), lambda i: (i, 0))] * 3,
        out_specs=pl.BlockSpec((tm, 3 * gw), lambda i: (i, 0)),
        out_shape=jax.ShapeDtypeStruct((m, 3 * gw), _BF16),
        scratch_shapes=[pltpu.VMEM((3, gw // LANES, tm, LANES), _F32), pltpu.VMEM((3, tm, LANES), _F32)],
        compiler_params=_cparams(("parallel",)),
        name="combine_groups",
    )(*outs, *lses)


def _mem_attn_kernel(q_ref, k_ref, v_ref, o_ref, *, scale):
    for hh in range(N_MEM_HEADS):
        sl = slice(hh * HEAD_DIM, (hh + 1) * HEAD_DIM)
        s = lax.dot_general(q_ref[:, sl], k_ref[:, sl], _NT, preferred_element_type=_F32) * scale
        m = jnp.max(s, axis=1, keepdims=True)
        p = jnp.exp(s - m)
        l = jnp.sum(p, axis=1, keepdims=True)
        o = jnp.dot(p.astype(_BF16), v_ref[:, sl], preferred_element_type=_F32)
        o_ref[:, sl] = (o / l).astype(o_ref.dtype)


def _memory_attention(proj, kvm, *, batch, seq, mem_tokens, q_col_block, tq):
    nq = seq // tq
    mw = N_MEM_HEADS * HEAD_DIM
    return pl.pallas_call(
        functools.partial(_mem_attn_kernel, scale=HEAD_DIM ** -0.5),
        grid=(batch, nq),
        in_specs=[
            pl.BlockSpec((tq, mw), lambda b, i: (b * nq + i, q_col_block)),
            pl.BlockSpec((mem_tokens, mw), lambda b, i: (b, 0)),
            pl.BlockSpec((mem_tokens, mw), lambda b, i: (b, 1)),
        ],
        out_specs=pl.BlockSpec((tq, mw), lambda b, i: (b * nq + i, 0)),
        out_shape=jax.ShapeDtypeStruct((batch * seq, mw), _BF16),
        compiler_params=_cparams(("parallel", "parallel")),
        name="memory_attention",
    )(proj, kvm, kvm)


def _out_proj_kernel(x_ref, a_ref, b_ref, wa_ref, wb_ref, o_ref):
    acc = jnp.dot(a_ref[...], wa_ref[...], preferred_element_type=_F32)
    acc = acc + jnp.dot(b_ref[...], wb_ref[...], preferred_element_type=_F32)
    o_ref[...] = x_ref[...] + acc


def _out_proj(x, o_mix, o_mem, w, layer, *, tm):
    m, d = x.shape
    ka, kb = o_mix.shape[1], o_mem.shape[1]
    assert ka % kb == 0
    return pl.pallas_call(
        _out_proj_kernel,
        grid=(m // tm,),
        in_specs=[
            pl.BlockSpec((tm, d), lambda i: (i, 0)),
            pl.BlockSpec((tm, ka), lambda i: (i, 0)),
            pl.BlockSpec((tm, kb), lambda i: (i, 0)),
            pl.BlockSpec((None, ka, d), lambda i: (layer, 0, 0)),
            pl.BlockSpec((None, kb, d), lambda i: (layer, ka // kb, 0)),
        ],
        out_specs=pl.BlockSpec((tm, d), lambda i: (i, 0)),
        out_shape=jax.ShapeDtypeStruct((m, d), _F32),
        compiler_params=_cparams(("parallel",)),
        name="out_proj",
    )(x, o_mix, o_mem, w, w)


def _mlp_kernel(x_ref, g_ref, w1_ref, w2_ref, gf_ref, o_ref, hn_ref, *, final_norm):
    f = pl.program_id(1)

    def ffn(h):
        a = jnp.dot(h, w1_ref[...], preferred_element_type=_F32)
        a = jnp.square(jnp.maximum(a, 0.0)).astype(_BF16)
        return jnp.dot(a, w2_ref[...], preferred_element_type=_F32)

    @pl.when(f == 0)
    def _():
        x = x_ref[...]
        h = _rms(x, g_ref[...]).astype(hn_ref.dtype)
        hn_ref[...] = h
        o_ref[...] = x + ffn(h)

    last = pl.num_programs(1) - 1

    @pl.when((f > 0) & (f < last) if final_norm else f > 0)
    def _():
        o_ref[...] += ffn(hn_ref[...])

    if final_norm:
        @pl.when(f == last)
        def _():
            o_ref[...] = _rms(o_ref[...] + ffn(hn_ref[...]), gf_ref[...])


def _mlp(x, g, w1, w2, layer, g_final, *, final_norm, tm, tf):
    m, d = x.shape
    ff = w1.shape[2]
    return pl.pallas_call(
        functools.partial(_mlp_kernel, final_norm=final_norm),
        grid=(m // tm, ff // tf),
        in_specs=[
            pl.BlockSpec((tm, d), lambda i, f: (i, 0)),
            pl.BlockSpec((1, d), lambda i, f: (0, 0)),
            pl.BlockSpec((None, d, tf), lambda i, f: (layer, 0, f)),
            pl.BlockSpec((None, tf, d), lambda i, f: (layer, f, 0)),
            pl.BlockSpec((1, d), lambda i, f: (0, 0)),
        ],
        out_specs=pl.BlockSpec((tm, d), lambda i, f: (i, 0)),
        out_shape=jax.ShapeDtypeStruct((m, d), _F32),
        scratch_shapes=[pltpu.VMEM((tm, d), _BF16)],
        compiler_params=_cparams(("parallel", "arbitrary")),
        name="mlp",
    )(x, g.reshape(1, d), w1, w2, g_final.reshape(1, d))


def kernel(x, mem, g_attn, w_in, w_out, lambda_qk, diff_subln_g, g_mem, w_mem_kv, g_mlp, w_mlp1, w_mlp2, g_final):
    batch, seq, d_model = x.shape
    depth = w_in.shape[0]
    mem_tokens = mem.shape[1]
    in_width = w_in.shape[2]
    mem_width = N_MEM_HEADS * HEAD_DIM
    mix_width = (in_width - mem_width) // 3
    n_heads = mix_width // HEAD_DIM
    slopes = _alibi_slopes(n_heads)
    dilations = tuple(d for _, d in DILATED_GROUPS)
    gw = HEADS_PER_GROUP * HEAD_DIM
    n_qkv_tiles = 3 * mix_width // gw

    xf = x.reshape(batch * seq, d_model)
    memf = mem.reshape(batch * mem_tokens, d_model)
    mix_tiles = mix_width // gw
    q_scale = LOG2E * (HEAD_DIM // 2) ** -0.5
    for i in range(depth):
        if i % N_MIXERS == 0:
            j = i // N_MIXERS
            lam_init = 0.8 - 0.6 * math.exp(-0.3 * i)
            k_qm_tiles = [mix_tiles + t for t in range(mix_tiles)] + [3 * mix_tiles]
            q_v_tiles = list(range(mix_tiles)) + [2 * mix_tiles + t for t in range(mix_tiles)]
            if i == 0:
                w_nat = _cast_tiles(w_in, i, k_qm_tiles, tn=gw, transpose=False)[None]
                nat_cols, layer = list(range(mix_tiles + 1)), 0
            else:
                w_nat, nat_cols, layer = w_in_b, k_qm_tiles, i
            proj, qvt = _norm_matmul_dual(
                xf, g_attn[i], w_nat, layer, _cast_tiles(w_in, i, q_v_tiles, tn=gw, transpose=True),
                tm=1024, tn=gw, nat_cols=nat_cols,
                t_rows=[(t, q_scale) for t in range(mix_tiles)] + [(mix_tiles + t, 1.0) for t in range(mix_tiles)])
            casts = [w_mem_kv, w_out, w_mlp1, w_mlp2, w_in] if i == 0 else []
            o_mix, casted = _diff_attention(qvt, proj, lambda_qk[j], diff_subln_g[j], slopes, casts, batch=batch,
                                            seq=seq, n_heads=n_heads, lam_init=lam_init, tq=256, tk=512, hb=6)
            if i == 0:
                w_kv_b, w_out_b, w1_b, w2_b, w_in_b = casted
            qm_col_block = mix_tiles
        else:
            tile_slot = [t % len(dilations) if t < n_qkv_tiles else 0 for t in range(in_width // gw)]
            proj = _norm_matmul(xf, g_attn[i], w_in_b, i, tm=1024, tn=2 * gw, dilations=dilations,
                                tile_slot=tile_slot, slot_width=gw)
            outs, lses = [], []
            for g, (window, dilation) in enumerate(DILATED_GROUPS):
                sl = slice(g * HEADS_PER_GROUP, (g + 1) * HEADS_PER_GROUP)
                o, lse = _dilated_group(proj, slopes[sl], batch=batch, seq=seq, group=g, window=window,
                                        dilation=dilation, in_width=in_width, mix_width=mix_width)
                outs.append(o)
                lses.append(lse)
            o_mix = _combine_groups(outs, lses, dilations, tm=2 * PERM_ROWS)
            qm_col_block = 3 * mix_tiles
        kvm = _norm_matmul(memf, g_mem, w_kv_b, i, tm=1024, tn=1024)
        o_mem = _memory_attention(proj, kvm, batch=batch, seq=seq, mem_tokens=mem_tokens,
                                  q_col_block=qm_col_block, tq=2048)
        xf = _out_proj(xf, o_mix, o_mem, w_out_b, i, tm=512)
        xf = _mlp(xf, g_mlp[i], w1_b, w2_b, i, g_final, final_norm=(i == depth - 1), tm=1024, tf=512)
    return xf.reshape(batch, seq, d_model)
```

```python
import functools
import math

import numpy as np
import jax
import jax.numpy as jnp
from jax import lax
from jax.experimental import pallas as pl
from jax.experimental.pallas import tpu as pltpu

HEAD_DIM = 128
N_MEM_HEADS = 4
N_MIXERS = 2
DILATED_GROUPS = ((128, 1), (512, 4), (2048, 16))
HEADS_PER_GROUP = 4
NORM_EPS = 1e-6
NEG = -1e30
LANES = 128
VMEM_LIMIT = 56 * 1024 * 1024
PERM_ROWS = 512
LOG2E = math.log2(math.e)

_F32 = jnp.float32
_BF16 = jnp.bfloat16
_NT = (((1,), (1,)), ((), ()))


def _alibi_slopes(n):
    def pow2(m):
        start = 2.0 ** (-(2.0 ** -(math.log2(m) - 3)))
        return [start * start ** i for i in range(m)]

    def slopes(m):
        if math.log2(m).is_integer():
            return pow2(m)
        c = 2 ** math.floor(math.log2(m))
        return pow2(c) + slopes(2 * c)[0::2][: m - c]

    return np.asarray(sorted(slopes(n), reverse=True), dtype=np.float32)


def _cparams(semantics):
    return pltpu.CompilerParams(dimension_semantics=semantics, vmem_limit_bytes=VMEM_LIMIT)


def _rms(x, g):
    ms = jnp.mean(x * x, axis=-1, keepdims=True)
    return x * lax.rsqrt(ms + NORM_EPS) * g


def _norm_matmul_kernel(x_ref, g_ref, w_ref, o_ref, xn_ref, hn_ref, *, dilations, tile_slot, slot_width):
    j = pl.program_id(1)
    tm = x_ref.shape[0]
    parts = o_ref.shape[1] // slot_width

    def part(t, h):
        cols = slice(t * slot_width, (t + 1) * slot_width)
        o_ref[:, cols] = jnp.dot(h, w_ref[:, cols], preferred_element_type=_F32).astype(o_ref.dtype)

    def permute(s):
        d = dilations[s]
        n = PERM_ROWS // d
        for c in range(xn_ref.shape[0]):
            for grp in range(tm // PERM_ROWS):
                for r in range(d):
                    rows = xn_ref[c, pl.ds(grp * PERM_ROWS + r, n, stride=d), :]
                    dst = grp * PERM_ROWS + r * n
                    hn_ref[s, dst:dst + n, c * LANES:(c + 1) * LANES] = rows.astype(hn_ref.dtype)

    @pl.when(j == 0)
    def _():
        xn = _rms(x_ref[...], g_ref[...])
        h0 = xn.astype(hn_ref.dtype)
        hn_ref[0] = h0
        if len(dilations) > 1:
            for c in range(xn_ref.shape[0]):
                xn_ref[c] = xn[:, c * LANES:(c + 1) * LANES]
        done = {0}
        for t in range(parts):
            s = tile_slot[t]
            if s not in done:
                permute(s)
                done.add(s)
            part(t, h0 if s == 0 else hn_ref[s])
        for s in range(len(dilations)):
            if s not in done:
                permute(s)

    @pl.when(j > 0)
    def _():
        for t in range(parts):
            slot = _pick(j * parts + t, list(tile_slot)) if len(dilations) > 1 else 0
            part(t, hn_ref[slot])


def _norm_matmul(x, g, w, layer, *, tm, tn, dilations=(1,), tile_slot=None, slot_width=None):
    m, k = x.shape
    n_tiles = w.shape[2] // tn
    slot_width = tn if slot_width is None else slot_width
    tile_slot = (0,) * (n_tiles * tn // slot_width) if tile_slot is None else tuple(tile_slot)
    xn_shape = (k // LANES, tm, LANES) if len(dilations) > 1 else (1, 8, LANES)
    return pl.pallas_call(
        functools.partial(_norm_matmul_kernel, dilations=dilations, tile_slot=tile_slot, slot_width=slot_width),
        grid=(m // tm, n_tiles),
        in_specs=[
            pl.BlockSpec((tm, k), lambda i, j: (i, 0)),
            pl.BlockSpec((1, k), lambda i, j: (0, 0)),
            pl.BlockSpec((None, k, tn), lambda i, j: (layer, 0, j)),
        ],
        out_specs=pl.BlockSpec((tm, tn), lambda i, j: (i, j)),
        out_shape=jax.ShapeDtypeStruct((m, n_tiles * tn), _BF16),
        scratch_shapes=[pltpu.VMEM(xn_shape, _F32), pltpu.VMEM((len(dilations), tm, k), _BF16)],
        compiler_params=_cparams(("parallel", "arbitrary")),
        name="norm_matmul",
    )(x, g.reshape(1, k), w)


def _cast_tiles_kernel(w_ref, o_ref, *, transpose):
    w = w_ref[...]
    o_ref[...] = (w.T if transpose else w).astype(o_ref.dtype)


def _cast_tiles(w, layer, col_tiles, *, tn, transpose):
    k = w.shape[1]
    n = len(col_tiles)
    if transpose:
        out_spec, out_dims = pl.BlockSpec((tn, k), lambda j: (j, 0)), (n * tn, k)
    else:
        out_spec, out_dims = pl.BlockSpec((k, tn), lambda j: (0, j)), (k, n * tn)
    return pl.pallas_call(
        functools.partial(_cast_tiles_kernel, transpose=transpose),
        grid=(n,),
        in_specs=[pl.BlockSpec((None, k, tn), lambda j: (layer, 0, _pick(j, list(col_tiles))))],
        out_specs=out_spec,
        out_shape=jax.ShapeDtypeStruct(out_dims, _BF16),
        compiler_params=_cparams(("parallel",)),
        name="cast_tiles_t" if transpose else "cast_tiles",
    )(w)


def _pick(j, values):
    out = values[-1]
    for idx in range(len(values) - 2, -1, -1):
        out = jnp.where(j == idx, values[idx], out)
    return out


def _norm_matmul_dual_kernel(x_ref, g_ref, wa_ref, wb_ref, ta_ref, tb_ref, nat_ref, tr_ref, hn_ref,
                             *, nat_steps, scales_a, scales_b):
    j = pl.program_id(1)
    tn = wa_ref.shape[1]

    def natural(h):
        nat_ref[:, :tn] = jnp.dot(h, wa_ref[...], preferred_element_type=_F32).astype(nat_ref.dtype)
        nat_ref[:, tn:] = jnp.dot(h, wb_ref[...], preferred_element_type=_F32).astype(nat_ref.dtype)

    @pl.when(j == 0)
    def _():
        h = _rms(x_ref[...], g_ref[...]).astype(hn_ref.dtype)
        hn_ref[...] = h
        natural(h)

    @pl.when((j > 0) & (j < nat_steps))
    def _():
        natural(hn_ref[...])

    @pl.when(j >= nat_steps)
    def _():
        acc = lax.dot_general(ta_ref[...], hn_ref[...], _NT, preferred_element_type=_F32)
        tr_ref[:tn, :] = (acc * _pick(j, scales_a)).astype(tr_ref.dtype)
        acc = lax.dot_general(tb_ref[...], hn_ref[...], _NT, preferred_element_type=_F32)
        tr_ref[tn:, :] = (acc * _pick(j, scales_b)).astype(tr_ref.dtype)


def _norm_matmul_dual(x, g, w, layer, wt, *, tm, tn, nat_cols, t_rows):
    m, k = x.shape
    assert len(nat_cols) % 2 == 0 and len(t_rows) % 2 == 0
    nat_steps, t_steps = len(nat_cols) // 2, len(t_rows) // 2
    ca = [nat_cols[2 * s] for s in range(nat_steps)] + [nat_cols[-2]] * t_steps
    cb = [nat_cols[2 * s + 1] for s in range(nat_steps)] + [nat_cols[-1]] * t_steps
    ra = [t_rows[0][0]] * nat_steps + [t_rows[2 * s][0] for s in range(t_steps)]
    rb = [t_rows[1][0]] * nat_steps + [t_rows[2 * s + 1][0] for s in range(t_steps)]
    scales_a = [1.0] * nat_steps + [float(t_rows[2 * s][1]) for s in range(t_steps)]
    scales_b = [1.0] * nat_steps + [float(t_rows[2 * s + 1][1]) for s in range(t_steps)]
    return pl.pallas_call(
        functools.partial(_norm_matmul_dual_kernel, nat_steps=nat_steps, scales_a=scales_a, scales_b=scales_b),
        grid=(m // tm, nat_steps + t_steps),
        in_specs=[
            pl.BlockSpec((tm, k), lambda i, j: (i, 0)),
            pl.BlockSpec((1, k), lambda i, j: (0, 0)),
            pl.BlockSpec((None, k, tn), lambda i, j: (layer, 0, _pick(j, ca))),
            pl.BlockSpec((None, k, tn), lambda i, j: (layer, 0, _pick(j, cb))),
            pl.BlockSpec((tn, k), lambda i, j: (_pick(j, ra), 0)),
            pl.BlockSpec((tn, k), lambda i, j: (_pick(j, rb), 0)),
        ],
        out_specs=[
            pl.BlockSpec((tm, 2 * tn), lambda i, j: (i, jnp.minimum(j, nat_steps - 1))),
            pl.BlockSpec((2 * tn, tm), lambda i, j: (jnp.maximum(j - nat_steps, 0), i)),
        ],
        out_shape=[
            jax.ShapeDtypeStruct((m, len(nat_cols) * tn), _BF16),
            jax.ShapeDtypeStruct((len(t_rows) * tn, m), _BF16),
        ],
        scratch_shapes=[pltpu.VMEM((tm, k), _BF16)],
        compiler_params=_cparams(("parallel", "arbitrary")),
        name="norm_matmul_dual",
    )(x, g.reshape(1, k), w, w, wt, wt)


def _diff_attn_kernel(*refs, n_cast, tq, tk, hb, ones_rows, lam_init):
    lam_ref, sig_ref, sigp_ref, g_ref, kc_ref, qt_ref, k_ref, vt_ref = refs[:8]
    cast_in = refs[8:8 + n_cast]
    o_ref = refs[8 + n_cast]
    cast_out = refs[9 + n_cast:9 + 2 * n_cast]
    m_ref, acc_ref = refs[9 + 2 * n_cast:]
    for src, dst in zip(cast_in, cast_out):
        dst[...] = src[...].astype(dst.dtype)
    qi = pl.program_id(2)
    hd = HEAD_DIM
    half = hd // 2
    reps = 2 * tq // LANES

    lp = lam_ref[...]
    lam = (jnp.exp(jnp.sum(lp[0:1] * lp[1:2], axis=1, keepdims=True))
           - jnp.exp(jnp.sum(lp[2:3] * lp[3:4], axis=1, keepdims=True)) + lam_init)

    row = lax.broadcasted_iota(jnp.int32, (hd, tq), 0)
    qzts, sigs = [], []
    for hh in range(hb):
        qt = qt_ref[hh * hd:(hh + 1) * hd, :]
        zero = jnp.zeros_like(qt)
        top = jnp.concatenate([jnp.where(row < half, qt, zero), jnp.where(row >= half, qt, zero)], axis=1)
        ext = jnp.concatenate([sigp_ref[hh]] * reps, axis=1).astype(_BF16)
        qzts.append(jnp.concatenate([top, ext], axis=0))
        sigs.append(sig_ref[hh][:, :1])
    key_i = lax.broadcasted_iota(jnp.int32, (tq, 2 * tq), 0)
    qry_i = lax.broadcasted_iota(jnp.int32, (tq, 2 * tq), 1)
    causal = key_i <= jnp.where(qry_i >= tq, qry_i - tq, qry_i)

    m_ref[...] = jnp.full(m_ref.shape, NEG, _F32)
    acc_ref[...] = jnp.zeros(acc_ref.shape, _F32)

    def step(start, size, masked):
        off = (start - qi * tq).astype(_F32)
        kc = kc_ref[:size, :]
        ones = jnp.ones((ones_rows, size), _BF16)
        ps = []
        for hh in range(hb):
            k = jnp.concatenate([k_ref[pl.ds(start, size), hh * hd:(hh + 1) * hd], kc], axis=1)
            c = sigs[hh] * off
            s = jnp.dot(k, qzts[hh], preferred_element_type=_F32)
            if masked:
                s = jnp.where(causal, s, NEG)
            m_prev = m_ref[hh]
            m_new = jnp.maximum(m_prev, jnp.max(s, axis=0, keepdims=True) + c)
            alpha = jnp.exp2(m_prev - m_new)
            p = jnp.exp2(s - (m_new - c))
            m_ref[hh] = m_new
            ps.append((p.astype(_BF16), alpha))
        for hh in range(hb):
            p, alpha = ps[hh]
            vt = jnp.concatenate([vt_ref[hh * hd:(hh + 1) * hd, pl.ds(start, size)], ones], axis=0)
            acc_ref[hh] = acc_ref[hh] * alpha + jnp.dot(vt, p, preferred_element_type=_F32)

    nb = (qi * tq) // tk

    def body(j, carry):
        step(pl.multiple_of(2 * j * tk, tk), tk, False)
        step(pl.multiple_of((2 * j + 1) * tk, tk), tk, False)
        return carry

    lax.fori_loop(0, nb // 2, body, 0)

    @pl.when(nb % 2 == 1)
    def _():
        step(pl.multiple_of((nb - 1) * tk, tk), tk, False)
    if tk != tq:
        @pl.when(qi % (tk // tq) == 1)
        def _():
            step(pl.multiple_of((qi - 1) * tq, tq), tq, False)
    step(pl.multiple_of(qi * tq, tq), tq, True)

    for hh in range(hb):
        ot = acc_ref[hh, :hd, :] / acc_ref[hh, hd:hd + 1, :]
        odt = ot[:, :tq] - lam * ot[:, tq:]
        ms = jnp.mean(odt * odt, axis=0, keepdims=True)
        yt = odt * lax.rsqrt(ms + NORM_EPS) * g_ref[...] * (1.0 - lam_init)
        o_ref[:, hh * hd:(hh + 1) * hd] = yt.T.astype(o_ref.dtype)


def _bf16_pieces(x):
    hi = x.astype(_BF16).astype(np.float32)
    mid = (x - hi).astype(_BF16).astype(np.float32)
    lo = (x - hi - mid).astype(_BF16).astype(np.float32)
    return hi, mid, lo


def _diff_attention(qvt, knat, lam_params, subln_g, slopes, casts, *, batch, seq, n_heads, lam_init, tq, tk, hb):
    assert tk in (tq, 2 * tq)
    nq = seq // tq
    hd = HEAD_DIM
    ng = n_heads // hb
    sig = (slopes.astype(np.float64) * LOG2E).astype(np.float32)
    sig_arr = np.broadcast_to(sig.reshape(n_heads, 1, 1), (n_heads, 1, hd))
    sigp_arr = np.zeros((n_heads, hd, LANES), np.float32)
    for idx, piece in enumerate(_bf16_pieces(sig)):
        sigp_arr[:, idx, :] = piece[:, None]
        sigp_arr[:, 3 + idx, :] = piece[:, None]
    key = np.arange(tk)
    kc_arr = np.zeros((tk, LANES), np.float32)
    kc_arr[:, 0:3] = (key & 255)[:, None]
    kc_arr[:, 3:6] = (key - (key & 255))[:, None]
    ones_rows = 16
    steps = batch * ng * nq
    cast_2d = [c.reshape(-1, c.shape[-1]) for c in casts]
    assert all(c.shape[0] % (16 * steps) == 0 for c in cast_2d), "cast slabs must be whole bf16 sublane tiles"
    cast_specs = [pl.BlockSpec((c.shape[0] // steps, c.shape[1]), lambda b, h, i: ((b * ng + h) * nq + i, 0))
                  for c in cast_2d]
    kern = functools.partial(_diff_attn_kernel, n_cast=len(casts), tq=tq, tk=tk, hb=hb, ones_rows=ones_rows,
                             lam_init=lam_init)
    outs = pl.pallas_call(
        kern,
        grid=(batch, ng, nq),
        in_specs=[
            pl.BlockSpec(lam_params.shape, lambda b, h, i: (0, 0)),
            pl.BlockSpec((hb, 1, hd), lambda b, h, i: (h, 0, 0)),
            pl.BlockSpec((hb, hd, LANES), lambda b, h, i: (h, 0, 0)),
            pl.BlockSpec((hd, 1), lambda b, h, i: (0, 0)),
            pl.BlockSpec((tk, LANES), lambda b, h, i: (0, 0)),
            pl.BlockSpec((hb * hd, tq), lambda b, h, i: (h, b * nq + i)),
            pl.BlockSpec((seq, hb * hd), lambda b, h, i: (b, h)),
            pl.BlockSpec((hb * hd, seq), lambda b, h, i: (ng + h, b)),
        ] + cast_specs,
        out_specs=[pl.BlockSpec((tq, hb * hd), lambda b, h, i: (b * nq + i, h))] + cast_specs,
        out_shape=[jax.ShapeDtypeStruct((batch * seq, n_heads * hd), _BF16)]
        + [jax.ShapeDtypeStruct(c.shape, _BF16) for c in cast_2d],
        scratch_shapes=[
            pltpu.VMEM((hb, 1, 2 * tq), _F32),
            pltpu.VMEM((hb, hd + ones_rows, 2 * tq), _F32),
        ],
        compiler_params=_cparams(("parallel", "parallel", "arbitrary")),
        name="diff_attention",
    )(lam_params, jnp.asarray(sig_arr), jnp.asarray(sigp_arr), subln_g.reshape(hd, 1),
      jnp.asarray(kc_arr, dtype=_BF16), qvt, knat, qvt, *cast_2d)
    return outs[0], [o.reshape(c.shape) for o, c in zip(outs[1:], casts)]


def _rows(ref, n, st, start, size, sl):
    if len(ref.shape) == 2:
        return ref[start:start + size, sl]
    return ref[start // n:(start + size) // n, st * n:(st + 1) * n, sl].reshape(size, sl.stop - sl.start)


def _store_rows(ref, n, st, start, size, sl, val):
    if len(ref.shape) == 2:
        ref[start:start + size, sl] = val
    else:
        ref[start // n:(start + size) // n, st * n:(st + 1) * n, sl] = val.reshape(size // n, n, sl.stop - sl.start)


def _dilated_kernel(*refs, tq, span, n, streams, slopes2, scale2, has_prev):
    if has_prev:
        q_ref, kp_ref, kc_ref, vp_ref, vc_ref, o_ref, lse_ref = refs
    else:
        q_ref, kc_ref, vc_ref, o_ref, lse_ref = refs
    length = q_ref.shape[0] if len(q_ref.shape) == 2 else q_ref.shape[0] * n
    first = pl.program_id(2) == 0
    nh = len(slopes2)
    lane_grp = lax.broadcasted_iota(jnp.int32, (tq, LANES), 1) // (LANES // nh)

    def masked_bias(nk, shift, first_block):
        row = lax.broadcasted_iota(jnp.int32, (tq, nk), 0)
        col = lax.broadcasted_iota(jnp.int32, (tq, nk), 1)
        dist = row - col + shift
        valid = (dist >= 0) & (dist <= span)
        if first_block and has_prev:
            valid = valid & ((col >= span) | jnp.logical_not(first))
        distf = dist.astype(_F32)
        return [jnp.where(valid, -s2 * distf, NEG) for s2 in slopes2]

    bias_head = masked_bias(tq + span if has_prev else tq, span if has_prev else 0, True)
    bias_body = masked_bias(tq + span, span, False) if length > tq else None

    for st in range(streams):
        for qb in range(length // tq):
            u0 = qb * tq
            lses = []
            for hh in range(nh):
                sl = slice(hh * HEAD_DIM, (hh + 1) * HEAD_DIM)
                q = _rows(q_ref, n, st, u0, tq, sl)
                if qb == 0 and has_prev:
                    k = jnp.concatenate([kp_ref[:, sl], _rows(kc_ref, n, st, 0, tq, sl)], axis=0)
                    v = jnp.concatenate([vp_ref[:, sl], _rows(vc_ref, n, st, 0, tq, sl)], axis=0)
                elif qb == 0:
                    k, v = _rows(kc_ref, n, st, 0, tq, sl), _rows(vc_ref, n, st, 0, tq, sl)
                else:
                    k = _rows(kc_ref, n, st, u0 - span, tq + span, sl)
                    v = _rows(vc_ref, n, st, u0 - span, tq + span, sl)
                bias = (bias_head if qb == 0 else bias_body)[hh]
                s = lax.dot_general(q, k, _NT, preferred_element_type=_F32) * scale2 + bias
                m = jnp.max(s, axis=1, keepdims=True)
                p = jnp.exp2(s - m)
                l = jnp.sum(p, axis=1, keepdims=True)
                o = jnp.dot(p.astype(_BF16), v, preferred_element_type=_F32) / l
                _store_rows(o_ref, n, st, u0, tq, sl, o.astype(o_ref.dtype))
                lses.append(m + jnp.log2(l))
            packed = jnp.broadcast_to(lses[nh - 1], (tq, LANES))
            for hh in range(nh - 2, -1, -1):
                packed = jnp.where(lane_grp == hh, lses[hh], packed)
            _store_rows(lse_ref, n, st, u0, tq, slice(0, LANES), packed)


def _dilated_group(proj, slopes_g, *, batch, seq, group, window, dilation, in_width, mix_width):
    span = window // dilation
    assert span == LANES, "key window per stream must be one 128-row block"
    gw = HEADS_PER_GROUP * HEAD_DIM
    koff = mix_width // gw
    rows = batch * seq
    n = PERM_ROWS // dilation
    tq = span if n >= span else 2 * span
    streams = max(1, LANES // n) if dilation > 1 else 1
    kern = functools.partial(
        _dilated_kernel, tq=tq, span=span, n=n, streams=streams,
        slopes2=tuple(float(s) * dilation * LOG2E for s in slopes_g), scale2=HEAD_DIM ** -0.5 * LOG2E,
        has_prev=dilation == 1)
    if dilation == 1:
        chunk = 1024
        nc = seq // chunk
        cur = lambda c: pl.BlockSpec((chunk, gw), lambda b, r, u: (b * nc + u, c))
        prev = lambda c: pl.BlockSpec(
            (span, gw), lambda b, r, u: (b * (seq // span) + jnp.maximum(u * (chunk // span) - 1, 0), c))
        in_specs = [cur(group), prev(koff + group), cur(koff + group), prev(2 * koff + group), cur(2 * koff + group)]
        operands = [proj] * 5
        out_specs = [pl.BlockSpec((chunk, w), lambda b, r, u: (b * nc + u, 0)) for w in (gw, LANES)]
        out_dims = [(rows, gw), (rows, LANES)]
        grid = (batch, 1, nc)
    else:
        tiles = seq // PERM_ROWS
        proj3 = proj.reshape(rows // PERM_ROWS, PERM_ROWS, in_width)
        blk = lambda c, w: pl.BlockSpec((tiles, streams * n, w), lambda b, r, u: (b, r, c))
        in_specs = [blk(group, gw), blk(koff + group, gw), blk(2 * koff + group, gw)]
        operands = [proj3] * 3
        out_specs = [blk(0, gw), blk(0, LANES)]
        out_dims = [(rows // PERM_ROWS, PERM_ROWS, gw), (rows // PERM_ROWS, PERM_ROWS, LANES)]
        grid = (batch, dilation // streams, 1)
    o, lse = pl.pallas_call(
        kern,
        grid=grid,
        in_specs=in_specs,
        out_specs=out_specs,
        out_shape=[jax.ShapeDtypeStruct(out_dims[0], _BF16), jax.ShapeDtypeStruct(out_dims[1], _F32)],
        compiler_params=_cparams(("parallel", "parallel", "arbitrary")),
        name=f"dilated_attention_d{dilation}",
    )(*operands)
    return o.reshape(rows, gw), lse.reshape(rows, LANES)


def _combine_kernel(o0_ref, o1_ref, o2_ref, l0_ref, l1_ref, l2_ref, out_ref, on_ref, ln_ref, *, dilations):
    tm = out_ref.shape[0]
    gw = o0_ref.shape[1]
    nh = gw // LANES
    for g, (o_ref, l_ref, d) in enumerate(zip((o0_ref, o1_ref, o2_ref), (l0_ref, l1_ref, l2_ref), dilations)):
        n = PERM_ROWS // d
        if d == 1:
            ln_ref[g] = l_ref[...]
            for hh in range(nh):
                on_ref[g, hh] = o_ref[:, hh * LANES:(hh + 1) * LANES].astype(_F32)
            continue
        for grp in range(tm // PERM_ROWS):
            for r in range(d):
                src = slice(grp * PERM_ROWS + r * n, grp * PERM_ROWS + (r + 1) * n)
                dst = pl.ds(grp * PERM_ROWS + r, n, stride=d)
                ln_ref[g, dst, :] = l_ref[src, :]
                for hh in range(nh):
                    on_ref[g, hh, dst, :] = o_ref[src, hh * LANES:(hh + 1) * LANES].astype(_F32)
    l0, l1, l2 = ln_ref[0], ln_ref[1], ln_ref[2]
    mx = jnp.maximum(jnp.maximum(l0, l1), l2)
    es = (jnp.exp2(l0 - mx), jnp.exp2(l1 - mx), jnp.exp2(l2 - mx))
    den = es[0] + es[1] + es[2]
    for g in range(3):
        wg = es[g] / den
        for hh in range(nh):
            lane0 = hh * (LANES // nh)
            col = g * gw + hh * LANES
            out_ref[:, col:col + LANES] = (on_ref[g, hh] * wg[:, lane0:lane0 + 1]).astype(out_ref.dtype)


def _combine_groups(outs, lses, dilations, *, tm):
    m, gw = outs[0].shape
    spec = pl.BlockSpec((tm, gw), lambda i: (i, 0))
    return pl.pallas_call(
        functools.partial(_combine_kernel, dilations=dilations),
        grid=(m // tm,),
        in_specs=[spec] * 3 + [pl.BlockSpec((tm, LANES), lambda i: (i, 0))] * 3,
        out_specs=pl.BlockSpec((tm, 3 * gw), lambda i: (i, 0)),
        out_shape=jax.ShapeDtypeStruct((m, 3 * gw), _BF16),
        scratch_shapes=[pltpu.VMEM((3, gw // LANES, tm, LANES), _F32), pltpu.VMEM((3, tm, LANES), _F32)],
        compiler_params=_cparams(("parallel",)),
        name="combine_groups",
    )(*outs, *lses)


def _mem_attn_kernel(q_ref, k_ref, v_ref, o_ref, *, scale):
    for hh in range(N_MEM_HEADS):
        sl = slice(hh * HEAD_DIM, (hh + 1) * HEAD_DIM)
        s = lax.dot_general(q_ref[:, sl], k_ref[:, sl], _NT, preferred_element_type=_F32) * scale
        m = jnp.max(s, axis=1, keepdims=True)
        p = jnp.exp(s - m)
        l = jnp.sum(p, axis=1, keepdims=True)
        o = jnp.dot(p.astype(_BF16), v_ref[:, sl], preferred_element_type=_F32)
        o_ref[:, sl] = (o / l).astype(o_ref.dtype)


def _kv_proj_kernel(x_ref, g_ref, w_ref, o_ref):
    h = _rms(x_ref[...], g_ref[...]).astype(_BF16)
    o_ref[...] = jnp.dot(h, w_ref[...], preferred_element_type=_F32).astype(o_ref.dtype)


def _kv_proj(x, g, w):
    depth, k, n = w.shape
    m = x.shape[0]
    return pl.pallas_call(
        _kv_proj_kernel,
        grid=(depth,),
        in_specs=[
            pl.BlockSpec((m, k), lambda l: (0, 0)),
            pl.BlockSpec((1, k), lambda l: (0, 0)),
            pl.BlockSpec((None, k, n), lambda l: (l, 0, 0)),
        ],
        out_specs=pl.BlockSpec((None, m, n), lambda l: (l, 0, 0)),
        out_shape=jax.ShapeDtypeStruct((depth, m, n), _BF16),
        compiler_params=_cparams(("parallel",)),
        name="kv_proj",
    )(x, g.reshape(1, k), w)


def _memory_attention(proj, kvm, layer, *, batch, seq, mem_tokens, q_col_block, tq):
    nq = seq // tq
    mw = N_MEM_HEADS * HEAD_DIM
    return pl.pallas_call(
        functools.partial(_mem_attn_kernel, scale=HEAD_DIM ** -0.5),
        grid=(batch, nq),
        in_specs=[
            pl.BlockSpec((tq, mw), lambda b, i: (b * nq + i, q_col_block)),
            pl.BlockSpec((None, mem_tokens, mw), lambda b, i: (layer, b, 0)),
            pl.BlockSpec((None, mem_tokens, mw), lambda b, i: (layer, b, 1)),
        ],
        out_specs=pl.BlockSpec((tq, mw), lambda b, i: (b * nq + i, 0)),
        out_shape=jax.ShapeDtypeStruct((batch * seq, mw), _BF16),
        compiler_params=_cparams(("parallel", "parallel")),
        name="memory_attention",
    )(proj, kvm, kvm)


def _out_proj_kernel(x_ref, a_ref, b_ref, wa_ref, wb_ref, o_ref):
    acc = jnp.dot(a_ref[...], wa_ref[...], preferred_element_type=_F32)
    acc = acc + jnp.dot(b_ref[...], wb_ref[...], preferred_element_type=_F32)
    o_ref[...] = x_ref[...] + acc


def _out_proj(x, o_mix, o_mem, w, layer, *, tm):
    m, d = x.shape
    ka, kb = o_mix.shape[1], o_mem.shape[1]
    assert ka % kb == 0
    return pl.pallas_call(
        _out_proj_kernel,
        grid=(m // tm,),
        in_specs=[
            pl.BlockSpec((tm, d), lambda i: (i, 0)),
            pl.BlockSpec((tm, ka), lambda i: (i, 0)),
            pl.BlockSpec((tm, kb), lambda i: (i, 0)),
            pl.BlockSpec((None, ka, d), lambda i: (layer, 0, 0)),
            pl.BlockSpec((None, kb, d), lambda i: (layer, ka // kb, 0)),
        ],
        out_specs=pl.BlockSpec((tm, d), lambda i: (i, 0)),
        out_shape=jax.ShapeDtypeStruct((m, d), _F32),
        compiler_params=_cparams(("parallel",)),
        name="out_proj",
    )(x, o_mix, o_mem, w, w)


def _mlp_kernel(x_ref, g_ref, w1_ref, w2_ref, gf_ref, o_ref, hn_ref, *, final_norm):
    f = pl.program_id(1)

    def ffn(h):
        a = jnp.dot(h, w1_ref[...], preferred_element_type=_F32)
        a = jnp.square(jnp.maximum(a, 0.0)).astype(_BF16)
        return jnp.dot(a, w2_ref[...], preferred_element_type=_F32)

    @pl.when(f == 0)
    def _():
        x = x_ref[...]
        h = _rms(x, g_ref[...]).astype(hn_ref.dtype)
        hn_ref[...] = h
        o_ref[...] = x + ffn(h)

    last = pl.num_programs(1) - 1

    @pl.when((f > 0) & (f < last) if final_norm else f > 0)
    def _():
        o_ref[...] += ffn(hn_ref[...])

    if final_norm:
        @pl.when(f == last)
        def _():
            o_ref[...] = _rms(o_ref[...] + ffn(hn_ref[...]), gf_ref[...])


def _mlp(x, g, w1, w2, layer, g_final, *, final_norm, tm, tf):
    m, d = x.shape
    ff = w1.shape[2]
    return pl.pallas_call(
        functools.partial(_mlp_kernel, final_norm=final_norm),
        grid=(m // tm, ff // tf),
        in_specs=[
            pl.BlockSpec((tm, d), lambda i, f: (i, 0)),
            pl.BlockSpec((1, d), lambda i, f: (0, 0)),
            pl.BlockSpec((None, d, tf), lambda i, f: (layer, 0, f)),
            pl.BlockSpec((None, tf, d), lambda i, f: (layer, f, 0)),
            pl.BlockSpec((1, d), lambda i, f: (0, 0)),
        ],
        out_specs=pl.BlockSpec((tm, d), lambda i, f: (i, 0)),
        out_shape=jax.ShapeDtypeStruct((m, d), _F32),
        scratch_shapes=[pltpu.VMEM((tm, d), _BF16)],
        compiler_params=_cparams(("parallel", "arbitrary")),
        name="mlp",
    )(x, g.reshape(1, d), w1, w2, g_final.reshape(1, d))


def kernel(x, mem, g_attn, w_in, w_out, lambda_qk, diff_subln_g, g_mem, w_mem_kv, g_mlp, w_mlp1, w_mlp2, g_final):
    batch, seq, d_model = x.shape
    depth = w_in.shape[0]
    mem_tokens = mem.shape[1]
    in_width = w_in.shape[2]
    mem_width = N_MEM_HEADS * HEAD_DIM
    mix_width = (in_width - mem_width) // 3
    n_heads = mix_width // HEAD_DIM
    slopes = _alibi_slopes(n_heads)
    dilations = tuple(d for _, d in DILATED_GROUPS)
    gw = HEADS_PER_GROUP * HEAD_DIM
    n_qkv_tiles = 3 * mix_width // gw

    xf = x.reshape(batch * seq, d_model)
    memf = mem.reshape(batch * mem_tokens, d_model)
    mix_tiles = mix_width // gw
    q_scale = LOG2E * (HEAD_DIM // 2) ** -0.5
    for i in range(depth):
        if i % N_MIXERS == 0:
            j = i // N_MIXERS
            lam_init = 0.8 - 0.6 * math.exp(-0.3 * i)
            k_qm_tiles = [mix_tiles + t for t in range(mix_tiles)] + [3 * mix_tiles]
            q_v_tiles = list(range(mix_tiles)) + [2 * mix_tiles + t for t in range(mix_tiles)]
            if i == 0:
                w_nat = _cast_tiles(w_in, i, k_qm_tiles, tn=gw, transpose=False)[None]
                nat_cols, layer = list(range(mix_tiles + 1)), 0
            else:
                w_nat, nat_cols, layer = w_in_b, k_qm_tiles, i
            proj, qvt = _norm_matmul_dual(
                xf, g_attn[i], w_nat, layer, _cast_tiles(w_in, i, q_v_tiles, tn=gw, transpose=True),
                tm=1024, tn=gw, nat_cols=nat_cols,
                t_rows=[(t, q_scale) for t in range(mix_tiles)] + [(mix_tiles + t, 1.0) for t in range(mix_tiles)])
            casts = [w_mem_kv, w_out, w_mlp1, w_mlp2, w_in] if i == 0 else []
            o_mix, casted = _diff_attention(qvt, proj, lambda_qk[j], diff_subln_g[j], slopes, casts, batch=batch,
                                            seq=seq, n_heads=n_heads, lam_init=lam_init, tq=256, tk=512, hb=6)
            if i == 0:
                w_kv_b, w_out_b, w1_b, w2_b, w_in_b = casted
            qm_col_block = mix_tiles
        else:
            tile_slot = [t % len(dilations) if t < n_qkv_tiles else 0 for t in range(in_width // gw)]
            proj = _norm_matmul(xf, g_attn[i], w_in_b, i, tm=1024, tn=2 * gw, dilations=dilations,
                                tile_slot=tile_slot, slot_width=gw)
            outs, lses = [], []
            for g, (window, dilation) in enumerate(DILATED_GROUPS):
                sl = slice(g * HEADS_PER_GROUP, (g + 1) * HEADS_PER_GROUP)
                o, lse = _dilated_group(proj, slopes[sl], batch=batch, seq=seq, group=g, window=window,
                                        dilation=dilation, in_width=in_width, mix_width=mix_width)
                outs.append(o)
                lses.append(lse)
            o_mix = _combine_groups(outs, lses, dilations, tm=2 * PERM_ROWS)
            qm_col_block = 3 * mix_tiles
        if i == 0:
            kvm = _kv_proj(memf, g_mem, w_kv_b)
        o_mem = _memory_attention(proj, kvm, i, batch=batch, seq=seq, mem_tokens=mem_tokens,
                                  q_col_block=qm_col_block, tq=2048)
        xf = _out_proj(xf, o_mix, o_mem, w_out_b, i, tm=512)
        xf = _mlp(xf, g_mlp[i], w1_b, w2_b, i, g_final, final_norm=(i == depth - 1), tm=1024, tf=512)
    return xf.reshape(batch, seq, d_model)
```
